```python
import jax
import jax.numpy as jnp
from jax import lax
import numpy as np

D_MODEL = 1024
BATCH = 2
SEQ = 8192
DEPTH = 4

GRID_W = 64
CTX_LEN = 256
N_MIXERS = 3
N_MOD = 6
RMS_EPS = 1e-6
LN_EPS = 1e-5

GDN_HEADS = 8
GDN_DK = 128
GDN_DV = 128
GDN_CONV = 5
GDN_CHUNK = 64

CONF_CONV = 31

MLA_HEADS = 8
MLA_Q_RANK = 512
MLA_KV_RANK = 256
MLA_NOPE = 128
MLA_ROPE = 64
MLA_V = 128
ROPE_THETA = 10000.0
Q_BLOCK = 128

FFN_DENSE = 2816
N_EXPERTS = 8
TOP_K = 2
FFN_EXPERT = 1408

kernel_name = 'hybrid_gdn_conformer_mla_moe_dit'


def rms_norm(x, g):
    xf = x.astype(jnp.float32)
    y = xf * lax.rsqrt(jnp.mean(xf * xf, axis=-1, keepdims=True) + RMS_EPS)
    return (y * g.astype(jnp.float32)).astype(x.dtype)


def layer_norm(x, g, b):
    xf = x.astype(jnp.float32)
    xc = xf - jnp.mean(xf, axis=-1, keepdims=True)
    y = xc * lax.rsqrt(jnp.mean(xc * xc, axis=-1, keepdims=True) + LN_EPS)
    return (y * g.astype(jnp.float32) + b.astype(jnp.float32)).astype(x.dtype)


def l2_normalize(x):
    xf = x.astype(jnp.float32)
    return (xf * lax.rsqrt(jnp.sum(xf * xf, axis=-1, keepdims=True) + RMS_EPS)).astype(x.dtype)


def modulate(h, shift, scale):
    return h * (1 + scale) + shift


def depthwise_conv(x, w):
    pad = w.shape[0] // 2
    return lax.conv_general_dilated(
        x, w[:, None, :].astype(x.dtype), window_strides=(1,), padding=[(pad, pad)],
        dimension_numbers=('NWC', 'WIO', 'NWC'), feature_group_count=x.shape[-1])


def swiglu(h, w_gate, w_up, w_down):
    return (jax.nn.silu(h @ w_gate) * (h @ w_up)) @ w_down


def moe_swiglu(h, w_router, w_gate, w_up, w_down):
    logits = (h @ w_router).astype(jnp.float32)
    top_val, top_idx = lax.top_k(logits, TOP_K)
    top_w = jax.nn.softmax(top_val, axis=-1)
    gates = jnp.einsum('blk,blke->ble', top_w,
                       jax.nn.one_hot(top_idx, N_EXPERTS, dtype=jnp.float32)).astype(h.dtype)
    out = jnp.zeros_like(h)
    for e in range(N_EXPERTS):
        out = out + gates[..., e:e + 1] * swiglu(h, w_gate[e], w_up[e], w_down[e])
    return out


def chunk_gated_delta(q, k, v, g, beta, state):
    b, l, h, dk = q.shape
    dv = v.shape[-1]
    n = l // GDN_CHUNK

    def chunks(t):
        t = t.astype(jnp.float32).reshape((b, n, GDN_CHUNK) + t.shape[2:])
        return jnp.moveaxis(jnp.moveaxis(t, 1, 0), 3, 2)

    qc = chunks(q) * (dk ** -0.5)
    kc, vc, bc = chunks(k), chunks(v), chunks(beta)
    gc = jnp.cumsum(chunks(g), axis=-1)
    idx = jnp.arange(GDN_CHUNK)
    causal = idx[:, None] >= idx[None, :]
    strict = (idx[:, None] > idx[None, :]).astype(jnp.float32)
    decay = jnp.exp(jnp.where(causal, gc[..., :, None] - gc[..., None, :], -jnp.inf))
    kb = kc * bc[..., None]
    lower = jnp.einsum('nbhcd,nbhed->nbhce', kb, kc) * decay * strict
    system = lower + jnp.eye(GDN_CHUNK, dtype=jnp.float32)
    rhs = jnp.concatenate([vc * bc[..., None], kb * jnp.exp(gc)[..., None]], axis=-1)
    sol = lax.linalg.triangular_solve(system, rhs, left_side=True, lower=True, unit_diagonal=True)
    u, w = sol[..., :dv], sol[..., dv:]
    attn = jnp.einsum('nbhcd,nbhed->nbhce', qc, kc) * decay
    q_dec = qc * jnp.exp(gc)[..., None]
    k_dec = kc * jnp.exp(gc[..., -1:] - gc)[..., None]
    g_last = jnp.exp(gc[..., -1])

    def step(s, xs):
        q_i, k_i, u_i, w_i, a_i, gl_i = xs
        v_new = u_i - jnp.einsum('bhcd,bhde->bhce', w_i, s)
        o = jnp.einsum('bhcd,bhde->bhce', q_i, s) + jnp.einsum('bhce,bhef->bhcf', a_i, v_new)
        s = s * gl_i[..., None, None] + jnp.einsum('bhcd,bhce->bhde', k_i, v_new)
        return s, o

    state, out = lax.scan(step, state.astype(jnp.float32), (q_dec, k_dec, u, w, attn, g_last))
    out = jnp.moveaxis(jnp.moveaxis(out, 2, 3), 0, 1).reshape(b, l, h, dv)
    return out.astype(v.dtype), state


def gdn_mixer(h_ctx, h_lat, w_in, conv_w, a_log, dt_bias, o_norm, w_out):
    n_qk = GDN_HEADS * GDN_DK
    n_v = GDN_HEADS * GDN_DV

    def project(h):
        b, l, _ = h.shape
        p = h @ w_in
        qkv = jax.nn.silu(depthwise_conv(p[..., :2 * n_qk + n_v], conv_w))
        q = l2_normalize(qkv[..., :n_qk].reshape(b, l, GDN_HEADS, GDN_DK))
        k = l2_normalize(qkv[..., n_qk:2 * n_qk].reshape(b, l, GDN_HEADS, GDN_DK))
        v = qkv[..., 2 * n_qk:].reshape(b, l, GDN_HEADS, GDN_DV)
        gate = p[..., 2 * n_qk + n_v:2 * n_qk + 2 * n_v]
        ab = p[..., 2 * n_qk + 2 * n_v:].astype(jnp.float32)
        beta = jax.nn.sigmoid(ab[..., :2 * GDN_HEADS]).reshape(b, l, 2, GDN_HEADS)
        a = ab[..., 2 * GDN_HEADS:].reshape(b, l, 2, GDN_HEADS)
        logdecay = -jnp.exp(a_log.astype(jnp.float32)) * jax.nn.softplus(a + dt_bias.astype(jnp.float32))
        return (q, k, v, logdecay, beta), gate

    (pc, gate_c), (pl, gate_l) = project(h_ctx), project(h_lat)
    zero = jnp.zeros((h_lat.shape[0], GDN_HEADS, GDN_DK, GDN_DV), jnp.float32)

    def run(parts, d, state):
        q, k, v, gd, bd = parts
        rev = (lambda t: t) if d == 0 else (lambda t: jnp.flip(t, axis=1))
        o, s = chunk_gated_delta(rev(q), rev(k), rev(v), rev(gd[:, :, d]), rev(bd[:, :, d]), state)
        return rev(o), s

    oc_f, sc_f = run(pc, 0, zero)
    oc_b, sc_b = run(pc, 1, zero)
    ol_f, _ = run(pl, 0, sc_f)
    ol_b, _ = run(pl, 1, sc_b)

    def finish(o, gate):
        b, l = o.shape[:2]
        y = rms_norm(o, o_norm) * jax.nn.silu(gate.reshape(b, l, GDN_HEADS, GDN_DV))
        return y.reshape(b, l, n_v) @ w_out

    return finish(oc_f + oc_b, gate_c), finish(ol_f + ol_b, gate_l)


def conformer_conv(h, w_in, b_in, dw, dw_b, ln_g, ln_b, w_out, b_out):
    a, gt = jnp.split(h @ w_in + b_in, 2, axis=-1)
    y = depthwise_conv(a * jax.nn.sigmoid(gt), dw) + dw_b
    y = jax.nn.silu(layer_norm(y, ln_g, ln_b))
    return y @ w_out + b_out


def axial_rope_tables(length):
    rows = length // GRID_W
    row = jnp.repeat(jnp.arange(rows, dtype=jnp.float32), GRID_W)
    col = jnp.tile(jnp.arange(GRID_W, dtype=jnp.float32), rows)
    n_freq = MLA_ROPE // 4
    inv = ROPE_THETA ** (-jnp.arange(n_freq, dtype=jnp.float32) / n_freq)
    ang = jnp.stack([row[:, None] * inv, col[:, None] * inv], axis=1)
    return jnp.cos(ang), jnp.sin(ang)


def apply_rope(x, cos, sin):
    xr = x.astype(jnp.float32).reshape(x.shape[:-1] + (2, 2, MLA_ROPE // 4))
    x1, x2 = xr[..., 0, :], xr[..., 1, :]
    out = jnp.stack([x1 * cos - x2 * sin, x1 * sin + x2 * cos], axis=-2)
    return out.reshape(x.shape).astype(x.dtype)


def mla_attend(qn, qr, kn, kr, v):
    scale = (MLA_NOPE + MLA_ROPE) ** -0.5
    s = jnp.einsum('bqhd,bkhd->bhqk', qn, kn) + jnp.einsum('bqhr,bkr->bhqk', qr, kr)
    p = jax.nn.softmax(s.astype(jnp.float32) * scale, axis=-1).astype(v.dtype)
    return jnp.einsum('bhqk,bkhd->bqhd', p, v)


def mla_mixer(h_ctx, h_lat, rope, w_in, q_norm, kv_norm, w_uq, w_ukv, w_o, with_ctx_out):
    def project(h, tables):
        b, l, _ = h.shape
        p = h @ w_in
        cq = p[..., :MLA_Q_RANK]
        ckv = p[..., MLA_Q_RANK:MLA_Q_RANK + MLA_KV_RANK]
        kr = p[..., MLA_Q_RANK + MLA_KV_RANK:]
        q = (rms_norm(cq, q_norm) @ w_uq).reshape(b, l, MLA_HEADS, MLA_NOPE + MLA_ROPE)
        kv = (rms_norm(ckv, kv_norm) @ w_ukv).reshape(b, l, MLA_HEADS, MLA_NOPE + MLA_V)
        qn, qr = q[..., :MLA_NOPE], q[..., MLA_NOPE:]
        kn, v = kv[..., :MLA_NOPE], kv[..., MLA_NOPE:]
        if tables is not None:
            cos, sin = tables
            qr = apply_rope(qr, cos[:, None], sin[:, None])
            kr = apply_rope(kr, cos, sin)
        return qn, qr, kn, kr, v

    qn_c, qr_c, kn_c, kr_c, v_c = project(h_ctx, None)
    qn_l, qr_l, kn_l, kr_l, v_l = project(h_lat, rope)
    b, l = h_lat.shape[:2]
    kn_all = jnp.concatenate([kn_c, kn_l], axis=1)
    kr_all = jnp.concatenate([kr_c, kr_l], axis=1)
    v_all = jnp.concatenate([v_c, v_l], axis=1)
    nblk = l // Q_BLOCK

    def blocks(t):
        return jnp.moveaxis(t.reshape((b, nblk, Q_BLOCK) + t.shape[2:]), 1, 0)

    o_l = lax.map(lambda qs: mla_attend(qs[0], qs[1], kn_all, kr_all, v_all), (blocks(qn_l), blocks(qr_l)))
    o_l = jnp.moveaxis(o_l, 0, 1).reshape(b, l, MLA_HEADS * MLA_V) @ w_o
    o_c = None
    if with_ctx_out:
        o_c = mla_attend(qn_c, qr_c, kn_c, kr_c, v_c).reshape(b, -1, MLA_HEADS * MLA_V) @ w_o
    return o_c, o_l


def setup_inputs(seed: int = 0) -> dict:
    key = jax.random.key(seed)
    ks = jax.random.split(key, 40)
    counter = iter(range(40))
    n_a = len(range(0, DEPTH, N_MIXERS))
    n_b = len(range(1, DEPTH, N_MIXERS))
    n_c = len(range(2, DEPTH, N_MIXERS))
    n_dense = len(range(0, DEPTH, 2))
    n_moe = len(range(1, DEPTH, 2))

    def nk():
        return ks[next(counter)]

    def normal(shape, scale):
        return jax.random.normal(nk(), shape, jnp.float32) * scale

    def gain(shape):
        return 1.0 + normal(shape, 0.02)

    d = D_MODEL
    gdn_in = 2 * GDN_HEADS * GDN_DK + 2 * GDN_HEADS * GDN_DV + 4 * GDN_HEADS
    gdn_conv_ch = 2 * GDN_HEADS * GDN_DK + GDN_HEADS * GDN_DV
    dt = jnp.exp(jax.random.uniform(nk(), (n_a, 2, GDN_HEADS), jnp.float32,
                                    minval=float(np.log(1e-3)), maxval=float(np.log(1e-1))))
    inp = {
        'x': normal((BATCH, SEQ, d), 1.0),
        'c': normal((BATCH, d), 1.0),
        'ctx': normal((BATCH, CTX_LEN, d), 1.0),
        'c_ctx': normal((d,), 1.0),
        'w_mod': normal((DEPTH, d, N_MOD * d), 0.5 * d ** -0.5),
        'b_mod': normal((DEPTH, N_MOD * d), 0.02),
        'norm_g': gain((DEPTH, 4, d)),
        'gdn_w_in': normal((n_a, d, gdn_in), d ** -0.5),
        'gdn_conv': normal((n_a, GDN_CONV, gdn_conv_ch), GDN_CONV ** -0.5),
        'gdn_a_log': jnp.log(jax.random.uniform(nk(), (n_a, 2, GDN_HEADS), jnp.float32, minval=1.0, maxval=16.0)),
        'gdn_dt_bias': dt + jnp.log(-jnp.expm1(-dt)),
        'gdn_o_norm': gain((n_a, GDN_DV)),
        'gdn_w_out': normal((n_a, GDN_HEADS * GDN_DV, d), (GDN_HEADS * GDN_DV) ** -0.5),
        'conf_w_in': normal((n_b, d, 2 * d), d ** -0.5),
        'conf_b_in': normal((n_b, 2 * d), 0.02),
        'conf_dw': normal((n_b, CONF_CONV, d), CONF_CONV ** -0.5),
        'conf_dw_b': normal((n_b, d), 0.02),
        'conf_ln_g': gain((n_b, d)),
        'conf_ln_b': normal((n_b, d), 0.02),
        'conf_w_out': normal((n_b, d, d), d ** -0.5),
        'conf_b_out': normal((n_b, d), 0.02),
        'mla_w_in': normal((n_c, d, MLA_Q_RANK + MLA_KV_RANK + MLA_ROPE), d ** -0.5),
        'mla_q_norm': gain((n_c, MLA_Q_RANK)),
        'mla_kv_norm': gain((n_c, MLA_KV_RANK)),
        'mla_w_uq': normal((n_c, MLA_Q_RANK, MLA_HEADS * (MLA_NOPE + MLA_ROPE)), MLA_Q_RANK ** -0.5),
        'mla_w_ukv': normal((n_c, MLA_KV_RANK, MLA_HEADS * (MLA_NOPE + MLA_V)), MLA_KV_RANK ** -0.5),
        'mla_w_o': normal((n_c, MLA_HEADS * MLA_V, d), (MLA_HEADS * MLA_V) ** -0.5),
        'ffn_w_gate': normal((n_dense, d, FFN_DENSE), d ** -0.5),
        'ffn_w_up': normal((n_dense, d, FFN_DENSE), d ** -0.5),
        'ffn_w_down': normal((n_dense, FFN_DENSE, d), FFN_DENSE ** -0.5),
        'moe_router': normal((n_moe, d, N_EXPERTS), d ** -0.5),
        'moe_w_gate': normal((n_moe, N_EXPERTS, d, FFN_EXPERT), d ** -0.5),
        'moe_w_up': normal((n_moe, N_EXPERTS, d, FFN_EXPERT), d ** -0.5),
        'moe_w_down': normal((n_moe, N_EXPERTS, FFN_EXPERT, d), FFN_EXPERT ** -0.5),
    }
    return inp


def reference(x, c, ctx, c_ctx, w_mod, b_mod, norm_g,
              gdn_w_in, gdn_conv, gdn_a_log, gdn_dt_bias, gdn_o_norm, gdn_w_out,
              conf_w_in, conf_b_in, conf_dw, conf_dw_b, conf_ln_g, conf_ln_b, conf_w_out, conf_b_out,
              mla_w_in, mla_q_norm, mla_kv_norm, mla_w_uq, mla_w_ukv, mla_w_o,
              ffn_w_gate, ffn_w_up, ffn_w_down,
              moe_router, moe_w_gate, moe_w_up, moe_w_down):
    rope = axial_rope_tables(x.shape[1])
    h_lat, h_ctx = x, ctx
    for i in range(DEPTH):
        last = i == DEPTH - 1
        mod_l = jnp.split((jax.nn.silu(c) @ w_mod[i] + b_mod[i])[:, None, :], N_MOD, axis=-1)
        mod_c = jnp.split((jax.nn.silu(c_ctx) @ w_mod[i] + b_mod[i])[None, None, :], N_MOD, axis=-1)
        a_l = modulate(rms_norm(h_lat, norm_g[i, 0]), mod_l[0], mod_l[1])
        a_c = modulate(rms_norm(h_ctx, norm_g[i, 0]), mod_c[0], mod_c[1])
        kind, j = i % N_MIXERS, i // N_MIXERS
        if kind == 0:
            o_c, o_l = gdn_mixer(a_c, a_l, gdn_w_in[j], gdn_conv[j], gdn_a_log[j], gdn_dt_bias[j],
                                 gdn_o_norm[j], gdn_w_out[j])
        elif kind == 1:
            conf = (conf_w_in[j], conf_b_in[j], conf_dw[j], conf_dw_b[j], conf_ln_g[j], conf_ln_b[j],
                    conf_w_out[j], conf_b_out[j])
            o_l = conformer_conv(a_l, *conf)
            o_c = None if last else conformer_conv(a_c, *conf)
        else:
            o_c, o_l = mla_mixer(a_c, a_l, rope, mla_w_in[j], mla_q_norm[j], mla_kv_norm[j],
                                 mla_w_uq[j], mla_w_ukv[j], mla_w_o[j], not last)

        def channel_mixer(h):
            f = i // 2
            if i % 2 == 0:
                return swiglu(h, ffn_w_gate[f], ffn_w_up[f], ffn_w_down[f])
            return moe_swiglu(h, moe_router[f], moe_w_gate[f], moe_w_up[f], moe_w_down[f])

        h_lat = h_lat + mod_l[2] * rms_norm(o_l, norm_g[i, 1])
        f_l = channel_mixer(modulate(rms_norm(h_lat, norm_g[i, 2]), mod_l[3], mod_l[4]))
        h_lat = h_lat + mod_l[5] * rms_norm(f_l, norm_g[i, 3])
        if not last:
            h_ctx = h_ctx + mod_c[2] * rms_norm(o_c, norm_g[i, 1])
            f_c = channel_mixer(modulate(rms_norm(h_ctx, norm_g[i, 2]), mod_c[3], mod_c[4]))
            h_ctx = h_ctx + mod_c[5] * rms_norm(f_c, norm_g[i, 3])
    return h_lat
```

```python
import functools

import jax
import jax.numpy as jnp
from jax import lax
from jax.experimental import pallas as pl
from jax.experimental.pallas import tpu as pltpu

F32 = jnp.float32
BF16 = jnp.bfloat16
HIGHEST = lax.Precision.HIGHEST

N_MOD = 6
RMS_EPS = 1e-6
LN_EPS = 1e-5
GRID_W = 64
GDN_HEADS = 8
GDN_DK = 128
GDN_DV = 128
GDN_CONV = 5
GDN_CHUNK = 64
CONF_CONV = 31
MLA_HEADS = 8
MLA_NOPE = 128
MLA_ROPE = 64
MLA_V = 128
MLA_Q_RANK = 512
MLA_KV_RANK = 256
ROPE_THETA = 10000.0
N_EXPERTS = 8

LANES = 128
SUBLANES = 8
TM = 256
HALO = 16
VMEM_LIMIT = 56 * 1024 * 1024


def _cparams(sem):
    return pltpu.CompilerParams(dimension_semantics=sem, vmem_limit_bytes=VMEM_LIMIT)


def _dot(a, b, **kw):
    return jnp.dot(a, b, preferred_element_type=F32, **kw)


def _dot_nt(a, b):
    return lax.dot_general(a, b, (((1,), (1,)), ((), ())), preferred_element_type=F32)


def _dot_tn(a, b):
    return lax.dot_general(a, b, (((0,), (0,)), ((), ())), preferred_element_type=F32)


def _silu(x):
    return x * jax.nn.sigmoid(x)


def _rms(x, eps=RMS_EPS):
    return x * lax.rsqrt(jnp.mean(x * x, axis=-1, keepdims=True) + eps)


def _norm_mod(h, g, shift, scale):
    return _rms(h) * g * (1.0 + scale) + shift


def _residual(h, o, g, gate):
    return h + gate * (_rms(o) * g)


def _tok_spec(c):
    return pl.BlockSpec((None, TM, c), lambda b, j: (b, j, 0))


def _full_spec(shape):
    nd = len(shape)
    return pl.BlockSpec(shape, lambda *_: (0,) * nd)


def _mod_spec(nt, d):
    return pl.BlockSpec((None, SUBLANES, d), lambda b, j: (jnp.where(j == nt - 1, 0, b + 1), 0, 0))


def _halo_specs(nt, c):
    per = TM // HALO
    prev = pl.BlockSpec((None, HALO, c), lambda b, j: (b, jnp.maximum(j * per - 1, 0), 0))
    nxt = pl.BlockSpec((None, HALO, c), lambda b, j: (b, jnp.minimum((j + 1) * per, nt * per - 1), 0))
    return prev, nxt


def _halo_valid(nt):
    j = pl.program_id(1)
    prev_ok = jnp.logical_and(j != 0, j != nt - 1)
    next_ok = j < nt - 2
    return prev_ok, next_ok


def _adaln_kernel(c_ref, w_ref, b_ref, o_ref):
    o_ref[...] = _dot(_silu(c_ref[...]), w_ref[...], precision=HIGHEST) + b_ref[...]


def _adaln(cvec, w_mod, b_mod):
    depth, d, n = w_mod.shape
    tn = 1536
    return pl.pallas_call(
        _adaln_kernel,
        grid=(depth, n // tn),
        in_specs=[pl.BlockSpec((SUBLANES, d), lambda i, k: (0, 0)),
                  pl.BlockSpec((None, d, tn), lambda i, k: (i, 0, k)),
                  pl.BlockSpec((None, 1, tn), lambda i, k: (i, 0, k))],
        out_specs=pl.BlockSpec((None, SUBLANES, tn), lambda i, k: (i, 0, k)),
        out_shape=jax.ShapeDtypeStruct((depth, SUBLANES, n), F32),
        compiler_params=_cparams(("parallel", "parallel")),
        name="adaln",
    )(cvec, w_mod, b_mod.reshape(depth, 1, n))


def _gdn_proj_kernel(h_ref, mod_ref, g_ref, w_ref, wab_ref, qkv_ref, gate_ref, ab_ref):
    a = _norm_mod(h_ref[...], g_ref[...], mod_ref[0:1, :], mod_ref[1:2, :])
    a16 = a.astype(BF16)
    nq = qkv_ref.shape[-1]
    for n in range(nq // 512):
        qkv_ref[:, n * 512:(n + 1) * 512] = _dot(a16, w_ref[:, n * 512:(n + 1) * 512]).astype(BF16)
    for n in range(gate_ref.shape[-1] // 512):
        gate_ref[:, n * 512:(n + 1) * 512] = _dot(a16, w_ref[:, nq + n * 512:nq + (n + 1) * 512]).astype(BF16)
    for dr in range(2):
        ab_ref[dr] = _dot(a, wab_ref[dr], precision=HIGHEST)


def _gdn_proj(h, mod, g, w_main, w_ab):
    b, t, d = h.shape
    nt = t // TM
    nq = 2 * GDN_HEADS * GDN_DK + GDN_HEADS * GDN_DV
    nv = GDN_HEADS * GDN_DV
    return pl.pallas_call(
        _gdn_proj_kernel,
        grid=(b, nt),
        in_specs=[_tok_spec(d), _mod_spec(nt, d), _full_spec((1, d)),
                  _full_spec(w_main.shape), _full_spec(w_ab.shape)],
        out_specs=[_tok_spec(nq), _tok_spec(nv),
                   pl.BlockSpec((None, 2, TM, LANES), lambda bb, j: (bb, 0, j, 0))],
        out_shape=[jax.ShapeDtypeStruct((b, t, nq), BF16),
                   jax.ShapeDtypeStruct((b, t, nv), BF16),
                   jax.ShapeDtypeStruct((b, 2, t, LANES), F32)],
        compiler_params=_cparams(("parallel", "parallel")),
        name="gdn_proj",
    )(h, mod, g, w_main, w_ab)


def _gdn_conv_kernel(nt, cur_ref, prev_ref, next_ref, w_ref, q_ref, k_ref, v_ref, ext_ref):
    prev_ok, next_ok = _halo_valid(nt)
    cw = 512
    pad = GDN_CONV // 2
    nqk = GDN_HEADS * GDN_DK
    for cc in range(cur_ref.shape[-1] // cw):
        cs = slice(cc * cw, (cc + 1) * cw)
        ext_ref[0:SUBLANES, :] = jnp.where(prev_ok, prev_ref[HALO - SUBLANES:HALO, cs].astype(F32), 0.0)
        ext_ref[SUBLANES:SUBLANES + TM, :] = cur_ref[:, cs].astype(F32)
        ext_ref[SUBLANES + TM:2 * SUBLANES + TM, :] = jnp.where(next_ok, next_ref[0:SUBLANES, cs].astype(F32), 0.0)
        acc = w_ref[0:1, cs] * ext_ref[pl.ds(SUBLANES - pad, TM), :]
        for tap in range(1, GDN_CONV):
            acc = acc + w_ref[tap:tap + 1, cs] * ext_ref[pl.ds(SUBLANES - pad + tap, TM), :]
        y = _silu(acc)
        for hh in range(cw // LANES):
            col = cc * cw + hh * LANES
            yh = y[:, hh * LANES:(hh + 1) * LANES]
            if col < 2 * nqk:
                yh = yh * lax.rsqrt(jnp.sum(yh * yh, axis=-1, keepdims=True) + RMS_EPS)
            if col < nqk:
                q_ref[:, col:col + LANES] = yh.astype(BF16)
            elif col < 2 * nqk:
                k_ref[:, col - nqk:col - nqk + LANES] = yh.astype(BF16)
            else:
                v_ref[:, col - 2 * nqk:col - 2 * nqk + LANES] = yh.astype(BF16)


def _gdn_conv(qkv, conv_w):
    b, t, c = qkv.shape
    nt = t // TM
    nqk = GDN_HEADS * GDN_DK
    prev, nxt = _halo_specs(nt, c)
    return pl.pallas_call(
        functools.partial(_gdn_conv_kernel, nt),
        grid=(b, nt),
        in_specs=[_tok_spec(c), prev, nxt, _full_spec(conv_w.shape)],
        out_specs=[_tok_spec(nqk), _tok_spec(nqk), _tok_spec(GDN_HEADS * GDN_DV)],
        out_shape=[jax.ShapeDtypeStruct((b, t, nqk), BF16),
                   jax.ShapeDtypeStruct((b, t, nqk), BF16),
                   jax.ShapeDtypeStruct((b, t, GDN_HEADS * GDN_DV), BF16)],
        scratch_shapes=[pltpu.VMEM((TM + 2 * SUBLANES, 512), F32)],
        compiler_params=_cparams(("parallel", "parallel")),
        name="gdn_conv",
    )(qkv, qkv, qkv, conv_w)


def _gdn_scan_kernel(reverse, q_ref, k_ref, v_ref, ab_ref, cf_ref, o_ref,
                     s_ref, gc_ref, gct_ref, beta_ref):
    nchunk = TM // GDN_CHUNK
    c64 = GDN_CHUNK
    scale = GDN_DK ** -0.5

    @pl.when(pl.program_id(1) == 0)
    def _():
        s_ref[...] = jnp.zeros_like(s_ref)

    ab = ab_ref[...]
    beta_ref[...] = jax.nn.sigmoid(ab)
    x = ab + cf_ref[1:2, :]
    softplus = jnp.maximum(x, 0.0) + jnp.log1p(jnp.exp(-jnp.abs(x)))
    g = -cf_ref[0:1, :] * softplus
    row = lax.broadcasted_iota(jnp.int32, (TM, TM), 0)
    col = lax.broadcasted_iota(jnp.int32, (TM, TM), 1)
    same = (row // c64) == (col // c64)
    tri = (col >= row) if reverse else (col <= row)
    cum = jnp.where(jnp.logical_and(same, tri), 1.0, 0.0)
    gc = _dot(cum, g, precision=HIGHEST)
    gc_ref[...] = gc
    for c in range(nchunk):
        sq = jnp.concatenate([gc[c * c64:(c + 1) * c64, :], jnp.zeros((LANES - c64, LANES), F32)], axis=0)
        gct_ref[c] = sq.T

    ri = lax.broadcasted_iota(jnp.int32, (c64, c64), 0)
    ci = lax.broadcasted_iota(jnp.int32, (c64, c64), 1)
    causal = (ri <= ci) if reverse else (ri >= ci)
    strict = (ri < ci) if reverse else (ri > ci)
    eye = jnp.where(ri == ci, 1.0, 0.0)

    def chunk(step, carry):
        c = (nchunk - 1 - step) if reverse else step
        r0 = pl.multiple_of(c * c64, c64)
        rows = pl.ds(r0, c64)
        r_last = r0 if reverse else r0 + (c64 - 1)
        gcb = gc_ref[rows, :]
        gctb = gct_ref[c]
        glb = gc_ref[pl.ds(r_last, 1), :]
        betab = beta_ref[rows, :]
        for h in range(GDN_HEADS):
            hs = slice(h * GDN_DK, (h + 1) * GDN_DK)
            gl = SUBLANES + h
            k16 = k_ref[rows, hs]
            k32 = k16.astype(F32)
            q32 = q_ref[rows, hs].astype(F32)
            v32 = v_ref[rows, hs].astype(F32)
            beta = betab[:, h:h + 1]
            gcol = gcb[:, gl:gl + 1]
            grow = gctb[gl:gl + 1, :c64]
            glast = glb[:, gl:gl + 1]
            dec = jnp.exp(jnp.minimum(gcol - grow, 0.0))
            kb = k32 * beta
            lower = _dot_nt(kb.astype(BF16), k16) * jnp.where(strict, dec, 0.0)
            inv = eye - lower
            pw = lower
            for _ in range(5):
                pw = _dot(pw, pw, precision=HIGHEST)
                inv = inv + _dot(pw, inv, precision=HIGHEST)
            eg = jnp.exp(gcol)
            rhs = jnp.concatenate([v32 * beta, kb * eg], axis=1)
            sol = _dot(inv, rhs, precision=HIGHEST)
            u = sol[:, :GDN_DV]
            w = sol[:, GDN_DV:]
            qs = q32 * scale
            attn = _dot_nt(qs.astype(BF16), k16) * jnp.where(causal, dec, 0.0)
            q_dec = qs * eg
            k_dec = k32 * jnp.exp(glast - gcol)
            s = s_ref[h]
            s16 = s.astype(BF16)
            v_new = u - _dot(w.astype(BF16), s16)
            vn16 = v_new.astype(BF16)
            o = _dot(q_dec.astype(BF16), s16) + _dot(attn.astype(BF16), vn16)
            s_ref[h] = s * jnp.exp(glast) + _dot_tn(k_dec.astype(BF16), vn16)
            o_ref[rows, hs] = o.astype(BF16)
        return carry

    lax.fori_loop(0, nchunk, chunk, 0)


def _gdn_scan(q, k, v, ab, cf, reverse):
    b, t, c = q.shape
    nt = t // TM
    dr = 1 if reverse else 0

    def tile(s):
        lat = (nt - 1 - s) if reverse else (s - 1)
        return jnp.where(s == 0, nt - 1, lat)

    spec = pl.BlockSpec((None, TM, c), lambda bb, s: (bb, tile(s), 0))
    return pl.pallas_call(
        functools.partial(_gdn_scan_kernel, reverse),
        grid=(b, nt),
        in_specs=[spec, spec, spec,
                  pl.BlockSpec((None, None, TM, LANES), lambda bb, s: (bb, dr, tile(s), 0)),
                  pl.BlockSpec((None, SUBLANES, LANES), lambda bb, s: (dr, 0, 0))],
        out_specs=spec,
        out_shape=jax.ShapeDtypeStruct((b, t, c), BF16),
        scratch_shapes=[pltpu.VMEM((GDN_HEADS, GDN_DK, GDN_DV), F32),
                        pltpu.VMEM((TM, LANES), F32),
                        pltpu.VMEM((TM // GDN_CHUNK, LANES, LANES), F32),
                        pltpu.VMEM((TM, LANES), F32)],
        compiler_params=_cparams(("parallel", "arbitrary")),
        name="gdn_scan_bwd" if reverse else "gdn_scan_fwd",
    )(q, k, v, ab, cf)


def _gdn_out_kernel(of_ref, ob_ref, gate_ref, h_ref, mod_ref, on_ref, w_ref, g_ref, out_ref, y_ref):
    for hh in range(GDN_HEADS):
        hs = slice(hh * GDN_DV, (hh + 1) * GDN_DV)
        o = of_ref[:, hs].astype(F32) + ob_ref[:, hs].astype(F32)
        gt = gate_ref[:, hs].astype(F32)
        y_ref[:, hs] = (_rms(o) * on_ref[...] * _silu(gt)).astype(BF16)
    out = _dot(y_ref[...], w_ref[...])
    out_ref[...] = _residual(h_ref[...], out, g_ref[...], mod_ref[2:3, :])


def _gdn_out(o_f, o_b, gate, h, mod, o_norm, w_out, g):
    b, t, d = h.shape
    nt = t // TM
    c = o_f.shape[-1]
    return pl.pallas_call(
        _gdn_out_kernel,
        grid=(b, nt),
        in_specs=[_tok_spec(c), _tok_spec(c), _tok_spec(c), _tok_spec(d), _mod_spec(nt, d),
                  _full_spec((1, GDN_DV)), _full_spec(w_out.shape), _full_spec((1, d))],
        out_specs=_tok_spec(d),
        out_shape=jax.ShapeDtypeStruct((b, t, d), F32),
        scratch_shapes=[pltpu.VMEM((TM, c), BF16)],
        compiler_params=_cparams(("parallel", "parallel")),
        name="gdn_out",
    )(o_f, o_b, gate, h, mod, o_norm, w_out, g)


def _conf_in_kernel(h_ref, mod_ref, g_ref, w_ref, b_ref, u_ref):
    a16 = _norm_mod(h_ref[...], g_ref[...], mod_ref[0:1, :], mod_ref[1:2, :]).astype(BF16)
    d = u_ref.shape[-1]
    cw = 512
    for n in range(d // cw):
        lin = _dot(a16, w_ref[:, n * cw:(n + 1) * cw]) + b_ref[:, n * cw:(n + 1) * cw]
        gt = _dot(a16, w_ref[:, d + n * cw:d + (n + 1) * cw]) + b_ref[:, d + n * cw:d + (n + 1) * cw]
        u_ref[:, n * cw:(n + 1) * cw] = (lin * jax.nn.sigmoid(gt)).astype(BF16)


def _conf_in(h, mod, g, w_in, b_in):
    b, t, d = h.shape
    nt = t // TM
    return pl.pallas_call(
        _conf_in_kernel,
        grid=(b, nt),
        in_specs=[_tok_spec(d), _mod_spec(nt, d), _full_spec((1, d)),
                  _full_spec(w_in.shape), _full_spec(b_in.shape)],
        out_specs=_tok_spec(d),
        out_shape=jax.ShapeDtypeStruct((b, t, d), BF16),
        compiler_params=_cparams(("parallel", "parallel")),
        name="conf_in",
    )(h, mod, g, w_in, b_in)


def _conf_conv_kernel(nt, cur_ref, prev_ref, next_ref, dw_ref, dwb_ref, lng_ref, lnb_ref,
                      w_ref, b_ref, h_ref, mod_ref, g_ref, out_ref, ext_ref, y_ref):
    prev_ok, next_ok = _halo_valid(nt)
    pad = CONF_CONV // 2
    ext_ref[0:HALO, :] = jnp.where(prev_ok, prev_ref[...].astype(F32), 0.0)
    ext_ref[HALO:HALO + TM, :] = cur_ref[...].astype(F32)
    ext_ref[HALO + TM:2 * HALO + TM, :] = jnp.where(next_ok, next_ref[...].astype(F32), 0.0)
    rw, cw = 64, 256
    d = cur_ref.shape[-1]
    for cc in range(d // cw):
        cs = slice(cc * cw, (cc + 1) * cw)
        for rr in range(TM // rw):
            base = HALO - pad + rr * rw
            acc = dw_ref[0:1, cs] * ext_ref[pl.ds(base, rw), cs]
            for tap in range(1, CONF_CONV):
                acc = acc + dw_ref[tap:tap + 1, cs] * ext_ref[pl.ds(base + tap, rw), cs]
            y_ref[rr * rw:(rr + 1) * rw, cs] = acc + dwb_ref[:, cs]
    y = y_ref[...]
    yc = y - jnp.mean(y, axis=-1, keepdims=True)
    yn = yc * lax.rsqrt(jnp.mean(yc * yc, axis=-1, keepdims=True) + LN_EPS) * lng_ref[...] + lnb_ref[...]
    out = _dot(_silu(yn).astype(BF16), w_ref[...]) + b_ref[...]
    out_ref[...] = _residual(h_ref[...], out, g_ref[...], mod_ref[2:3, :])


def _conf_conv(u, dw, dw_b, ln_g, ln_b, w_out, b_out, h, mod, g):
    b, t, d = h.shape
    nt = t // TM
    prev, nxt = _halo_specs(nt, d)
    vec = _full_spec((1, d))
    return pl.pallas_call(
        functools.partial(_conf_conv_kernel, nt),
        grid=(b, nt),
        in_specs=[_tok_spec(d), prev, nxt, _full_spec(dw.shape), vec, vec, vec,
                  _full_spec(w_out.shape), vec, _tok_spec(d), _mod_spec(nt, d), vec],
        out_specs=_tok_spec(d),
        out_shape=jax.ShapeDtypeStruct((b, t, d), F32),
        scratch_shapes=[pltpu.VMEM((TM + 2 * HALO, d), F32), pltpu.VMEM((TM, d), F32)],
        compiler_params=_cparams(("parallel", "parallel")),
        name="conf_conv",
    )(u, u, u, dw, dw_b, ln_g, ln_b, w_out, b_out, h, mod, g)


def _mla_proj_kernel(h_ref, mod_ref, g_ref, win_ref, qn_ref, kvn_ref, wuq_ref, wukv_ref, cs_ref,
                     q_ref, k_ref, v_ref):
    a16 = _norm_mod(h_ref[...], g_ref[...], mod_ref[0:1, :], mod_ref[1:2, :]).astype(BF16)
    p = _dot(a16, win_ref[...])
    cq = _rms(p[:, :MLA_Q_RANK]) * qn_ref[...]
    ckv = _rms(p[:, MLA_Q_RANK:MLA_Q_RANK + MLA_KV_RANK]) * kvn_ref[...]
    kr = p[:, MLA_Q_RANK + MLA_KV_RANK:]
    cs = cs_ref[...]
    kx = kr * cs
    krot = (kx + pltpu.roll(kx, MLA_ROPE, 1)).astype(BF16)
    scale = (MLA_NOPE + MLA_ROPE) ** -0.5
    cq16 = cq.astype(BF16)
    ckv16 = ckv.astype(BF16)
    hw = MLA_NOPE + 2 * MLA_ROPE
    for hh in range(MLA_HEADS):
        qh = _dot(cq16, wuq_ref[:, hh * hw:(hh + 1) * hw]) * scale
        q_ref[:, hh * hw:hh * hw + MLA_NOPE] = qh[:, :MLA_NOPE].astype(BF16)
        q_ref[:, hh * hw + MLA_NOPE:(hh + 1) * hw] = (qh[:, MLA_NOPE:] * cs).astype(BF16)
        k_ref[:, hh * hw:hh * hw + MLA_NOPE] = _dot(
            ckv16, wukv_ref[:, hh * MLA_NOPE:(hh + 1) * MLA_NOPE]).astype(BF16)
        k_ref[:, hh * hw + MLA_NOPE:(hh + 1) * hw] = krot
        nk = MLA_HEADS * MLA_NOPE
        v_ref[:, hh * MLA_V:(hh + 1) * MLA_V] = _dot(
            ckv16, wukv_ref[:, nk + hh * MLA_V:nk + (hh + 1) * MLA_V]).astype(BF16)


def _mla_proj(h, mod, g, w_in, q_norm, kv_norm, w_uq, w_ukv, cs_tab):
    b, t, d = h.shape
    nt = t // TM
    hw = MLA_NOPE + 2 * MLA_ROPE
    return pl.pallas_call(
        _mla_proj_kernel,
        grid=(b, nt),
        in_specs=[_tok_spec(d), _mod_spec(nt, d), _full_spec((1, d)), _full_spec(w_in.shape),
                  _full_spec(q_norm.shape), _full_spec(kv_norm.shape),
                  _full_spec(w_uq.shape), _full_spec(w_ukv.shape),
                  pl.BlockSpec((TM, LANES), lambda bb, j: (j, 0))],
        out_specs=[_tok_spec(MLA_HEADS * hw), _tok_spec(MLA_HEADS * hw), _tok_spec(MLA_HEADS * MLA_V)],
        out_shape=[jax.ShapeDtypeStruct((b, t, MLA_HEADS * hw), BF16),
                   jax.ShapeDtypeStruct((b, t, MLA_HEADS * hw), BF16),
                   jax.ShapeDtypeStruct((b, t, MLA_HEADS * MLA_V), BF16)],
        compiler_params=_cparams(("parallel", "parallel")),
        name="mla_proj",
    )(h, mod, g, w_in, q_norm, kv_norm, w_uq, w_ukv, cs_tab)


def _attn_kernel(tk, q_ref, k_ref, v_ref, o_ref, m_ref, l_ref, acc_ref):
    nk = k_ref.shape[0] // tk
    m_ref[...] = jnp.full_like(m_ref, -jnp.inf)
    l_ref[...] = jnp.zeros_like(l_ref)
    acc_ref[...] = jnp.zeros_like(acc_ref)
    q = q_ref[...]

    def body(i, carry):
        rows = pl.ds(pl.multiple_of(i * tk, tk), tk)
        s = _dot_nt(q, k_ref[rows, :])
        m_old = m_ref[...]
        m_new = jnp.maximum(m_old, jnp.max(s, axis=-1, keepdims=True))
        alpha = jnp.exp(m_old - m_new)
        p = jnp.exp(s - m_new)
        l_ref[...] = alpha * l_ref[...] + jnp.sum(p, axis=-1, keepdims=True)
        acc_ref[...] = alpha * acc_ref[...] + _dot(p.astype(BF16), v_ref[rows, :])
        m_ref[...] = m_new
        return carry

    lax.fori_loop(0, nk, body, 0)
    o_ref[...] = (acc_ref[...] / l_ref[...]).astype(BF16)


def _attention(q, k, v, tq, q_rows, q_row0, k_rows, k_row0, name):
    b = q.shape[0]
    hw = MLA_NOPE + 2 * MLA_ROPE
    tk = next(c for c in (768, 512, 256) if k_rows % c == 0)
    q0 = q_row0 // tq
    k0 = k_row0 // k_rows
    return pl.pallas_call(
        functools.partial(_attn_kernel, tk),
        grid=(b, MLA_HEADS, q_rows // tq),
        in_specs=[pl.BlockSpec((None, tq, hw), lambda bb, hh, i: (bb, q0 + i, hh)),
                  pl.BlockSpec((None, k_rows, hw), lambda bb, hh, i: (bb, k0, hh)),
                  pl.BlockSpec((None, k_rows, MLA_V), lambda bb, hh, i: (bb, k0, hh))],
        out_specs=pl.BlockSpec((None, tq, MLA_V), lambda bb, hh, i: (bb, i, hh)),
        out_shape=jax.ShapeDtypeStruct((b, q_rows, MLA_HEADS * MLA_V), BF16),
        scratch_shapes=[pltpu.VMEM((tq, 1), F32), pltpu.VMEM((tq, 1), F32), pltpu.VMEM((tq, MLA_V), F32)],
        compiler_params=_cparams(("parallel", "parallel", "arbitrary")),
        name=name,
    )(q, k, v)


def _proj_out_kernel(o_ref, w_ref, h_ref, mod_ref, g_ref, out_ref):
    out = _dot(o_ref[...], w_ref[...])
    out_ref[...] = _residual(h_ref[...], out, g_ref[...], mod_ref[2:3, :])


def _proj_out(o, w, h, mod, g):
    b, t, d = h.shape
    nt = t // TM
    return pl.pallas_call(
        _proj_out_kernel,
        grid=(b, nt),
        in_specs=[_tok_spec(o.shape[-1]), _full_spec(w.shape), _tok_spec(d), _mod_spec(nt, d),
                  _full_spec((1, d))],
        out_specs=_tok_spec(d),
        out_shape=jax.ShapeDtypeStruct((b, t, d), F32),
        compiler_params=_cparams(("parallel", "parallel")),
        name="mla_out",
    )(o, w, h, mod, g)


def _ffn_kernel(fc, h_ref, mod_ref, g2_ref, wg_ref, wu_ref, wd_ref, g3_ref, out_ref, acc_ref):
    h = h_ref[...]
    a16 = _norm_mod(h, g2_ref[...], mod_ref[3:4, :], mod_ref[4:5, :]).astype(BF16)
    for c in range(wg_ref.shape[-1] // fc):
        fs = slice(c * fc, (c + 1) * fc)
        act = (_silu(_dot(a16, wg_ref[:, fs])) * _dot(a16, wu_ref[:, fs])).astype(BF16)
        part = _dot(act, wd_ref[fs, :])
        if c == 0:
            acc_ref[...] = part
        else:
            acc_ref[...] += part
    out_ref[...] = _residual(h, acc_ref[...], g3_ref[...], mod_ref[5:6, :])


def _ffn(h, mod, g2, w_gate, w_up, w_down, g3, n_tiles):
    b, t, d = h.shape
    nt = t // TM
    return pl.pallas_call(
        functools.partial(_ffn_kernel, 256),
        grid=(b, n_tiles),
        in_specs=[_tok_spec(d), _mod_spec(nt, d), _full_spec((1, d)), _full_spec(w_gate.shape),
                  _full_spec(w_up.shape), _full_spec(w_down.shape), _full_spec((1, d))],
        out_specs=_tok_spec(d),
        out_shape=jax.ShapeDtypeStruct((b, n_tiles * TM, d), F32),
        scratch_shapes=[pltpu.VMEM((TM, d), F32)],
        compiler_params=_cparams(("parallel", "parallel")),
        name="ffn",
    )(h, mod, g2, w_gate, w_up, w_down, g3)


def _moe_kernel(h_ref, mod_ref, g2_ref, wr_ref, wg_ref, wu_ref, wd_ref, g3_ref, out_ref,
                a_ref, gates_ref, acc_ref):
    e = pl.program_id(2)
    lane = lax.broadcasted_iota(jnp.int32, (TM, LANES), 1).astype(F32)

    @pl.when(e == 0)
    def _():
        a = _norm_mod(h_ref[...], g2_ref[...], mod_ref[3:4, :], mod_ref[4:5, :])
        a_ref[...] = a.astype(BF16)
        logits = jnp.where(lane < N_EXPERTS, _dot(a, wr_ref[...], precision=HIGHEST), -jnp.inf)
        m1 = jnp.max(logits, axis=-1, keepdims=True)
        i1 = jnp.min(jnp.where(logits == m1, lane, float(LANES)), axis=-1, keepdims=True)
        rest = jnp.where(lane == i1, -jnp.inf, logits)
        m2 = jnp.max(rest, axis=-1, keepdims=True)
        i2 = jnp.min(jnp.where(rest == m2, lane, float(LANES)), axis=-1, keepdims=True)
        e2 = jnp.exp(m2 - m1)
        w1 = 1.0 / (1.0 + e2)
        gates_ref[...] = jnp.where(lane == i1, w1, 0.0) + jnp.where(lane == i2, e2 * w1, 0.0)
        acc_ref[...] = jnp.zeros_like(acc_ref)

    gate_e = jnp.sum(jnp.where(lane == e.astype(F32), gates_ref[...], 0.0), axis=-1, keepdims=True)
    a16 = a_ref[...]
    act = _silu(_dot(a16, wg_ref[...])) * _dot(a16, wu_ref[...])
    acc_ref[...] += gate_e * _dot(act.astype(BF16), wd_ref[...])

    @pl.when(e == N_EXPERTS - 1)
    def _():
        out_ref[...] = _residual(h_ref[...], acc_ref[...], g3_ref[...], mod_ref[5:6, :])


def _moe(h, mod, g2, w_router, w_gate, w_up, w_down, g3, n_tiles):
    b, t, d = h.shape
    nt = t // TM
    f = w_gate.shape[-1]
    tok = pl.BlockSpec((None, TM, d), lambda bb, j, e: (bb, j, 0))
    vec = pl.BlockSpec((1, d), lambda bb, j, e: (0, 0))
    return pl.pallas_call(
        _moe_kernel,
        grid=(b, n_tiles, N_EXPERTS),
        in_specs=[tok,
                  pl.BlockSpec((None, SUBLANES, d), lambda bb, j, e: (jnp.where(j == nt - 1, 0, bb + 1), 0, 0)),
                  vec,
                  pl.BlockSpec((d, LANES), lambda bb, j, e: (0, 0)),
                  pl.BlockSpec((None, d, f), lambda bb, j, e: (e, 0, 0)),
                  pl.BlockSpec((None, d, f), lambda bb, j, e: (e, 0, 0)),
                  pl.BlockSpec((None, f, d), lambda bb, j, e: (e, 0, 0)),
                  vec],
        out_specs=tok,
        out_shape=jax.ShapeDtypeStruct((b, n_tiles * TM, d), F32),
        scratch_shapes=[pltpu.VMEM((TM, d), BF16), pltpu.VMEM((TM, LANES), F32), pltpu.VMEM((TM, d), F32)],
        compiler_params=_cparams(("parallel", "parallel", "arbitrary")),
        name="moe",
    )(h, mod, g2, w_router, w_gate, w_up, w_down, g3)


def _pad_rows(a, rows):
    return jnp.pad(a, ((0, rows - a.shape[0]),) + ((0, 0),) * (a.ndim - 1))


def _rope_table(seq, ctx_len):
    n_freq = MLA_ROPE // 4
    t = jnp.arange(seq)
    pos = jnp.stack([(t // GRID_W).astype(F32), (t % GRID_W).astype(F32)], axis=1)
    inv = ROPE_THETA ** (-jnp.arange(n_freq, dtype=F32) / n_freq)
    ang = pos[:, :, None] * inv
    cos = jnp.repeat(jnp.cos(ang)[:, :, None, :], 2, axis=2).reshape(seq, MLA_ROPE)
    sin = jnp.stack([-jnp.sin(ang), jnp.sin(ang)], axis=2).reshape(seq, MLA_ROPE)
    lat = jnp.concatenate([cos, sin], axis=1)
    ctx = jnp.concatenate([jnp.ones((ctx_len, MLA_ROPE), F32), jnp.zeros((ctx_len, MLA_ROPE), F32)], axis=1)
    return jnp.concatenate([lat, ctx], axis=0)


def _swap_halves(w):
    shp = w.shape
    return jnp.flip(w.reshape(shp[:-1] + (2, 2, MLA_ROPE // 4)), axis=-2).reshape(shp)


def kernel(x, c, ctx, c_ctx, w_mod, b_mod, norm_g, gdn_w_in, gdn_conv, gdn_a_log, gdn_dt_bias, gdn_o_norm, gdn_w_out, conf_w_in, conf_b_in, conf_dw, conf_dw_b, conf_ln_g, conf_ln_b, conf_w_out, conf_b_out, mla_w_in, mla_q_norm, mla_kv_norm, mla_w_uq, mla_w_ukv, mla_w_o, ffn_w_gate, ffn_w_up, ffn_w_down, moe_router, moe_w_gate, moe_w_up, moe_w_down):
    bsz, seq, d = x.shape
    ctx_len = ctx.shape[1]
    depth = w_mod.shape[0]
    assert ctx_len == TM and seq % (2 * TM) == 0 and bsz + 1 <= SUBLANES
    nt = (seq + ctx_len) // TM
    n_lat = seq // TM

    cvec = _pad_rows(jnp.concatenate([c_ctx[None, :], c], axis=0), SUBLANES)
    mods = _adaln(cvec, w_mod, b_mod)
    mods = mods.reshape(depth, SUBLANES, N_MOD, d)[:, :bsz + 1]
    mods = jnp.pad(mods, ((0, 0), (0, 0), (0, SUBLANES - N_MOD), (0, 0)))

    h = jnp.concatenate([x, ctx], axis=1)
    rope_cs = _rope_table(seq, ctx_len)

    n_mix = 3
    for i in range(depth):
        last = i == depth - 1
        mod = mods[i]
        g = norm_g[i].reshape(4, 1, d)
        kind, j = i % n_mix, i // n_mix
        if kind == 0:
            nqk = GDN_HEADS * GDN_DK
            nmain = 2 * nqk + 2 * GDN_HEADS * GDN_DV
            w_in = gdn_w_in[j]
            w_main = w_in[:, :nmain].astype(BF16)
            w_tail = w_in[:, nmain:]
            nh = GDN_HEADS
            w_ab = jnp.stack([
                jnp.pad(jnp.concatenate([w_tail[:, dr * nh:(dr + 1) * nh],
                                         w_tail[:, (2 + dr) * nh:(3 + dr) * nh]], axis=1),
                        ((0, 0), (0, LANES - 2 * nh))) for dr in range(2)])
            lane_pad = ((0, 0), (nh, LANES - 2 * nh))
            cf = jnp.stack([jnp.pad(jnp.exp(gdn_a_log[j]), lane_pad),
                            jnp.pad(gdn_dt_bias[j], lane_pad)], axis=1)
            cf = jnp.pad(cf, ((0, 0), (0, SUBLANES - 2), (0, 0)))
            qkv, gate, ab = _gdn_proj(h, mod, g[0], w_main, w_ab)
            q, k, v = _gdn_conv(qkv, _pad_rows(gdn_conv[j], SUBLANES))
            o_f = _gdn_scan(q, k, v, ab, cf, reverse=False)
            o_b = _gdn_scan(q, k, v, ab, cf, reverse=True)
            h = _gdn_out(o_f, o_b, gate, h, mod, gdn_o_norm[j].reshape(1, -1),
                         gdn_w_out[j].astype(BF16), g[1])
        elif kind == 1:
            u = _conf_in(h, mod, g[0], conf_w_in[j].astype(BF16), conf_b_in[j].reshape(1, -1))
            h = _conf_conv(u, _pad_rows(conf_dw[j], 32), conf_dw_b[j].reshape(1, -1),
                           conf_ln_g[j].reshape(1, -1), conf_ln_b[j].reshape(1, -1),
                           conf_w_out[j].astype(BF16), conf_b_out[j].reshape(1, -1), h, mod, g[1])
        else:
            w_in = mla_w_in[j]
            w_kr = w_in[:, MLA_Q_RANK + MLA_KV_RANK:]
            w_in2 = jnp.concatenate([w_in, _swap_halves(w_kr)], axis=1).astype(BF16)
            hq = MLA_NOPE + MLA_ROPE
            w_uq = mla_w_uq[j].reshape(MLA_Q_RANK, MLA_HEADS, hq)
            w_uq2 = jnp.concatenate([w_uq, _swap_halves(w_uq[..., MLA_NOPE:])], axis=-1)
            w_uq2 = w_uq2.reshape(MLA_Q_RANK, -1).astype(BF16)
            w_ukv = mla_w_ukv[j].reshape(MLA_KV_RANK, MLA_HEADS, MLA_NOPE + MLA_V)
            w_ukv2 = jnp.concatenate([w_ukv[..., :MLA_NOPE].reshape(MLA_KV_RANK, -1),
                                      w_ukv[..., MLA_NOPE:].reshape(MLA_KV_RANK, -1)], axis=1).astype(BF16)
            qa, ka, va = _mla_proj(h, mod, g[0], w_in2, mla_q_norm[j].reshape(1, -1),
                                   mla_kv_norm[j].reshape(1, -1), w_uq2, w_ukv2, rope_cs)
            o_l = _attention(qa, ka, va, 2 * TM, seq, 0, seq + ctx_len, 0, "attn_latent")
            o_c = _attention(qa, ka, va, TM, ctx_len, seq, ctx_len, seq, "attn_context")
            o = jnp.concatenate([o_l, o_c], axis=1)
            h = _proj_out(o, mla_w_o[j].astype(BF16), h, mod, g[1])

        n_tiles = n_lat if last else nt
        f = i // 2
        if i % 2 == 0:
            h = _ffn(h, mod, g[2], ffn_w_gate[f].astype(BF16), ffn_w_up[f].astype(BF16),
                     ffn_w_down[f].astype(BF16), g[3], n_tiles)
        else:
            w_r = jnp.pad(moe_router[f], ((0, 0), (0, LANES - N_EXPERTS)))
            h = _moe(h, mod, g[2], w_r, moe_w_gate[f].astype(BF16), moe_w_up[f].astype(BF16),
                     moe_w_down[f].astype(BF16), g[3], n_tiles)
    return h
```

```python
import functools

import jax
import jax.numpy as jnp
from jax import lax
from jax.experimental import pallas as pl
from jax.experimental.pallas import tpu as pltpu

F32 = jnp.float32
BF16 = jnp.bfloat16
HIGHEST = lax.Precision.HIGHEST

N_MOD = 6
RMS_EPS = 1e-6
LN_EPS = 1e-5
GRID_W = 64
GDN_HEADS = 8
GDN_DK = 128
GDN_DV = 128
GDN_CONV = 5
GDN_CHUNK = 64
CONF_CONV = 31
MLA_HEADS = 8
MLA_NOPE = 128
MLA_ROPE = 64
MLA_V = 128
MLA_Q_RANK = 512
MLA_KV_RANK = 256
ROPE_THETA = 10000.0
LOG2E = 1.4426950408889634
N_EXPERTS = 8

LANES = 128
SUBLANES = 8
TM = 256
HALO = 16
VMEM_LIMIT = 56 * 1024 * 1024


def _cparams(sem):
    return pltpu.CompilerParams(dimension_semantics=sem, vmem_limit_bytes=VMEM_LIMIT)


def _dot(a, b, **kw):
    return jnp.dot(a, b, preferred_element_type=F32, **kw)


def _dot_nt(a, b):
    return lax.dot_general(a, b, (((1,), (1,)), ((), ())), preferred_element_type=F32)


def _dot_tn(a, b):
    return lax.dot_general(a, b, (((0,), (0,)), ((), ())), preferred_element_type=F32)


def _silu(x):
    return x * jax.nn.sigmoid(x)


def _rms(x, eps=RMS_EPS):
    return x * lax.rsqrt(jnp.mean(x * x, axis=-1, keepdims=True) + eps)


def _norm_mod(h, g, shift, scale):
    return _rms(h) * g * (1.0 + scale) + shift


def _residual(h, o, g, gate):
    return h + gate * (_rms(o) * g)


def _tok_spec(c):
    return pl.BlockSpec((None, TM, c), lambda b, j: (b, j, 0))


def _full_spec(shape):
    nd = len(shape)
    return pl.BlockSpec(shape, lambda *_: (0,) * nd)


def _mod_spec(nt, d):
    return pl.BlockSpec((None, SUBLANES, d), lambda b, j: (jnp.where(j == nt - 1, 0, b + 1), 0, 0))


def _halo_specs(nt, c):
    per = TM // HALO
    prev = pl.BlockSpec((None, HALO, c), lambda b, j: (b, jnp.maximum(j * per - 1, 0), 0))
    nxt = pl.BlockSpec((None, HALO, c), lambda b, j: (b, jnp.minimum((j + 1) * per, nt * per - 1), 0))
    return prev, nxt


def _halo_valid(nt):
    j = pl.program_id(1)
    prev_ok = jnp.logical_and(j != 0, j != nt - 1)
    next_ok = j < nt - 2
    return prev_ok, next_ok


def _adaln_kernel(c_ref, w_ref, b_ref, o_ref):
    o_ref[...] = _dot(_silu(c_ref[...]), w_ref[...], precision=HIGHEST) + b_ref[...]


def _adaln(cvec, w_mod, b_mod):
    depth, d, n = w_mod.shape
    tn = 1536
    return pl.pallas_call(
        _adaln_kernel,
        grid=(depth, n // tn),
        in_specs=[pl.BlockSpec((SUBLANES, d), lambda i, k: (0, 0)),
                  pl.BlockSpec((None, d, tn), lambda i, k: (i, 0, k)),
                  pl.BlockSpec((None, 1, tn), lambda i, k: (i, 0, k))],
        out_specs=pl.BlockSpec((None, SUBLANES, tn), lambda i, k: (i, 0, k)),
        out_shape=jax.ShapeDtypeStruct((depth, SUBLANES, n), F32),
        compiler_params=_cparams(("parallel", "parallel")),
        name="adaln",
    )(cvec, w_mod, b_mod.reshape(depth, 1, n))


def _gdn_proj_kernel(h_ref, mod_ref, g_ref, w_ref, wab_ref, qkv_ref, gate_ref, ab_ref):
    a = _norm_mod(h_ref[...], g_ref[...], mod_ref[0:1, :], mod_ref[1:2, :])
    a16 = a.astype(BF16)
    nq = qkv_ref.shape[-1]
    for n in range(nq // 512):
        qkv_ref[:, n * 512:(n + 1) * 512] = _dot(a16, w_ref[:, n * 512:(n + 1) * 512]).astype(BF16)
    for n in range(gate_ref.shape[-1] // 512):
        gate_ref[:, n * 512:(n + 1) * 512] = _dot(a16, w_ref[:, nq + n * 512:nq + (n + 1) * 512]).astype(BF16)
    for dr in range(2):
        ab_ref[dr] = _dot(a, wab_ref[dr], precision=HIGHEST)


def _gdn_proj(h, mod, g, w_main, w_ab):
    b, t, d = h.shape
    nt = t // TM
    nq = 2 * GDN_HEADS * GDN_DK + GDN_HEADS * GDN_DV
    nv = GDN_HEADS * GDN_DV
    return pl.pallas_call(
        _gdn_proj_kernel,
        grid=(b, nt),
        in_specs=[_tok_spec(d), _mod_spec(nt, d), _full_spec((1, d)),
                  _full_spec(w_main.shape), _full_spec(w_ab.shape)],
        out_specs=[_tok_spec(nq), _tok_spec(nv),
                   pl.BlockSpec((None, 2, TM, LANES), lambda bb, j: (bb, 0, j, 0))],
        out_shape=[jax.ShapeDtypeStruct((b, t, nq), BF16),
                   jax.ShapeDtypeStruct((b, t, nv), BF16),
                   jax.ShapeDtypeStruct((b, 2, t, LANES), F32)],
        compiler_params=_cparams(("parallel", "parallel")),
        name="gdn_proj",
    )(h, mod, g, w_main, w_ab)


def _gdn_conv_kernel(nt, cur_ref, prev_ref, next_ref, w_ref, q_ref, k_ref, v_ref, ext_ref):
    prev_ok, next_ok = _halo_valid(nt)
    cw = 512
    pad = GDN_CONV // 2
    nqk = GDN_HEADS * GDN_DK
    for cc in range(cur_ref.shape[-1] // cw):
        cs = slice(cc * cw, (cc + 1) * cw)
        ext_ref[0:SUBLANES, :] = jnp.where(prev_ok, prev_ref[HALO - SUBLANES:HALO, cs].astype(F32), 0.0)
        ext_ref[SUBLANES:SUBLANES + TM, :] = cur_ref[:, cs].astype(F32)
        ext_ref[SUBLANES + TM:2 * SUBLANES + TM, :] = jnp.where(next_ok, next_ref[0:SUBLANES, cs].astype(F32), 0.0)
        acc = w_ref[0:1, cs] * ext_ref[pl.ds(SUBLANES - pad, TM), :]
        for tap in range(1, GDN_CONV):
            acc = acc + w_ref[tap:tap + 1, cs] * ext_ref[pl.ds(SUBLANES - pad + tap, TM), :]
        y = _silu(acc)
        for hh in range(cw // LANES):
            col = cc * cw + hh * LANES
            yh = y[:, hh * LANES:(hh + 1) * LANES]
            if col < 2 * nqk:
                yh = yh * lax.rsqrt(jnp.sum(yh * yh, axis=-1, keepdims=True) + RMS_EPS)
            if col < nqk:
                q_ref[:, col:col + LANES] = yh.astype(BF16)
            elif col < 2 * nqk:
                k_ref[:, col - nqk:col - nqk + LANES] = yh.astype(BF16)
            else:
                v_ref[:, col - 2 * nqk:col - 2 * nqk + LANES] = yh.astype(BF16)


def _gdn_conv(qkv, conv_w):
    b, t, c = qkv.shape
    nt = t // TM
    nqk = GDN_HEADS * GDN_DK
    prev, nxt = _halo_specs(nt, c)
    return pl.pallas_call(
        functools.partial(_gdn_conv_kernel, nt),
        grid=(b, nt),
        in_specs=[_tok_spec(c), prev, nxt, _full_spec(conv_w.shape)],
        out_specs=[_tok_spec(nqk), _tok_spec(nqk), _tok_spec(GDN_HEADS * GDN_DV)],
        out_shape=[jax.ShapeDtypeStruct((b, t, nqk), BF16),
                   jax.ShapeDtypeStruct((b, t, nqk), BF16),
                   jax.ShapeDtypeStruct((b, t, GDN_HEADS * GDN_DV), BF16)],
        scratch_shapes=[pltpu.VMEM((TM + 2 * SUBLANES, 512), F32)],
        compiler_params=_cparams(("parallel", "parallel")),
        name="gdn_conv",
    )(qkv, qkv, qkv, conv_w)


def _split(x):
    hi = x.astype(BF16)
    return hi, (x - hi.astype(F32)).astype(BF16)


def _dot_hi(a, b):
    ah, al = _split(a)
    bh, bl = _split(b)
    return _dot(ah, bh) + (_dot(ah, bl) + _dot(al, bh))


def _gdn_scan_kernel(reverse, nb, q_ref, k_ref, v_ref, ab_ref, cf_ref, o_ref,
                     s_ref, gc_ref, gct_ref, beta_ref, u_ref, wq_ref, kd_ref, at_ref):
    nchunk = TM // GDN_CHUNK
    c64 = GDN_CHUNK
    scale = GDN_DK ** -0.5
    chains = [(b, h) for b in range(nb) for h in range(GDN_HEADS)]

    @pl.when(pl.program_id(0) == 0)
    def _():
        s_ref[...] = jnp.zeros_like(s_ref)

    row = lax.broadcasted_iota(jnp.int32, (TM, TM), 0)
    col = lax.broadcasted_iota(jnp.int32, (TM, TM), 1)
    same = (row // c64) == (col // c64)
    tri = (col >= row) if reverse else (col <= row)
    cum = jnp.where(jnp.logical_and(same, tri), 1.0, 0.0)
    for b in range(nb):
        ab = ab_ref[b]
        beta_ref[b] = jax.nn.sigmoid(ab)
        x = ab + cf_ref[1:2, :]
        softplus = jnp.maximum(x, 0.0) + jnp.log1p(jnp.exp(-jnp.abs(x)))
        g = -cf_ref[0:1, :] * softplus
        gc = _dot(cum, g, precision=HIGHEST)
        gc_ref[b] = gc
        for c in range(nchunk):
            blk = gc[c * c64:(c + 1) * c64, :]
            gct_ref[b, c] = jnp.concatenate([blk, blk], axis=0).T

    ri = lax.broadcasted_iota(jnp.int32, (c64, LANES), 0)
    ln = lax.broadcasted_iota(jnp.int32, (c64, LANES), 1)
    ci = ln % c64
    left = ln < c64
    causal = (ri <= ci) if reverse else (ri >= ci)
    strict = (ri < ci) if reverse else (ri > ci)
    blk16 = (ri // 16) == (ci // 16)
    blk32 = (ri // 32) == (ci // 32)
    m_c1 = jnp.logical_and(strict, jnp.logical_and(blk32, jnp.logical_not(blk16)))[:, :c64]
    m_c2 = jnp.logical_and(strict, jnp.logical_not(blk32))[:, :c64]
    m_l0x = jnp.logical_and(jnp.logical_and(strict, blk16), jnp.logical_not(left))
    eye_l = jnp.where(jnp.logical_and(ri == ci, left), 1.0, 0.0)
    zpad = jnp.zeros((c64, LANES), F32)

    def prep(c, carry):
        r0 = pl.multiple_of(c * c64, c64)
        rows = pl.ds(r0, c64)
        r_last = r0 if reverse else r0 + (c64 - 1)
        gcb = [gc_ref[b, rows, :] for b in range(nb)]
        gctb = [gct_ref[b, c] for b in range(nb)]
        glb = [gc_ref[b, pl.ds(r_last, 1), :] for b in range(nb)]
        betab = [beta_ref[b, rows, :] for b in range(nb)]
        st = []
        for (b, h) in chains:
            hs = slice(h * GDN_DK, (h + 1) * GDN_DK)
            gl = SUBLANES + h
            k16 = k_ref[b, rows, hs]
            k32 = k16.astype(F32)
            beta = betab[b][:, h:h + 1]
            gcol = gcb[b][:, gl:gl + 1]
            dec2 = jnp.exp(jnp.minimum(gcol - gctb[b][gl:gl + 1, :], 0.0))
            kb = k32 * beta
            kk2 = _dot_nt(kb.astype(BF16), jnp.concatenate([k16, k16], axis=0))
            ll2 = kk2 * dec2
            st.append(dict(hs=hs, b=b, h=h, k32=k32, k16=k16, beta=beta, gcol=gcol, kb=kb,
                           glast=glb[b][:, gl:gl + 1], dec=dec2[:, :c64],
                           ll=ll2[:, :c64], x=jnp.where(m_l0x, -ll2, eye_l)))
        for _ in range(4):
            for s in st:
                x = s["x"]
                s["x"] = jnp.where(left, x, 0.0) + _dot_hi(x, jnp.concatenate([zpad, x], axis=0))
        for s in st:
            s["d"] = s["x"][:, :c64]
        for mask in (m_c1, m_c2):
            for s in st:
                s["t"] = _dot_hi(jnp.where(mask, s["ll"], 0.0), s["d"])
            for s in st:
                s["d"] = s["d"] - _dot_hi(s["d"], s["t"])
        for s in st:
            eg = jnp.exp(s["gcol"])
            s["eg"] = eg
            v32 = v_ref[s["b"], rows, s["hs"]].astype(F32)
            rhs = jnp.concatenate([v32 * s["beta"], s["kb"] * eg], axis=1)
            s["sol"] = _dot_hi(s["d"], rhs)
        for s in st:
            b, h, hs = s["b"], s["h"], s["hs"]
            qs = q_ref[b, rows, hs].astype(F32) * scale
            attn = _dot_nt(qs.astype(BF16), s["k16"]) * jnp.where(causal[:, :c64], s["dec"], 0.0)
            u_ref[b, rows, hs] = s["sol"][:, :GDN_DV]
            wq_ref[b, c, h, 0:c64, :] = s["sol"][:, GDN_DV:].astype(BF16)
            wq_ref[b, c, h, c64:2 * c64, :] = (qs * s["eg"]).astype(BF16)
            kd_ref[b, rows, hs] = (s["k32"] * jnp.exp(s["glast"] - s["gcol"])).astype(BF16)
            at_ref[b, c, h] = attn.astype(BF16)
        return carry

    lax.fori_loop(0, nchunk, prep, 0)

    def scan(step, carry):
        c = (nchunk - 1 - step) if reverse else step
        r0 = pl.multiple_of(c * c64, c64)
        rows = pl.ds(r0, c64)
        r_last = r0 if reverse else r0 + (c64 - 1)
        glb = [gc_ref[b, pl.ds(r_last, 1), :] for b in range(nb)]
        st = []
        for (b, h) in chains:
            s = s_ref[b, h]
            st.append(dict(b=b, h=h, hs=slice(h * GDN_DK, (h + 1) * GDN_DK), s=s,
                           ws=_dot(wq_ref[b, c, h], s.astype(BF16))))
        for d in st:
            v_new = u_ref[d["b"], rows, d["hs"]] - d["ws"][:c64]
            d["vn"] = v_new.astype(BF16)
        for d in st:
            b, h, hs = d["b"], d["h"], d["hs"]
            o = d["ws"][c64:] + _dot(at_ref[b, c, h], d["vn"])
            o_ref[b, rows, hs] = o.astype(BF16)
            egl = jnp.exp(glb[b][:, SUBLANES + h:SUBLANES + h + 1])
            s_ref[b, h] = d["s"] * egl + _dot_tn(kd_ref[b, rows, hs], d["vn"])
        return carry

    lax.fori_loop(0, nchunk, scan, 0)


def _gdn_scan(q, k, v, ab, cf, reverse):
    b, t, c = q.shape
    nt = t // TM
    dr = 1 if reverse else 0
    nchunk = TM // GDN_CHUNK

    def tile(s):
        lat = (nt - 1 - s) if reverse else (s - 1)
        return jnp.where(s == 0, nt - 1, lat)

    spec = pl.BlockSpec((b, TM, c), lambda s: (0, tile(s), 0))
    return pl.pallas_call(
        functools.partial(_gdn_scan_kernel, reverse, b),
        grid=(nt,),
        in_specs=[spec, spec, spec,
                  pl.BlockSpec((b, None, TM, LANES), lambda s: (0, dr, tile(s), 0)),
                  pl.BlockSpec((None, SUBLANES, LANES), lambda s: (dr, 0, 0))],
        out_specs=spec,
        out_shape=jax.ShapeDtypeStruct((b, t, c), BF16),
        scratch_shapes=[pltpu.VMEM((b, GDN_HEADS, GDN_DK, GDN_DV), F32),
                        pltpu.VMEM((b, TM, LANES), F32),
                        pltpu.VMEM((b, nchunk, LANES, LANES), F32),
                        pltpu.VMEM((b, TM, LANES), F32),
                        pltpu.VMEM((b, TM, c), F32),
                        pltpu.VMEM((b, nchunk, GDN_HEADS, 2 * GDN_CHUNK, GDN_DK), BF16),
                        pltpu.VMEM((b, TM, c), BF16),
                        pltpu.VMEM((b, nchunk, GDN_HEADS, GDN_CHUNK, GDN_CHUNK), BF16)],
        compiler_params=_cparams(("arbitrary",)),
        name="gdn_scan_bwd" if reverse else "gdn_scan_fwd",
    )(q, k, v, ab, cf)


def _gdn_out_kernel(of_ref, ob_ref, gate_ref, h_ref, mod_ref, on_ref, w_ref, g_ref, out_ref, y_ref):
    for hh in range(GDN_HEADS):
        hs = slice(hh * GDN_DV, (hh + 1) * GDN_DV)
        o = of_ref[:, hs].astype(F32) + ob_ref[:, hs].astype(F32)
        gt = gate_ref[:, hs].astype(F32)
        y_ref[:, hs] = (_rms(o) * on_ref[...] * _silu(gt)).astype(BF16)
    out = _dot(y_ref[...], w_ref[...])
    out_ref[...] = _residual(h_ref[...], out, g_ref[...], mod_ref[2:3, :])


def _gdn_out(o_f, o_b, gate, h, mod, o_norm, w_out, g):
    b, t, d = h.shape
    nt = t // TM
    c = o_f.shape[-1]
    return pl.pallas_call(
        _gdn_out_kernel,
        grid=(b, nt),
        in_specs=[_tok_spec(c), _tok_spec(c), _tok_spec(c), _tok_spec(d), _mod_spec(nt, d),
                  _full_spec((1, GDN_DV)), _full_spec(w_out.shape), _full_spec((1, d))],
        out_specs=_tok_spec(d),
        out_shape=jax.ShapeDtypeStruct((b, t, d), F32),
        scratch_shapes=[pltpu.VMEM((TM, c), BF16)],
        compiler_params=_cparams(("parallel", "parallel")),
        name="gdn_out",
    )(o_f, o_b, gate, h, mod, o_norm, w_out, g)


def _conf_in_kernel(h_ref, mod_ref, g_ref, w_ref, b_ref, u_ref):
    a16 = _norm_mod(h_ref[...], g_ref[...], mod_ref[0:1, :], mod_ref[1:2, :]).astype(BF16)
    d = u_ref.shape[-1]
    cw = 512
    for n in range(d // cw):
        lin = _dot(a16, w_ref[:, n * cw:(n + 1) * cw]) + b_ref[:, n * cw:(n + 1) * cw]
        gt = _dot(a16, w_ref[:, d + n * cw:d + (n + 1) * cw]) + b_ref[:, d + n * cw:d + (n + 1) * cw]
        u_ref[:, n * cw:(n + 1) * cw] = (lin * jax.nn.sigmoid(gt)).astype(BF16)


def _conf_in(h, mod, g, w_in, b_in):
    b, t, d = h.shape
    nt = t // TM
    return pl.pallas_call(
        _conf_in_kernel,
        grid=(b, nt),
        in_specs=[_tok_spec(d), _mod_spec(nt, d), _full_spec((1, d)),
                  _full_spec(w_in.shape), _full_spec(b_in.shape)],
        out_specs=_tok_spec(d),
        out_shape=jax.ShapeDtypeStruct((b, t, d), BF16),
        compiler_params=_cparams(("parallel", "parallel")),
        name="conf_in",
    )(h, mod, g, w_in, b_in)


def _conf_conv_kernel(nt, cur_ref, prev_ref, next_ref, dw_ref, dwb_ref, lng_ref, lnb_ref,
                      w_ref, b_ref, h_ref, mod_ref, g_ref, out_ref, ext_ref, y_ref):
    prev_ok, next_ok = _halo_valid(nt)
    pad = CONF_CONV // 2
    ext_ref[0:HALO, :] = jnp.where(prev_ok, prev_ref[...].astype(F32), 0.0)
    ext_ref[HALO:HALO + TM, :] = cur_ref[...].astype(F32)
    ext_ref[HALO + TM:2 * HALO + TM, :] = jnp.where(next_ok, next_ref[...].astype(F32), 0.0)
    rw, cw = 64, 256
    d = cur_ref.shape[-1]
    for cc in range(d // cw):
        cs = slice(cc * cw, (cc + 1) * cw)
        for rr in range(TM // rw):
            base = HALO - pad + rr * rw
            acc = dw_ref[0:1, cs] * ext_ref[pl.ds(base, rw), cs]
            for tap in range(1, CONF_CONV):
                acc = acc + dw_ref[tap:tap + 1, cs] * ext_ref[pl.ds(base + tap, rw), cs]
            y_ref[rr * rw:(rr + 1) * rw, cs] = acc + dwb_ref[:, cs]
    y = y_ref[...]
    yc = y - jnp.mean(y, axis=-1, keepdims=True)
    yn = yc * lax.rsqrt(jnp.mean(yc * yc, axis=-1, keepdims=True) + LN_EPS) * lng_ref[...] + lnb_ref[...]
    out = _dot(_silu(yn).astype(BF16), w_ref[...]) + b_ref[...]
    out_ref[...] = _residual(h_ref[...], out, g_ref[...], mod_ref[2:3, :])


def _conf_conv(u, dw, dw_b, ln_g, ln_b, w_out, b_out, h, mod, g):
    b, t, d = h.shape
    nt = t // TM
    prev, nxt = _halo_specs(nt, d)
    vec = _full_spec((1, d))
    return pl.pallas_call(
        functools.partial(_conf_conv_kernel, nt),
        grid=(b, nt),
        in_specs=[_tok_spec(d), prev, nxt, _full_spec(dw.shape), vec, vec, vec,
                  _full_spec(w_out.shape), vec, _tok_spec(d), _mod_spec(nt, d), vec],
        out_specs=_tok_spec(d),
        out_shape=jax.ShapeDtypeStruct((b, t, d), F32),
        scratch_shapes=[pltpu.VMEM((TM + 2 * HALO, d), F32), pltpu.VMEM((TM, d), F32)],
        compiler_params=_cparams(("parallel", "parallel")),
        name="conf_conv",
    )(u, u, u, dw, dw_b, ln_g, ln_b, w_out, b_out, h, mod, g)


def _mla_proj_kernel(h_ref, mod_ref, g_ref, win_ref, qn_ref, kvn_ref, wuq_ref, wukv_ref, cs_ref,
                     q_ref, k_ref, v_ref):
    a16 = _norm_mod(h_ref[...], g_ref[...], mod_ref[0:1, :], mod_ref[1:2, :]).astype(BF16)
    p = _dot(a16, win_ref[...])
    cq = _rms(p[:, :MLA_Q_RANK]) * qn_ref[...]
    ckv = _rms(p[:, MLA_Q_RANK:MLA_Q_RANK + MLA_KV_RANK]) * kvn_ref[...]
    kr = p[:, MLA_Q_RANK + MLA_KV_RANK:]
    cs = cs_ref[...]
    kx = kr * cs
    krot = (kx + pltpu.roll(kx, MLA_ROPE, 1)).astype(BF16)
    scale = (MLA_NOPE + MLA_ROPE) ** -0.5 * LOG2E
    cq16 = cq.astype(BF16)
    ckv16 = ckv.astype(BF16)
    hw = MLA_NOPE + 2 * MLA_ROPE
    for hh in range(MLA_HEADS):
        qh = _dot(cq16, wuq_ref[:, hh * hw:(hh + 1) * hw]) * scale
        q_ref[:, hh * hw:hh * hw + MLA_NOPE] = qh[:, :MLA_NOPE].astype(BF16)
        q_ref[:, hh * hw + MLA_NOPE:(hh + 1) * hw] = (qh[:, MLA_NOPE:] * cs).astype(BF16)
        k_ref[:, hh * hw:hh * hw + MLA_NOPE] = _dot(
            ckv16, wukv_ref[:, hh * MLA_NOPE:(hh + 1) * MLA_NOPE]).astype(BF16)
        k_ref[:, hh * hw + MLA_NOPE:(hh + 1) * hw] = krot
        nk = MLA_HEADS * MLA_NOPE
        v_ref[:, 2 * hh * MLA_V:(2 * hh + 1) * MLA_V] = _dot(
            ckv16, wukv_ref[:, nk + hh * MLA_V:nk + (hh + 1) * MLA_V]).astype(BF16)
        v_ref[:, (2 * hh + 1) * MLA_V:(2 * hh + 2) * MLA_V] = jnp.ones((TM, MLA_V), BF16)


def _mla_proj(h, mod, g, w_in, q_norm, kv_norm, w_uq, w_ukv, cs_tab):
    b, t, d = h.shape
    nt = t // TM
    hw = MLA_NOPE + 2 * MLA_ROPE
    return pl.pallas_call(
        _mla_proj_kernel,
        grid=(b, nt),
        in_specs=[_tok_spec(d), _mod_spec(nt, d), _full_spec((1, d)), _full_spec(w_in.shape),
                  _full_spec(q_norm.shape), _full_spec(kv_norm.shape),
                  _full_spec(w_uq.shape), _full_spec(w_ukv.shape),
                  pl.BlockSpec((TM, LANES), lambda bb, j: (j, 0))],
        out_specs=[_tok_spec(MLA_HEADS * hw), _tok_spec(MLA_HEADS * hw), _tok_spec(MLA_HEADS * 2 * MLA_V)],
        out_shape=[jax.ShapeDtypeStruct((b, t, MLA_HEADS * hw), BF16),
                   jax.ShapeDtypeStruct((b, t, MLA_HEADS * hw), BF16),
                   jax.ShapeDtypeStruct((b, t, MLA_HEADS * 2 * MLA_V), BF16)],
        compiler_params=_cparams(("parallel", "parallel")),
        name="mla_proj",
    )(h, mod, g, w_in, q_norm, kv_norm, w_uq, w_ukv, cs_tab)


def _attn_kernel(tk, q_ref, k_ref, v_ref, o_ref, m_ref, acc_ref):
    q = q_ref[...]
    for i in range(k_ref.shape[0] // tk):
        rows = slice(i * tk, (i + 1) * tk)
        s = _dot_nt(q, k_ref[rows, :])
        m_blk = jnp.max(s, axis=-1, keepdims=True)
        if i == 0:
            m_new = m_blk
            acc_ref[...] = _dot(jnp.exp2(s - m_new).astype(BF16), v_ref[rows, :])
        else:
            m_old = m_ref[...]
            m_new = jnp.maximum(m_old, m_blk)
            acc_ref[...] = jnp.exp2(m_old - m_new) * acc_ref[...] + _dot(
                jnp.exp2(s - m_new).astype(BF16), v_ref[rows, :])
        m_ref[...] = m_new
    acc = acc_ref[...]
    o_ref[...] = (acc[:, :MLA_V] / acc[:, MLA_V:]).astype(BF16)


def _attention(q, k, v, tq, q_rows, q_row0, k_rows, k_row0, name):
    b = q.shape[0]
    hw = MLA_NOPE + 2 * MLA_ROPE
    tk = next(c for c in (768, 512, 256) if k_rows % c == 0)
    q0 = q_row0 // tq
    k0 = k_row0 // k_rows
    return pl.pallas_call(
        functools.partial(_attn_kernel, tk),
        grid=(b, MLA_HEADS, q_rows // tq),
        in_specs=[pl.BlockSpec((None, tq, hw), lambda bb, hh, i: (bb, q0 + i, hh)),
                  pl.BlockSpec((None, k_rows, hw), lambda bb, hh, i: (bb, k0, hh)),
                  pl.BlockSpec((None, k_rows, 2 * MLA_V), lambda bb, hh, i: (bb, k0, hh))],
        out_specs=pl.BlockSpec((None, tq, MLA_V), lambda bb, hh, i: (bb, i, hh)),
        out_shape=jax.ShapeDtypeStruct((b, q_rows, MLA_HEADS * MLA_V), BF16),
        scratch_shapes=[pltpu.VMEM((tq, 1), F32), pltpu.VMEM((tq, 2 * MLA_V), F32)],
        compiler_params=_cparams(("parallel", "parallel", "arbitrary")),
        name=name,
    )(q, k, v)


def _proj_out_kernel(o_ref, w_ref, h_ref, mod_ref, g_ref, out_ref):
    out = _dot(o_ref[...], w_ref[...])
    out_ref[...] = _residual(h_ref[...], out, g_ref[...], mod_ref[2:3, :])


def _proj_out(o, w, h, mod, g):
    b, t, d = h.shape
    nt = t // TM
    return pl.pallas_call(
        _proj_out_kernel,
        grid=(b, nt),
        in_specs=[_tok_spec(o.shape[-1]), _full_spec(w.shape), _tok_spec(d), _mod_spec(nt, d),
                  _full_spec((1, d))],
        out_specs=_tok_spec(d),
        out_shape=jax.ShapeDtypeStruct((b, t, d), F32),
        compiler_params=_cparams(("parallel", "parallel")),
        name="mla_out",
    )(o, w, h, mod, g)


def _ffn_kernel(fc, h_ref, mod_ref, g2_ref, wg_ref, wu_ref, wd_ref, g3_ref, out_ref, acc_ref):
    h = h_ref[...]
    a16 = _norm_mod(h, g2_ref[...], mod_ref[3:4, :], mod_ref[4:5, :]).astype(BF16)
    for c in range(wg_ref.shape[-1] // fc):
        fs = slice(c * fc, (c + 1) * fc)
        act = (_silu(_dot(a16, wg_ref[:, fs])) * _dot(a16, wu_ref[:, fs])).astype(BF16)
        part = _dot(act, wd_ref[fs, :])
        if c == 0:
            acc_ref[...] = part
        else:
            acc_ref[...] += part
    out_ref[...] = _residual(h, acc_ref[...], g3_ref[...], mod_ref[5:6, :])


def _ffn(h, mod, g2, w_gate, w_up, w_down, g3, n_tiles):
    b, t, d = h.shape
    nt = t // TM
    return pl.pallas_call(
        functools.partial(_ffn_kernel, 256),
        grid=(b, n_tiles),
        in_specs=[_tok_spec(d), _mod_spec(nt, d), _full_spec((1, d)), _full_spec(w_gate.shape),
                  _full_spec(w_up.shape), _full_spec(w_down.shape), _full_spec((1, d))],
        out_specs=_tok_spec(d),
        out_shape=jax.ShapeDtypeStruct((b, n_tiles * TM, d), F32),
        scratch_shapes=[pltpu.VMEM((TM, d), F32)],
        compiler_params=_cparams(("parallel", "parallel")),
        name="ffn",
    )(h, mod, g2, w_gate, w_up, w_down, g3)


def _moe_kernel(h_ref, mod_ref, g2_ref, wr_ref, wg_ref, wu_ref, wd_ref, g3_ref, out_ref,
                a_ref, gates_ref, acc_ref):
    e = pl.program_id(2)
    lane = lax.broadcasted_iota(jnp.int32, (TM, LANES), 1).astype(F32)

    @pl.when(e == 0)
    def _():
        a = _norm_mod(h_ref[...], g2_ref[...], mod_ref[3:4, :], mod_ref[4:5, :])
        a_ref[...] = a.astype(BF16)
        logits = jnp.where(lane < N_EXPERTS, _dot(a, wr_ref[...], precision=HIGHEST), -jnp.inf)
        m1 = jnp.max(logits, axis=-1, keepdims=True)
        i1 = jnp.min(jnp.where(logits == m1, lane, float(LANES)), axis=-1, keepdims=True)
        rest = jnp.where(lane == i1, -jnp.inf, logits)
        m2 = jnp.max(rest, axis=-1, keepdims=True)
        i2 = jnp.min(jnp.where(rest == m2, lane, float(LANES)), axis=-1, keepdims=True)
        e2 = jnp.exp(m2 - m1)
        w1 = 1.0 / (1.0 + e2)
        gates_ref[...] = jnp.where(lane == i1, w1, 0.0) + jnp.where(lane == i2, e2 * w1, 0.0)
        acc_ref[...] = jnp.zeros_like(acc_ref)

    gate_e = jnp.sum(jnp.where(lane == e.astype(F32), gates_ref[...], 0.0), axis=-1, keepdims=True)
    a16 = a_ref[...]
    act = _silu(_dot(a16, wg_ref[...])) * _dot(a16, wu_ref[...])
    acc_ref[...] += gate_e * _dot(act.astype(BF16), wd_ref[...])

    @pl.when(e == N_EXPERTS - 1)
    def _():
        out_ref[...] = _residual(h_ref[...], acc_ref[...], g3_ref[...], mod_ref[5:6, :])


def _moe(h, mod, g2, w_router, w_gate, w_up, w_down, g3, n_tiles):
    b, t, d = h.shape
    nt = t // TM
    f = w_gate.shape[-1]
    tok = pl.BlockSpec((None, TM, d), lambda bb, j, e: (bb, j, 0))
    vec = pl.BlockSpec((1, d), lambda bb, j, e: (0, 0))
    return pl.pallas_call(
        _moe_kernel,
        grid=(b, n_tiles, N_EXPERTS),
        in_specs=[tok,
                  pl.BlockSpec((None, SUBLANES, d), lambda bb, j, e: (jnp.where(j == nt - 1, 0, bb + 1), 0, 0)),
                  vec,
                  pl.BlockSpec((d, LANES), lambda bb, j, e: (0, 0)),
                  pl.BlockSpec((None, d, f), lambda bb, j, e: (e, 0, 0)),
                  pl.BlockSpec((None, d, f), lambda bb, j, e: (e, 0, 0)),
                  pl.BlockSpec((None, f, d), lambda bb, j, e: (e, 0, 0)),
                  vec],
        out_specs=tok,
        out_shape=jax.ShapeDtypeStruct((b, n_tiles * TM, d), F32),
        scratch_shapes=[pltpu.VMEM((TM, d), BF16), pltpu.VMEM((TM, LANES), F32), pltpu.VMEM((TM, d), F32)],
        compiler_params=_cparams(("parallel", "parallel", "arbitrary")),
        name="moe",
    )(h, mod, g2, w_router, w_gate, w_up, w_down, g3)


def _pad_rows(a, rows):
    return jnp.pad(a, ((0, rows - a.shape[0]),) + ((0, 0),) * (a.ndim - 1))


def _rope_table(seq, ctx_len):
    n_freq = MLA_ROPE // 4
    t = jnp.arange(seq)
    pos = jnp.stack([(t // GRID_W).astype(F32), (t % GRID_W).astype(F32)], axis=1)
    inv = ROPE_THETA ** (-jnp.arange(n_freq, dtype=F32) / n_freq)
    ang = pos[:, :, None] * inv
    cos = jnp.repeat(jnp.cos(ang)[:, :, None, :], 2, axis=2).reshape(seq, MLA_ROPE)
    sin = jnp.stack([-jnp.sin(ang), jnp.sin(ang)], axis=2).reshape(seq, MLA_ROPE)
    lat = jnp.concatenate([cos, sin], axis=1)
    ctx = jnp.concatenate([jnp.ones((ctx_len, MLA_ROPE), F32), jnp.zeros((ctx_len, MLA_ROPE), F32)], axis=1)
    return jnp.concatenate([lat, ctx], axis=0)


def _swap_halves(w):
    shp = w.shape
    return jnp.flip(w.reshape(shp[:-1] + (2, 2, MLA_ROPE // 4)), axis=-2).reshape(shp)


def kernel(x, c, ctx, c_ctx, w_mod, b_mod, norm_g, gdn_w_in, gdn_conv, gdn_a_log, gdn_dt_bias, gdn_o_norm, gdn_w_out, conf_w_in, conf_b_in, conf_dw, conf_dw_b, conf_ln_g, conf_ln_b, conf_w_out, conf_b_out, mla_w_in, mla_q_norm, mla_kv_norm, mla_w_uq, mla_w_ukv, mla_w_o, ffn_w_gate, ffn_w_up, ffn_w_down, moe_router, moe_w_gate, moe_w_up, moe_w_down):
    bsz, seq, d = x.shape
    ctx_len = ctx.shape[1]
    depth = w_mod.shape[0]
    assert ctx_len == TM and seq % (2 * TM) == 0 and bsz + 1 <= SUBLANES
    nt = (seq + ctx_len) // TM
    n_lat = seq // TM

    cvec = _pad_rows(jnp.concatenate([c_ctx[None, :], c], axis=0), SUBLANES)
    mods = _adaln(cvec, w_mod, b_mod)
    mods = mods.reshape(depth, SUBLANES, N_MOD, d)[:, :bsz + 1]
    mods = jnp.pad(mods, ((0, 0), (0, 0), (0, SUBLANES - N_MOD), (0, 0)))

    h = jnp.concatenate([x, ctx], axis=1)
    rope_cs = _rope_table(seq, ctx_len)

    n_mix = 3
    for i in range(depth):
        last = i == depth - 1
        mod = mods[i]
        g = norm_g[i].reshape(4, 1, d)
        kind, j = i % n_mix, i // n_mix
        if kind == 0:
            nqk = GDN_HEADS * GDN_DK
            nmain = 2 * nqk + 2 * GDN_HEADS * GDN_DV
            w_in = gdn_w_in[j]
            w_main = w_in[:, :nmain].astype(BF16)
            w_tail = w_in[:, nmain:]
            nh = GDN_HEADS
            w_ab = jnp.stack([
                jnp.pad(jnp.concatenate([w_tail[:, dr * nh:(dr + 1) * nh],
                                         w_tail[:, (2 + dr) * nh:(3 + dr) * nh]], axis=1),
                        ((0, 0), (0, LANES - 2 * nh))) for dr in range(2)])
            lane_pad = ((0, 0), (nh, LANES - 2 * nh))
            cf = jnp.stack([jnp.pad(jnp.exp(gdn_a_log[j]), lane_pad),
                            jnp.pad(gdn_dt_bias[j], lane_pad)], axis=1)
            cf = jnp.pad(cf, ((0, 0), (0, SUBLANES - 2), (0, 0)))
            qkv, gate, ab = _gdn_proj(h, mod, g[0], w_main, w_ab)
            q, k, v = _gdn_conv(qkv, _pad_rows(gdn_conv[j], SUBLANES))
            o_f = _gdn_scan(q, k, v, ab, cf, reverse=False)
            o_b = _gdn_scan(q, k, v, ab, cf, reverse=True)
            h = _gdn_out(o_f, o_b, gate, h, mod, gdn_o_norm[j].reshape(1, -1),
                         gdn_w_out[j].astype(BF16), g[1])
        elif kind == 1:
            u = _conf_in(h, mod, g[0], conf_w_in[j].astype(BF16), conf_b_in[j].reshape(1, -1))
            h = _conf_conv(u, _pad_rows(conf_dw[j], 32), conf_dw_b[j].reshape(1, -1),
                           conf_ln_g[j].reshape(1, -1), conf_ln_b[j].reshape(1, -1),
                           conf_w_out[j].astype(BF16), conf_b_out[j].reshape(1, -1), h, mod, g[1])
        else:
            w_in = mla_w_in[j]
            w_kr = w_in[:, MLA_Q_RANK + MLA_KV_RANK:]
            w_in2 = jnp.concatenate([w_in, _swap_halves(w_kr)], axis=1).astype(BF16)
            hq = MLA_NOPE + MLA_ROPE
            w_uq = mla_w_uq[j].reshape(MLA_Q_RANK, MLA_HEADS, hq)
            w_uq2 = jnp.concatenate([w_uq, _swap_halves(w_uq[..., MLA_NOPE:])], axis=-1)
            w_uq2 = w_uq2.reshape(MLA_Q_RANK, -1).astype(BF16)
            w_ukv = mla_w_ukv[j].reshape(MLA_KV_RANK, MLA_HEADS, MLA_NOPE + MLA_V)
            w_ukv2 = jnp.concatenate([w_ukv[..., :MLA_NOPE].reshape(MLA_KV_RANK, -1),
                                      w_ukv[..., MLA_NOPE:].reshape(MLA_KV_RANK, -1)], axis=1).astype(BF16)
            qa, ka, va = _mla_proj(h, mod, g[0], w_in2, mla_q_norm[j].reshape(1, -1),
                                   mla_kv_norm[j].reshape(1, -1), w_uq2, w_ukv2, rope_cs)
            o_l = _attention(qa, ka, va, 2 * TM, seq, 0, seq + ctx_len, 0, "attn_latent")
            o_c = _attention(qa, ka, va, TM, ctx_len, seq, ctx_len, seq, "attn_context")
            o = jnp.concatenate([o_l, o_c], axis=1)
            h = _proj_out(o, mla_w_o[j].astype(BF16), h, mod, g[1])

        n_tiles = n_lat if last else nt
        f = i // 2
        if i % 2 == 0:
            h = _ffn(h, mod, g[2], ffn_w_gate[f].astype(BF16), ffn_w_up[f].astype(BF16),
                     ffn_w_down[f].astype(BF16), g[3], n_tiles)
        else:
            w_r = jnp.pad(moe_router[f], ((0, 0), (0, LANES - N_EXPERTS)))
            h = _moe(h, mod, g[2], w_r, moe_w_gate[f].astype(BF16), moe_w_up[f].astype(BF16),
                     moe_w_down[f].astype(BF16), g[3], n_tiles)
    return h
```

```python
import functools

import jax
import jax.numpy as jnp
from jax import lax
from jax.experimental import pallas as pl
from jax.experimental.pallas import tpu as pltpu

F32 = jnp.float32
BF16 = jnp.bfloat16
HIGHEST = lax.Precision.HIGHEST

N_MOD = 6
RMS_EPS = 1e-6
LN_EPS = 1e-5
GRID_W = 64
GDN_HEADS = 8
GDN_DK = 128
GDN_DV = 128
GDN_CONV = 5
GDN_CHUNK = 64
CONF_CONV = 31
MLA_HEADS = 8
MLA_NOPE = 128
MLA_ROPE = 64
MLA_V = 128
MLA_Q_RANK = 512
MLA_KV_RANK = 256
ROPE_THETA = 10000.0
LOG2E = 1.4426950408889634
N_EXPERTS = 8

LANES = 128
SUBLANES = 8
TM = 256
HALO = 16
VMEM_LIMIT = 56 * 1024 * 1024


def _cparams(sem):
    return pltpu.CompilerParams(dimension_semantics=sem, vmem_limit_bytes=VMEM_LIMIT)


def _dot(a, b, **kw):
    return jnp.dot(a, b, preferred_element_type=F32, **kw)


def _dot_nt(a, b):
    return lax.dot_general(a, b, (((1,), (1,)), ((), ())), preferred_element_type=F32)


def _dot_tn(a, b):
    return lax.dot_general(a, b, (((0,), (0,)), ((), ())), preferred_element_type=F32)


def _silu(x):
    return x * jax.nn.sigmoid(x)


def _rms(x, eps=RMS_EPS):
    return x * lax.rsqrt(jnp.mean(x * x, axis=-1, keepdims=True) + eps)


def _norm_mod(h, g, shift, scale):
    return _rms(h) * g * (1.0 + scale) + shift


def _residual(h, o, g, gate):
    return h + gate * (_rms(o) * g)


def _tok_spec(c):
    return pl.BlockSpec((None, TM, c), lambda b, j: (b, j, 0))


def _full_spec(shape):
    nd = len(shape)
    return pl.BlockSpec(shape, lambda *_: (0,) * nd)


def _mod_spec(nt, d):
    return pl.BlockSpec((None, SUBLANES, d), lambda b, j: (jnp.where(j == nt - 1, 0, b + 1), 0, 0))


def _halo_specs(nt, c):
    per = TM // HALO
    prev = pl.BlockSpec((None, HALO, c), lambda b, j: (b, jnp.maximum(j * per - 1, 0), 0))
    nxt = pl.BlockSpec((None, HALO, c), lambda b, j: (b, jnp.minimum((j + 1) * per, nt * per - 1), 0))
    return prev, nxt


def _halo_valid(nt):
    j = pl.program_id(1)
    prev_ok = jnp.logical_and(j != 0, j != nt - 1)
    next_ok = j < nt - 2
    return prev_ok, next_ok


def _adaln_kernel(c_ref, w_ref, b_ref, o_ref):
    o_ref[...] = _dot(_silu(c_ref[...]), w_ref[...], precision=HIGHEST) + b_ref[...]


def _adaln(cvec, w_mod, b_mod):
    depth, d, n = w_mod.shape
    tn = 1536
    return pl.pallas_call(
        _adaln_kernel,
        grid=(depth, n // tn),
        in_specs=[pl.BlockSpec((SUBLANES, d), lambda i, k: (0, 0)),
                  pl.BlockSpec((None, d, tn), lambda i, k: (i, 0, k)),
                  pl.BlockSpec((None, 1, tn), lambda i, k: (i, 0, k))],
        out_specs=pl.BlockSpec((None, SUBLANES, tn), lambda i, k: (i, 0, k)),
        out_shape=jax.ShapeDtypeStruct((depth, SUBLANES, n), F32),
        compiler_params=_cparams(("parallel", "parallel")),
        name="adaln",
    )(cvec, w_mod, b_mod.reshape(depth, 1, n))


def _gdn_proj_kernel(h_ref, mod_ref, g_ref, w_ref, wab_ref, qkv_ref, gate_ref, ab_ref):
    a = _norm_mod(h_ref[...], g_ref[...], mod_ref[0:1, :], mod_ref[1:2, :])
    a16 = a.astype(BF16)
    nq = qkv_ref.shape[-1]
    for n in range(nq // 512):
        qkv_ref[:, n * 512:(n + 1) * 512] = _dot(a16, w_ref[:, n * 512:(n + 1) * 512]).astype(BF16)
    for n in range(gate_ref.shape[-1] // 512):
        gate_ref[:, n * 512:(n + 1) * 512] = _dot(a16, w_ref[:, nq + n * 512:nq + (n + 1) * 512]).astype(BF16)
    for dr in range(2):
        ab_ref[dr] = _dot(a, wab_ref[dr], precision=HIGHEST)


def _gdn_proj(h, mod, g, w_main, w_ab):
    b, t, d = h.shape
    nt = t // TM
    nq = 2 * GDN_HEADS * GDN_DK + GDN_HEADS * GDN_DV
    nv = GDN_HEADS * GDN_DV
    return pl.pallas_call(
        _gdn_proj_kernel,
        grid=(b, nt),
        in_specs=[_tok_spec(d), _mod_spec(nt, d), _full_spec((1, d)),
                  _full_spec(w_main.shape), _full_spec(w_ab.shape)],
        out_specs=[_tok_spec(nq), _tok_spec(nv),
                   pl.BlockSpec((None, 2, TM, LANES), lambda bb, j: (bb, 0, j, 0))],
        out_shape=[jax.ShapeDtypeStruct((b, t, nq), BF16),
                   jax.ShapeDtypeStruct((b, t, nv), BF16),
                   jax.ShapeDtypeStruct((b, 2, t, LANES), F32)],
        compiler_params=_cparams(("parallel", "parallel")),
        name="gdn_proj",
    )(h, mod, g, w_main, w_ab)


def _gdn_conv_kernel(nt, cur_ref, prev_ref, next_ref, w_ref, q_ref, k_ref, v_ref, ext_ref):
    prev_ok, next_ok = _halo_valid(nt)
    cw = 512
    pad = GDN_CONV // 2
    nqk = GDN_HEADS * GDN_DK
    for cc in range(cur_ref.shape[-1] // cw):
        cs = slice(cc * cw, (cc + 1) * cw)
        ext_ref[0:SUBLANES, :] = jnp.where(prev_ok, prev_ref[HALO - SUBLANES:HALO, cs].astype(F32), 0.0)
        ext_ref[SUBLANES:SUBLANES + TM, :] = cur_ref[:, cs].astype(F32)
        ext_ref[SUBLANES + TM:2 * SUBLANES + TM, :] = jnp.where(next_ok, next_ref[0:SUBLANES, cs].astype(F32), 0.0)
        acc = w_ref[0:1, cs] * ext_ref[pl.ds(SUBLANES - pad, TM), :]
        for tap in range(1, GDN_CONV):
            acc = acc + w_ref[tap:tap + 1, cs] * ext_ref[pl.ds(SUBLANES - pad + tap, TM), :]
        y = _silu(acc)
        for hh in range(cw // LANES):
            col = cc * cw + hh * LANES
            yh = y[:, hh * LANES:(hh + 1) * LANES]
            if col < 2 * nqk:
                yh = yh * lax.rsqrt(jnp.sum(yh * yh, axis=-1, keepdims=True) + RMS_EPS)
            if col < nqk:
                q_ref[:, col:col + LANES] = yh.astype(BF16)
            elif col < 2 * nqk:
                k_ref[:, col - nqk:col - nqk + LANES] = yh.astype(BF16)
            else:
                v_ref[:, col - 2 * nqk:col - 2 * nqk + LANES] = yh.astype(BF16)


def _gdn_conv(qkv, conv_w):
    b, t, c = qkv.shape
    nt = t // TM
    nqk = GDN_HEADS * GDN_DK
    prev, nxt = _halo_specs(nt, c)
    return pl.pallas_call(
        functools.partial(_gdn_conv_kernel, nt),
        grid=(b, nt),
        in_specs=[_tok_spec(c), prev, nxt, _full_spec(conv_w.shape)],
        out_specs=[_tok_spec(nqk), _tok_spec(nqk), _tok_spec(GDN_HEADS * GDN_DV)],
        out_shape=[jax.ShapeDtypeStruct((b, t, nqk), BF16),
                   jax.ShapeDtypeStruct((b, t, nqk), BF16),
                   jax.ShapeDtypeStruct((b, t, GDN_HEADS * GDN_DV), BF16)],
        scratch_shapes=[pltpu.VMEM((TM + 2 * SUBLANES, 512), F32)],
        compiler_params=_cparams(("parallel", "parallel")),
        name="gdn_conv",
    )(qkv, qkv, qkv, conv_w)


def _split(x):
    hi = x.astype(BF16)
    return hi, (x - hi.astype(F32)).astype(BF16)


def _dot_hi(a, b):
    ah, al = _split(a)
    bh, bl = _split(b)
    return _dot(ah, bh) + (_dot(ah, bl) + _dot(al, bh))


def _gdn_scan_kernel(reverse, nb, q_ref, k_ref, v_ref, ab_ref, cf_ref, o_ref,
                     s_ref, gc_ref, gct_ref, beta_ref, u_ref, wq_ref, kd_ref, at_ref):
    nchunk = TM // GDN_CHUNK
    c64 = GDN_CHUNK
    scale = GDN_DK ** -0.5
    chains = [(b, h) for b in range(nb) for h in range(GDN_HEADS)]

    @pl.when(pl.program_id(0) == 0)
    def _():
        s_ref[...] = jnp.zeros_like(s_ref)

    row = lax.broadcasted_iota(jnp.int32, (TM, TM), 0)
    col = lax.broadcasted_iota(jnp.int32, (TM, TM), 1)
    same = (row // c64) == (col // c64)
    tri = (col >= row) if reverse else (col <= row)
    cum = jnp.where(jnp.logical_and(same, tri), 1.0, 0.0)
    for b in range(nb):
        ab = ab_ref[b]
        beta_ref[b] = jax.nn.sigmoid(ab)
        x = ab + cf_ref[1:2, :]
        softplus = jnp.maximum(x, 0.0) + jnp.log1p(jnp.exp(-jnp.abs(x)))
        g = -cf_ref[0:1, :] * softplus
        gc = _dot(cum, g, precision=HIGHEST)
        gc_ref[b] = gc
        for c in range(nchunk):
            blk = gc[c * c64:(c + 1) * c64, :]
            gct_ref[b, c] = jnp.concatenate([blk, blk], axis=0).T

    ri = lax.broadcasted_iota(jnp.int32, (c64, LANES), 0)
    ln = lax.broadcasted_iota(jnp.int32, (c64, LANES), 1)
    ci = ln % c64
    left = ln < c64
    causal = (ri <= ci) if reverse else (ri >= ci)
    strict = (ri < ci) if reverse else (ri > ci)
    blk16 = (ri // 16) == (ci // 16)
    blk32 = (ri // 32) == (ci // 32)
    m_c1 = jnp.logical_and(strict, jnp.logical_and(blk32, jnp.logical_not(blk16)))[:, :c64]
    m_c2 = jnp.logical_and(strict, jnp.logical_not(blk32))[:, :c64]
    m_l0x = jnp.logical_and(jnp.logical_and(strict, blk16), jnp.logical_not(left))
    eye_l = jnp.where(jnp.logical_and(ri == ci, left), 1.0, 0.0)
    zpad = jnp.zeros((c64, LANES), F32)

    def prep(c, carry):
        r0 = pl.multiple_of(c * c64, c64)
        rows = pl.ds(r0, c64)
        r_last = r0 if reverse else r0 + (c64 - 1)
        gcb = [gc_ref[b, rows, :] for b in range(nb)]
        gctb = [gct_ref[b, c] for b in range(nb)]
        glb = [gc_ref[b, pl.ds(r_last, 1), :] for b in range(nb)]
        betab = [beta_ref[b, rows, :] for b in range(nb)]
        st = []
        for (b, h) in chains:
            hs = slice(h * GDN_DK, (h + 1) * GDN_DK)
            gl = SUBLANES + h
            k16 = k_ref[b, rows, hs]
            k32 = k16.astype(F32)
            beta = betab[b][:, h:h + 1]
            gcol = gcb[b][:, gl:gl + 1]
            dec2 = jnp.exp(jnp.minimum(gcol - gctb[b][gl:gl + 1, :], 0.0))
            kb = k32 * beta
            kk2 = _dot_nt(kb.astype(BF16), jnp.concatenate([k16, k16], axis=0))
            ll2 = kk2 * dec2
            st.append(dict(hs=hs, b=b, h=h, k32=k32, k16=k16, beta=beta, gcol=gcol, kb=kb,
                           glast=glb[b][:, gl:gl + 1], dec=dec2[:, :c64],
                           ll=ll2[:, :c64], x=jnp.where(m_l0x, -ll2, eye_l)))
        for _ in range(4):
            for s in st:
                x = s["x"]
                s["x"] = jnp.where(left, x, 0.0) + _dot_hi(x, jnp.concatenate([zpad, x], axis=0))
        for s in st:
            s["d"] = s["x"][:, :c64]
        for mask in (m_c1, m_c2):
            for s in st:
                s["t"] = _dot_hi(jnp.where(mask, s["ll"], 0.0), s["d"])
            for s in st:
                s["d"] = s["d"] - _dot_hi(s["d"], s["t"])
        for s in st:
            eg = jnp.exp(s["gcol"])
            s["eg"] = eg
            v32 = v_ref[s["b"], rows, s["hs"]].astype(F32)
            rhs = jnp.concatenate([v32 * s["beta"], s["kb"] * eg], axis=1)
            s["sol"] = _dot_hi(s["d"], rhs)
        for s in st:
            b, h, hs = s["b"], s["h"], s["hs"]
            qs = q_ref[b, rows, hs].astype(F32) * scale
            attn = _dot_nt(qs.astype(BF16), s["k16"]) * jnp.where(causal[:, :c64], s["dec"], 0.0)
            u_ref[b, rows, hs] = s["sol"][:, :GDN_DV]
            wq_ref[b, c, h, 0:c64, :] = s["sol"][:, GDN_DV:].astype(BF16)
            wq_ref[b, c, h, c64:2 * c64, :] = (qs * s["eg"]).astype(BF16)
            kd_ref[b, rows, hs] = (s["k32"] * jnp.exp(s["glast"] - s["gcol"])).astype(BF16)
            at_ref[b, c, h] = attn.astype(BF16)
        return carry

    lax.fori_loop(0, nchunk, prep, 0)

    def scan(step, carry):
        c = (nchunk - 1 - step) if reverse else step
        r0 = pl.multiple_of(c * c64, c64)
        rows = pl.ds(r0, c64)
        r_last = r0 if reverse else r0 + (c64 - 1)
        glb = [gc_ref[b, pl.ds(r_last, 1), :] for b in range(nb)]
        st = []
        for (b, h) in chains:
            s = s_ref[b, h]
            st.append(dict(b=b, h=h, hs=slice(h * GDN_DK, (h + 1) * GDN_DK), s=s,
                           ws=_dot(wq_ref[b, c, h], s.astype(BF16))))
        for d in st:
            v_new = u_ref[d["b"], rows, d["hs"]] - d["ws"][:c64]
            d["vn"] = v_new.astype(BF16)
        for d in st:
            b, h, hs = d["b"], d["h"], d["hs"]
            o = d["ws"][c64:] + _dot(at_ref[b, c, h], d["vn"])
            o_ref[b, rows, hs] = o.astype(BF16)
            egl = jnp.exp(glb[b][:, SUBLANES + h:SUBLANES + h + 1])
            s_ref[b, h] = d["s"] * egl + _dot_tn(kd_ref[b, rows, hs], d["vn"])
        return carry

    lax.fori_loop(0, nchunk, scan, 0)


def _gdn_scan(q, k, v, ab, cf, reverse):
    b, t, c = q.shape
    nt = t // TM
    dr = 1 if reverse else 0
    nchunk = TM // GDN_CHUNK

    def tile(s):
        lat = (nt - 1 - s) if reverse else (s - 1)
        return jnp.where(s == 0, nt - 1, lat)

    spec = pl.BlockSpec((b, TM, c), lambda s: (0, tile(s), 0))
    return pl.pallas_call(
        functools.partial(_gdn_scan_kernel, reverse, b),
        grid=(nt,),
        in_specs=[spec, spec, spec,
                  pl.BlockSpec((b, None, TM, LANES), lambda s: (0, dr, tile(s), 0)),
                  pl.BlockSpec((None, SUBLANES, LANES), lambda s: (dr, 0, 0))],
        out_specs=spec,
        out_shape=jax.ShapeDtypeStruct((b, t, c), BF16),
        scratch_shapes=[pltpu.VMEM((b, GDN_HEADS, GDN_DK, GDN_DV), F32),
                        pltpu.VMEM((b, TM, LANES), F32),
                        pltpu.VMEM((b, nchunk, LANES, LANES), F32),
                        pltpu.VMEM((b, TM, LANES), F32),
                        pltpu.VMEM((b, TM, c), F32),
                        pltpu.VMEM((b, nchunk, GDN_HEADS, 2 * GDN_CHUNK, GDN_DK), BF16),
                        pltpu.VMEM((b, TM, c), BF16),
                        pltpu.VMEM((b, nchunk, GDN_HEADS, GDN_CHUNK, GDN_CHUNK), BF16)],
        compiler_params=_cparams(("arbitrary",)),
        name="gdn_scan_bwd" if reverse else "gdn_scan_fwd",
    )(q, k, v, ab, cf)


def _gdn_out_kernel(of_ref, ob_ref, gate_ref, h_ref, mod_ref, on_ref, w_ref, g_ref, out_ref, y_ref):
    for hh in range(GDN_HEADS):
        hs = slice(hh * GDN_DV, (hh + 1) * GDN_DV)
        o = of_ref[:, hs].astype(F32) + ob_ref[:, hs].astype(F32)
        gt = gate_ref[:, hs].astype(F32)
        y_ref[:, hs] = (_rms(o) * on_ref[...] * _silu(gt)).astype(BF16)
    out = _dot(y_ref[...], w_ref[...])
    out_ref[...] = _residual(h_ref[...], out, g_ref[...], mod_ref[2:3, :])


def _gdn_out(o_f, o_b, gate, h, mod, o_norm, w_out, g):
    b, t, d = h.shape
    nt = t // TM
    c = o_f.shape[-1]
    return pl.pallas_call(
        _gdn_out_kernel,
        grid=(b, nt),
        in_specs=[_tok_spec(c), _tok_spec(c), _tok_spec(c), _tok_spec(d), _mod_spec(nt, d),
                  _full_spec((1, GDN_DV)), _full_spec(w_out.shape), _full_spec((1, d))],
        out_specs=_tok_spec(d),
        out_shape=jax.ShapeDtypeStruct((b, t, d), F32),
        scratch_shapes=[pltpu.VMEM((TM, c), BF16)],
        compiler_params=_cparams(("parallel", "parallel")),
        name="gdn_out",
    )(o_f, o_b, gate, h, mod, o_norm, w_out, g)


def _conf_in_kernel(h_ref, mod_ref, g_ref, w_ref, b_ref, u_ref):
    a16 = _norm_mod(h_ref[...], g_ref[...], mod_ref[0:1, :], mod_ref[1:2, :]).astype(BF16)
    d = u_ref.shape[-1]
    cw = 512
    for n in range(d // cw):
        lin = _dot(a16, w_ref[:, n * cw:(n + 1) * cw]) + b_ref[:, n * cw:(n + 1) * cw]
        gt = _dot(a16, w_ref[:, d + n * cw:d + (n + 1) * cw]) + b_ref[:, d + n * cw:d + (n + 1) * cw]
        u_ref[:, n * cw:(n + 1) * cw] = (lin * jax.nn.sigmoid(gt)).astype(BF16)


def _conf_in(h, mod, g, w_in, b_in):
    b, t, d = h.shape
    nt = t // TM
    return pl.pallas_call(
        _conf_in_kernel,
        grid=(b, nt),
        in_specs=[_tok_spec(d), _mod_spec(nt, d), _full_spec((1, d)),
                  _full_spec(w_in.shape), _full_spec(b_in.shape)],
        out_specs=_tok_spec(d),
        out_shape=jax.ShapeDtypeStruct((b, t, d), BF16),
        compiler_params=_cparams(("parallel", "parallel")),
        name="conf_in",
    )(h, mod, g, w_in, b_in)


def _conf_conv_kernel(nt, cur_ref, prev_ref, next_ref, dw_ref, dwb_ref, lng_ref, lnb_ref,
                      w_ref, b_ref, h_ref, mod_ref, g_ref, out_ref, ext_ref, y_ref):
    prev_ok, next_ok = _halo_valid(nt)
    pad = CONF_CONV // 2
    ext_ref[0:HALO, :] = jnp.where(prev_ok, prev_ref[...].astype(F32), 0.0)
    ext_ref[HALO:HALO + TM, :] = cur_ref[...].astype(F32)
    ext_ref[HALO + TM:2 * HALO + TM, :] = jnp.where(next_ok, next_ref[...].astype(F32), 0.0)
    rw, cw = 64, 256
    d = cur_ref.shape[-1]
    for cc in range(d // cw):
        cs = slice(cc * cw, (cc + 1) * cw)
        for rr in range(TM // rw):
            base = HALO - pad + rr * rw
            acc = dw_ref[0:1, cs] * ext_ref[pl.ds(base, rw), cs]
            for tap in range(1, CONF_CONV):
                acc = acc + dw_ref[tap:tap + 1, cs] * ext_ref[pl.ds(base + tap, rw), cs]
            y_ref[rr * rw:(rr + 1) * rw, cs] = acc + dwb_ref[:, cs]
    y = y_ref[...]
    yc = y - jnp.mean(y, axis=-1, keepdims=True)
    yn = yc * lax.rsqrt(jnp.mean(yc * yc, axis=-1, keepdims=True) + LN_EPS) * lng_ref[...] + lnb_ref[...]
    out = _dot(_silu(yn).astype(BF16), w_ref[...]) + b_ref[...]
    out_ref[...] = _residual(h_ref[...], out, g_ref[...], mod_ref[2:3, :])


def _conf_conv(u, dw, dw_b, ln_g, ln_b, w_out, b_out, h, mod, g):
    b, t, d = h.shape
    nt = t // TM
    prev, nxt = _halo_specs(nt, d)
    vec = _full_spec((1, d))
    return pl.pallas_call(
        functools.partial(_conf_conv_kernel, nt),
        grid=(b, nt),
        in_specs=[_tok_spec(d), prev, nxt, _full_spec(dw.shape), vec, vec, vec,
                  _full_spec(w_out.shape), vec, _tok_spec(d), _mod_spec(nt, d), vec],
        out_specs=_tok_spec(d),
        out_shape=jax.ShapeDtypeStruct((b, t, d), F32),
        scratch_shapes=[pltpu.VMEM((TM + 2 * HALO, d), F32), pltpu.VMEM((TM, d), F32)],
        compiler_params=_cparams(("parallel", "parallel")),
        name="conf_conv",
    )(u, u, u, dw, dw_b, ln_g, ln_b, w_out, b_out, h, mod, g)


def _mla_proj_kernel(h_ref, mod_ref, g_ref, win_ref, qn_ref, kvn_ref, wuq_ref, wukv_ref, cs_ref,
                     q_ref, k_ref, v_ref):
    a16 = _norm_mod(h_ref[...], g_ref[...], mod_ref[0:1, :], mod_ref[1:2, :]).astype(BF16)
    p = _dot(a16, win_ref[...])
    cq = _rms(p[:, :MLA_Q_RANK]) * qn_ref[...]
    ckv = _rms(p[:, MLA_Q_RANK:MLA_Q_RANK + MLA_KV_RANK]) * kvn_ref[...]
    kr = p[:, MLA_Q_RANK + MLA_KV_RANK:]
    cs = cs_ref[...]
    kx = kr * cs
    krot = (kx + pltpu.roll(kx, MLA_ROPE, 1)).astype(BF16)
    scale = (MLA_NOPE + MLA_ROPE) ** -0.5 * LOG2E
    cq16 = cq.astype(BF16)
    ckv16 = ckv.astype(BF16)
    hw = MLA_NOPE + 2 * MLA_ROPE
    for hh in range(MLA_HEADS):
        qh = _dot(cq16, wuq_ref[:, hh * hw:(hh + 1) * hw]) * scale
        q_ref[:, hh * hw:hh * hw + MLA_NOPE] = qh[:, :MLA_NOPE].astype(BF16)
        q_ref[:, hh * hw + MLA_NOPE:(hh + 1) * hw] = (qh[:, MLA_NOPE:] * cs).astype(BF16)
        k_ref[:, hh * hw:hh * hw + MLA_NOPE] = _dot(
            ckv16, wukv_ref[:, hh * MLA_NOPE:(hh + 1) * MLA_NOPE]).astype(BF16)
        k_ref[:, hh * hw + MLA_NOPE:(hh + 1) * hw] = krot
        nk = MLA_HEADS * MLA_NOPE
        v_ref[:, 2 * hh * MLA_V:(2 * hh + 1) * MLA_V] = _dot(
            ckv16, wukv_ref[:, nk + hh * MLA_V:nk + (hh + 1) * MLA_V]).astype(BF16)
        v_ref[:, (2 * hh + 1) * MLA_V:(2 * hh + 2) * MLA_V] = jnp.ones((TM, MLA_V), BF16)


def _mla_proj(h, mod, g, w_in, q_norm, kv_norm, w_uq, w_ukv, cs_tab):
    b, t, d = h.shape
    nt = t // TM
    hw = MLA_NOPE + 2 * MLA_ROPE
    return pl.pallas_call(
        _mla_proj_kernel,
        grid=(b, nt),
        in_specs=[_tok_spec(d), _mod_spec(nt, d), _full_spec((1, d)), _full_spec(w_in.shape),
                  _full_spec(q_norm.shape), _full_spec(kv_norm.shape),
                  _full_spec(w_uq.shape), _full_spec(w_ukv.shape),
                  pl.BlockSpec((TM, LANES), lambda bb, j: (j, 0))],
        out_specs=[_tok_spec(MLA_HEADS * hw), _tok_spec(MLA_HEADS * hw), _tok_spec(MLA_HEADS * 2 * MLA_V)],
        out_shape=[jax.ShapeDtypeStruct((b, t, MLA_HEADS * hw), BF16),
                   jax.ShapeDtypeStruct((b, t, MLA_HEADS * hw), BF16),
                   jax.ShapeDtypeStruct((b, t, MLA_HEADS * 2 * MLA_V), BF16)],
        compiler_params=_cparams(("parallel", "parallel")),
        name="mla_proj",
    )(h, mod, g, w_in, q_norm, kv_norm, w_uq, w_ukv, cs_tab)


def _attn_kernel(tk, q_ref, k_ref, v_ref, o_ref, m_ref, acc_ref):
    q = q_ref[...]
    for i in range(k_ref.shape[0] // tk):
        rows = slice(i * tk, (i + 1) * tk)
        s = _dot_nt(q, k_ref[rows, :])
        m_blk = jnp.max(s, axis=-1, keepdims=True)
        if i == 0:
            m_new = m_blk
            acc_ref[...] = _dot(jnp.exp2(s - m_new).astype(BF16), v_ref[rows, :])
        else:
            m_old = m_ref[...]
            m_new = jnp.maximum(m_old, m_blk)
            acc_ref[...] = jnp.exp2(m_old - m_new) * acc_ref[...] + _dot(
                jnp.exp2(s - m_new).astype(BF16), v_ref[rows, :])
        m_ref[...] = m_new
    acc = acc_ref[...]
    o_ref[...] = (acc[:, :MLA_V] / acc[:, MLA_V:]).astype(BF16)


def _attention(q, k, v, tq, q_rows, q_row0, k_rows, k_row0, name):
    b = q.shape[0]
    hw = MLA_NOPE + 2 * MLA_ROPE
    tk = next(c for c in (768, 512, 256) if k_rows % c == 0)
    q0 = q_row0 // tq
    k0 = k_row0 // k_rows
    return pl.pallas_call(
        functools.partial(_attn_kernel, tk),
        grid=(b, MLA_HEADS, q_rows // tq),
        in_specs=[pl.BlockSpec((None, tq, hw), lambda bb, hh, i: (bb, q0 + i, hh)),
                  pl.BlockSpec((None, k_rows, hw), lambda bb, hh, i: (bb, k0, hh)),
                  pl.BlockSpec((None, k_rows, 2 * MLA_V), lambda bb, hh, i: (bb, k0, hh))],
        out_specs=pl.BlockSpec((None, tq, MLA_V), lambda bb, hh, i: (bb, i, hh)),
        out_shape=jax.ShapeDtypeStruct((b, q_rows, MLA_HEADS * MLA_V), BF16),
        scratch_shapes=[pltpu.VMEM((tq, 1), F32), pltpu.VMEM((tq, 2 * MLA_V), F32)],
        compiler_params=_cparams(("parallel", "parallel", "arbitrary")),
        name=name,
    )(q, k, v)


def _proj_out_kernel(o_ref, w_ref, h_ref, mod_ref, g_ref, out_ref):
    out = _dot(o_ref[...], w_ref[...])
    out_ref[...] = _residual(h_ref[...], out, g_ref[...], mod_ref[2:3, :])


def _proj_out(o, w, h, mod, g):
    b, t, d = h.shape
    nt = t // TM
    return pl.pallas_call(
        _proj_out_kernel,
        grid=(b, nt),
        in_specs=[_tok_spec(o.shape[-1]), _full_spec(w.shape), _tok_spec(d), _mod_spec(nt, d),
                  _full_spec((1, d))],
        out_specs=_tok_spec(d),
        out_shape=jax.ShapeDtypeStruct((b, t, d), F32),
        compiler_params=_cparams(("parallel", "parallel")),
        name="mla_out",
    )(o, w, h, mod, g)


def _ffn_kernel(fc, h_ref, mod_ref, g2_ref, wg_ref, wu_ref, wd_ref, g3_ref, out_ref, acc_ref):
    h = h_ref[...]
    a16 = _norm_mod(h, g2_ref[...], mod_ref[3:4, :], mod_ref[4:5, :]).astype(BF16)
    for c in range(wg_ref.shape[-1] // fc):
        fs = slice(c * fc, (c + 1) * fc)
        act = (_silu(_dot(a16, wg_ref[:, fs])) * _dot(a16, wu_ref[:, fs])).astype(BF16)
        part = _dot(act, wd_ref[fs, :])
        if c == 0:
            acc_ref[...] = part
        else:
            acc_ref[...] += part
    out_ref[...] = _residual(h, acc_ref[...], g3_ref[...], mod_ref[5:6, :])


def _ffn(h, mod, g2, w_gate, w_up, w_down, g3, n_tiles):
    b, t, d = h.shape
    nt = t // TM
    return pl.pallas_call(
        functools.partial(_ffn_kernel, 256),
        grid=(b, n_tiles),
        in_specs=[_tok_spec(d), _mod_spec(nt, d), _full_spec((1, d)), _full_spec(w_gate.shape),
                  _full_spec(w_up.shape), _full_spec(w_down.shape), _full_spec((1, d))],
        out_specs=_tok_spec(d),
        out_shape=jax.ShapeDtypeStruct((b, n_tiles * TM, d), F32),
        scratch_shapes=[pltpu.VMEM((TM, d), F32)],
        compiler_params=_cparams(("parallel", "parallel")),
        name="ffn",
    )(h, mod, g2, w_gate, w_up, w_down, g3)


def _moe_route_kernel(h_ref, mod_ref, g2_ref, wr_ref, a_ref, rt_ref):
    lane = lax.broadcasted_iota(jnp.int32, (TM, LANES), 1).astype(F32)
    a = _norm_mod(h_ref[...], g2_ref[...], mod_ref[3:4, :], mod_ref[4:5, :])
    for s in range(SUBLANES):
        a_ref[pl.ds(s, TM, stride=SUBLANES), :] = a[:, s * LANES:(s + 1) * LANES]
    logits = jnp.where(lane < N_EXPERTS, _dot(a, wr_ref[...], precision=HIGHEST), -jnp.inf)
    m1 = jnp.max(logits, axis=-1, keepdims=True)
    i1 = jnp.min(jnp.where(logits == m1, lane, float(LANES)), axis=-1, keepdims=True)
    rest = jnp.where(lane == i1, -jnp.inf, logits)
    m2 = jnp.max(rest, axis=-1, keepdims=True)
    i2 = jnp.min(jnp.where(rest == m2, lane, float(LANES)), axis=-1, keepdims=True)
    e2 = jnp.exp(m2 - m1)
    w1 = 1.0 / (1.0 + e2)
    rt_ref[...] = (jnp.where(lane == 0.0, i1, 0.0) + jnp.where(lane == 1.0, i2, 0.0)
                   + jnp.where(lane == 2.0, w1, 0.0) + jnp.where(lane == 3.0, e2 * w1, 0.0))


def _moe_route(h, mod, g2, w_router, n_tiles):
    b, t, d = h.shape
    nt = t // TM
    return pl.pallas_call(
        _moe_route_kernel,
        grid=(b, n_tiles),
        in_specs=[_tok_spec(d), _mod_spec(nt, d), _full_spec((1, d)), _full_spec(w_router.shape)],
        out_specs=[pl.BlockSpec((TM * SUBLANES, LANES), lambda bb, j: (bb * n_tiles + j, 0)),
                   _tok_spec(LANES)],
        out_shape=[jax.ShapeDtypeStruct((b * n_tiles * TM * SUBLANES, LANES), F32),
                   jax.ShapeDtypeStruct((b, n_tiles * TM, LANES), F32)],
        compiler_params=_cparams(("parallel", "parallel")),
        name="moe_route",
    )(h, mod, g2, w_router)


def _gather_tiles(idx_ref, base, n, src_hbm, dst, sem):
    def body(r, carry):
        row = pl.multiple_of(idx_ref[base + r], SUBLANES)
        pltpu.make_async_copy(src_hbm.at[pl.ds(row, SUBLANES), :],
                              dst.at[pl.ds(pl.multiple_of(r * SUBLANES, SUBLANES), SUBLANES), :], sem).start()
        return carry
    lax.fori_loop(0, n, body, 0, unroll=8)


def _wait_tiles(n, src_hbm, dst, sem):
    pltpu.make_async_copy(src_hbm.at[pl.ds(0, n * SUBLANES), :], dst, sem).wait()


def _untile(buf, n):
    return jnp.concatenate([buf[pl.ds(s, n, stride=SUBLANES), :] for s in range(SUBLANES)], axis=1)


def _moe_experts_kernel(te_ref, nact_ref, src_ref, a_hbm, wg_ref, wu_ref, wd_ref, y_ref, xbuf, sem):
    i = pl.program_id(0)
    nact = nact_ref[0]

    @pl.when(i == 0)
    def _():
        _gather_tiles(src_ref, 0, TM, a_hbm, xbuf.at[0], sem.at[0])

    @pl.when(i + 1 < nact)
    def _():
        nxt = (i + 1) % 2
        _gather_tiles(src_ref, (i + 1) * TM, TM, a_hbm, xbuf.at[nxt], sem.at[nxt])

    @pl.when(i < nact)
    def _():
        cur = i % 2
        _wait_tiles(TM, a_hbm, xbuf.at[cur], sem.at[cur])
        x16 = _untile(xbuf.at[cur], TM).astype(BF16)
        f = wg_ref.shape[-1]
        acc = None
        for c0 in range(0, f, 512):
            fs = slice(c0, min(c0 + 512, f))
            act = (_silu(_dot(x16, wg_ref[:, fs])) * _dot(x16, wu_ref[:, fs])).astype(BF16)
            part = _dot(act, wd_ref[fs, :])
            acc = part if acc is None else acc + part
        for s in range(SUBLANES):
            y_ref[pl.ds(s, TM, stride=SUBLANES), :] = acc[:, s * LANES:(s + 1) * LANES]

    @pl.when(i >= nact)
    def _():
        y_ref[...] = jnp.zeros_like(y_ref)


def _moe_experts(tile_expert, n_active, src_rows, a2d, w_gate, w_up, w_down):
    n_tiles = tile_expert.shape[0]
    _, d, f = w_gate.shape
    wspec_in = pl.BlockSpec((None, d, f), lambda i, te, na, sr: (te[i], 0, 0))
    wspec_out = pl.BlockSpec((None, f, d), lambda i, te, na, sr: (te[i], 0, 0))
    return pl.pallas_call(
        _moe_experts_kernel,
        grid_spec=pltpu.PrefetchScalarGridSpec(
            num_scalar_prefetch=3,
            grid=(n_tiles,),
            in_specs=[pl.BlockSpec(memory_space=pl.ANY), wspec_in, wspec_in, wspec_out],
            out_specs=pl.BlockSpec((TM * SUBLANES, LANES), lambda i, te, na, sr: (i, 0)),
            scratch_shapes=[pltpu.VMEM((2, TM * SUBLANES, LANES), F32), pltpu.SemaphoreType.DMA((2,))]),
        out_shape=jax.ShapeDtypeStruct((n_tiles * TM * SUBLANES, LANES), F32),
        compiler_params=_cparams(("arbitrary",)),
        name="moe_experts",
    )(tile_expert, n_active, src_rows, a2d, w_gate, w_up, w_down)


def _moe_combine_kernel(n_tiles, dst_ref, y_hbm, rt_ref, h_ref, mod_ref, g3_ref, out_ref, ybuf, sem):
    step = pl.program_id(0) * n_tiles + pl.program_id(1)
    nsteps = pl.num_programs(0) * n_tiles

    @pl.when(step == 0)
    def _():
        _gather_tiles(dst_ref, 0, 2 * TM, y_hbm, ybuf.at[0], sem.at[0])

    @pl.when(step + 1 < nsteps)
    def _():
        nxt = (step + 1) % 2
        _gather_tiles(dst_ref, (step + 1) * 2 * TM, 2 * TM, y_hbm, ybuf.at[nxt], sem.at[nxt])

    cur = step % 2
    _wait_tiles(2 * TM, y_hbm, ybuf.at[cur], sem.at[cur])
    yb = ybuf.at[cur]
    w1 = rt_ref[:, 2:3]
    w2 = rt_ref[:, 3:4]
    f = jnp.concatenate(
        [w1 * yb[pl.ds(s, TM, stride=2 * SUBLANES), :] + w2 * yb[pl.ds(SUBLANES + s, TM, stride=2 * SUBLANES), :]
         for s in range(SUBLANES)], axis=1)
    out_ref[...] = _residual(h_ref[...], f, g3_ref[...], mod_ref[5:6, :])


def _moe_combine(dst_rows, y2d, rt, h, mod, g3, n_tiles):
    b, t, d = h.shape
    nt = t // TM
    tok = lambda c: pl.BlockSpec((None, TM, c), lambda bb, j, ds: (bb, j, 0))
    return pl.pallas_call(
        functools.partial(_moe_combine_kernel, n_tiles),
        grid_spec=pltpu.PrefetchScalarGridSpec(
            num_scalar_prefetch=1,
            grid=(b, n_tiles),
            in_specs=[pl.BlockSpec(memory_space=pl.ANY), tok(LANES), tok(d),
                      pl.BlockSpec((None, SUBLANES, d),
                                   lambda bb, j, ds: (jnp.where(j == nt - 1, 0, bb + 1), 0, 0)),
                      pl.BlockSpec((1, d), lambda bb, j, ds: (0, 0))],
            out_specs=tok(d),
            scratch_shapes=[pltpu.VMEM((2, 2 * TM * SUBLANES, LANES), F32), pltpu.SemaphoreType.DMA((2,))]),
        out_shape=jax.ShapeDtypeStruct((b, n_tiles * TM, d), F32),
        compiler_params=_cparams(("arbitrary", "arbitrary")),
        name="moe_combine",
    )(dst_rows, y2d, rt, h, mod, g3)


def _moe(h, mod, g2, w_router, w_gate, w_up, w_down, g3, n_tiles):
    b = h.shape[0]
    n_tok = b * n_tiles * TM
    a2d, rt = _moe_route(h, mod, g2, w_router, n_tiles)
    expert = rt.reshape(n_tok, LANES)[:, :2].astype(jnp.int32).reshape(-1)
    onehot = (expert[:, None] == jnp.arange(N_EXPERTS)[None, :]).astype(jnp.int32)
    rank = jnp.sum((jnp.cumsum(onehot, axis=0) - onehot) * onehot, axis=1)
    count = jnp.sum(onehot, axis=0)
    padded = (count + TM - 1) // TM * TM
    ends = jnp.cumsum(padded)
    dest = (ends - padded)[expert] + rank
    n_tiles_max = 2 * n_tok // TM + N_EXPERTS
    token_row = (jnp.arange(2 * n_tok, dtype=jnp.int32) // 2) * SUBLANES
    src_rows = jnp.zeros((n_tiles_max * TM,), jnp.int32).at[dest].set(token_row)
    tile_start = jnp.arange(n_tiles_max, dtype=jnp.int32) * TM
    tile_expert = jnp.minimum(jnp.sum(ends[None, :] <= tile_start[:, None], axis=1), N_EXPERTS - 1)
    n_active = (ends[-1] // TM).reshape(1)
    y2d = _moe_experts(tile_expert.astype(jnp.int32), n_active.astype(jnp.int32), src_rows, a2d,
                       w_gate, w_up, w_down)
    return _moe_combine((dest * SUBLANES).astype(jnp.int32), y2d, rt, h, mod, g3, n_tiles)


def _pad_rows(a, rows):
    return jnp.pad(a, ((0, rows - a.shape[0]),) + ((0, 0),) * (a.ndim - 1))


def _rope_table(seq, ctx_len):
    n_freq = MLA_ROPE // 4
    t = jnp.arange(seq)
    pos = jnp.stack([(t // GRID_W).astype(F32), (t % GRID_W).astype(F32)], axis=1)
    inv = ROPE_THETA ** (-jnp.arange(n_freq, dtype=F32) / n_freq)
    ang = pos[:, :, None] * inv
    cos = jnp.repeat(jnp.cos(ang)[:, :, None, :], 2, axis=2).reshape(seq, MLA_ROPE)
    sin = jnp.stack([-jnp.sin(ang), jnp.sin(ang)], axis=2).reshape(seq, MLA_ROPE)
    lat = jnp.concatenate([cos, sin], axis=1)
    ctx = jnp.concatenate([jnp.ones((ctx_len, MLA_ROPE), F32), jnp.zeros((ctx_len, MLA_ROPE), F32)], axis=1)
    return jnp.concatenate([lat, ctx], axis=0)


def _swap_halves(w):
    shp = w.shape
    return jnp.flip(w.reshape(shp[:-1] + (2, 2, MLA_ROPE // 4)), axis=-2).reshape(shp)


def kernel(x, c, ctx, c_ctx, w_mod, b_mod, norm_g, gdn_w_in, gdn_conv, gdn_a_log, gdn_dt_bias, gdn_o_norm, gdn_w_out, conf_w_in, conf_b_in, conf_dw, conf_dw_b, conf_ln_g, conf_ln_b, conf_w_out, conf_b_out, mla_w_in, mla_q_norm, mla_kv_norm, mla_w_uq, mla_w_ukv, mla_w_o, ffn_w_gate, ffn_w_up, ffn_w_down, moe_router, moe_w_gate, moe_w_up, moe_w_down):
    bsz, seq, d = x.shape
    ctx_len = ctx.shape[1]
    depth = w_mod.shape[0]
    assert ctx_len == TM and seq % (2 * TM) == 0 and bsz + 1 <= SUBLANES
    nt = (seq + ctx_len) // TM
    n_lat = seq // TM

    cvec = _pad_rows(jnp.concatenate([c_ctx[None, :], c], axis=0), SUBLANES)
    mods = _adaln(cvec, w_mod, b_mod)
    mods = mods.reshape(depth, SUBLANES, N_MOD, d)[:, :bsz + 1]
    mods = jnp.pad(mods, ((0, 0), (0, 0), (0, SUBLANES - N_MOD), (0, 0)))

    h = jnp.concatenate([x, ctx], axis=1)
    rope_cs = _rope_table(seq, ctx_len)

    n_mix = 3
    for i in range(depth):
        last = i == depth - 1
        mod = mods[i]
        g = norm_g[i].reshape(4, 1, d)
        kind, j = i % n_mix, i // n_mix
        if kind == 0:
            nqk = GDN_HEADS * GDN_DK
            nmain = 2 * nqk + 2 * GDN_HEADS * GDN_DV
            w_in = gdn_w_in[j]
            w_main = w_in[:, :nmain].astype(BF16)
            w_tail = w_in[:, nmain:]
            nh = GDN_HEADS
            w_ab = jnp.stack([
                jnp.pad(jnp.concatenate([w_tail[:, dr * nh:(dr + 1) * nh],
                                         w_tail[:, (2 + dr) * nh:(3 + dr) * nh]], axis=1),
                        ((0, 0), (0, LANES - 2 * nh))) for dr in range(2)])
            lane_pad = ((0, 0), (nh, LANES - 2 * nh))
            cf = jnp.stack([jnp.pad(jnp.exp(gdn_a_log[j]), lane_pad),
                            jnp.pad(gdn_dt_bias[j], lane_pad)], axis=1)
            cf = jnp.pad(cf, ((0, 0), (0, SUBLANES - 2), (0, 0)))
            qkv, gate, ab = _gdn_proj(h, mod, g[0], w_main, w_ab)
            q, k, v = _gdn_conv(qkv, _pad_rows(gdn_conv[j], SUBLANES))
            o_f = _gdn_scan(q, k, v, ab, cf, reverse=False)
            o_b = _gdn_scan(q, k, v, ab, cf, reverse=True)
            h = _gdn_out(o_f, o_b, gate, h, mod, gdn_o_norm[j].reshape(1, -1),
                         gdn_w_out[j].astype(BF16), g[1])
        elif kind == 1:
            u = _conf_in(h, mod, g[0], conf_w_in[j].astype(BF16), conf_b_in[j].reshape(1, -1))
            h = _conf_conv(u, _pad_rows(conf_dw[j], 32), conf_dw_b[j].reshape(1, -1),
                           conf_ln_g[j].reshape(1, -1), conf_ln_b[j].reshape(1, -1),
                           conf_w_out[j].astype(BF16), conf_b_out[j].reshape(1, -1), h, mod, g[1])
        else:
            w_in = mla_w_in[j]
            w_kr = w_in[:, MLA_Q_RANK + MLA_KV_RANK:]
            w_in2 = jnp.concatenate([w_in, _swap_halves(w_kr)], axis=1).astype(BF16)
            hq = MLA_NOPE + MLA_ROPE
            w_uq = mla_w_uq[j].reshape(MLA_Q_RANK, MLA_HEADS, hq)
            w_uq2 = jnp.concatenate([w_uq, _swap_halves(w_uq[..., MLA_NOPE:])], axis=-1)
            w_uq2 = w_uq2.reshape(MLA_Q_RANK, -1).astype(BF16)
            w_ukv = mla_w_ukv[j].reshape(MLA_KV_RANK, MLA_HEADS, MLA_NOPE + MLA_V)
            w_ukv2 = jnp.concatenate([w_ukv[..., :MLA_NOPE].reshape(MLA_KV_RANK, -1),
                                      w_ukv[..., MLA_NOPE:].reshape(MLA_KV_RANK, -1)], axis=1).astype(BF16)
            qa, ka, va = _mla_proj(h, mod, g[0], w_in2, mla_q_norm[j].reshape(1, -1),
                                   mla_kv_norm[j].reshape(1, -1), w_uq2, w_ukv2, rope_cs)
            o_l = _attention(qa, ka, va, 2 * TM, seq, 0, seq + ctx_len, 0, "attn_latent")
            o_c = _attention(qa, ka, va, TM, ctx_len, seq, ctx_len, seq, "attn_context")
            o = jnp.concatenate([o_l, o_c], axis=1)
            h = _proj_out(o, mla_w_o[j].astype(BF16), h, mod, g[1])

        n_tiles = n_lat if last else nt
        f = i // 2
        if i % 2 == 0:
            h = _ffn(h, mod, g[2], ffn_w_gate[f].astype(BF16), ffn_w_up[f].astype(BF16),
                     ffn_w_down[f].astype(BF16), g[3], n_tiles)
        else:
            w_r = jnp.pad(moe_router[f], ((0, 0), (0, LANES - N_EXPERTS)))
            h = _moe(h, mod, g[2], w_r, moe_w_gate[f].astype(BF16), moe_w_up[f].astype(BF16),
                     moe_w_down[f].astype(BF16), g[3], n_tiles)
    return h
```

```python
import functools

import jax
import jax.numpy as jnp
from jax import lax
from jax.experimental import pallas as pl
from jax.experimental.pallas import tpu as pltpu

F32 = jnp.float32
BF16 = jnp.bfloat16
HIGHEST = lax.Precision.HIGHEST

N_MOD = 6
RMS_EPS = 1e-6
LN_EPS = 1e-5
GRID_W = 64
GDN_HEADS = 8
GDN_DK = 128
GDN_DV = 128
GDN_CONV = 5
GDN_CHUNK = 64
CONF_CONV = 31
MLA_HEADS = 8
MLA_NOPE = 128
MLA_ROPE = 64
MLA_V = 128
MLA_Q_RANK = 512
MLA_KV_RANK = 256
ROPE_THETA = 10000.0
LOG2E = 1.4426950408889634
N_EXPERTS = 8

LANES = 128
SUBLANES = 8
TM = 256
HALO = 16
VMEM_LIMIT = 56 * 1024 * 1024


def _cparams(sem):
    return pltpu.CompilerParams(dimension_semantics=sem, vmem_limit_bytes=VMEM_LIMIT)


def _dot(a, b, **kw):
    return jnp.dot(a, b, preferred_element_type=F32, **kw)


def _dot_nt(a, b):
    return lax.dot_general(a, b, (((1,), (1,)), ((), ())), preferred_element_type=F32)


def _dot_tn(a, b):
    return lax.dot_general(a, b, (((0,), (0,)), ((), ())), preferred_element_type=F32)


def _silu(x):
    return x * jax.nn.sigmoid(x)


def _rms(x, eps=RMS_EPS):
    return x * lax.rsqrt(jnp.mean(x * x, axis=-1, keepdims=True) + eps)


def _norm_mod(h, g, shift, scale):
    return _rms(h) * g * (1.0 + scale) + shift


def _residual(h, o, g, gate):
    return h + gate * (_rms(o) * g)


def _tok_spec(c):
    return pl.BlockSpec((None, TM, c), lambda b, j: (b, j, 0))


def _full_spec(shape):
    nd = len(shape)
    return pl.BlockSpec(shape, lambda *_: (0,) * nd)


def _mod_spec(nt, d):
    return pl.BlockSpec((None, SUBLANES, d), lambda b, j: (jnp.where(j == nt - 1, 0, b + 1), 0, 0))


def _halo_specs(nt, c):
    per = TM // HALO
    prev = pl.BlockSpec((None, HALO, c), lambda b, j: (b, jnp.maximum(j * per - 1, 0), 0))
    nxt = pl.BlockSpec((None, HALO, c), lambda b, j: (b, jnp.minimum((j + 1) * per, nt * per - 1), 0))
    return prev, nxt


def _halo_valid(nt):
    j = pl.program_id(1)
    prev_ok = jnp.logical_and(j != 0, j != nt - 1)
    next_ok = j < nt - 2
    return prev_ok, next_ok


def _adaln_kernel(c_ref, w_ref, b_ref, o_ref):
    o_ref[...] = _dot(_silu(c_ref[...]), w_ref[...], precision=HIGHEST) + b_ref[...]


def _adaln(cvec, w_mod, b_mod):
    depth, d, n = w_mod.shape
    tn = 1536
    return pl.pallas_call(
        _adaln_kernel,
        grid=(depth, n // tn),
        in_specs=[pl.BlockSpec((SUBLANES, d), lambda i, k: (0, 0)),
                  pl.BlockSpec((None, d, tn), lambda i, k: (i, 0, k)),
                  pl.BlockSpec((None, 1, tn), lambda i, k: (i, 0, k))],
        out_specs=pl.BlockSpec((None, SUBLANES, tn), lambda i, k: (i, 0, k)),
        out_shape=jax.ShapeDtypeStruct((depth, SUBLANES, n), F32),
        compiler_params=_cparams(("parallel", "parallel")),
        name="adaln",
    )(cvec, w_mod, b_mod.reshape(depth, 1, n))


def _gdn_proj_kernel(h_ref, mod_ref, g_ref, w_ref, wab_ref, qkv_ref, gate_ref, ab_ref):
    a = _norm_mod(h_ref[...], g_ref[...], mod_ref[0:1, :], mod_ref[1:2, :])
    a16 = a.astype(BF16)
    nq = qkv_ref.shape[-1]
    for n in range(nq // 512):
        qkv_ref[:, n * 512:(n + 1) * 512] = _dot(a16, w_ref[:, n * 512:(n + 1) * 512]).astype(BF16)
    for n in range(gate_ref.shape[-1] // 512):
        gate_ref[:, n * 512:(n + 1) * 512] = _dot(a16, w_ref[:, nq + n * 512:nq + (n + 1) * 512]).astype(BF16)
    for dr in range(2):
        ab_ref[dr] = _dot(a, wab_ref[dr], precision=HIGHEST)


def _gdn_proj(h, mod, g, w_main, w_ab):
    b, t, d = h.shape
    nt = t // TM
    nq = 2 * GDN_HEADS * GDN_DK + GDN_HEADS * GDN_DV
    nv = GDN_HEADS * GDN_DV
    return pl.pallas_call(
        _gdn_proj_kernel,
        grid=(b, nt),
        in_specs=[_tok_spec(d), _mod_spec(nt, d), _full_spec((1, d)),
                  _full_spec(w_main.shape), _full_spec(w_ab.shape)],
        out_specs=[_tok_spec(nq), _tok_spec(nv),
                   pl.BlockSpec((None, 2, TM, LANES), lambda bb, j: (bb, 0, j, 0))],
        out_shape=[jax.ShapeDtypeStruct((b, t, nq), BF16),
                   jax.ShapeDtypeStruct((b, t, nv), BF16),
                   jax.ShapeDtypeStruct((b, 2, t, LANES), F32)],
        compiler_params=_cparams(("parallel", "parallel")),
        name="gdn_proj",
    )(h, mod, g, w_main, w_ab)


def _gdn_conv_kernel(nt, cur_ref, prev_ref, next_ref, w_ref, q_ref, k_ref, v_ref, ext_ref):
    prev_ok, next_ok = _halo_valid(nt)
    cw = 512
    pad = GDN_CONV // 2
    nqk = GDN_HEADS * GDN_DK
    for cc in range(cur_ref.shape[-1] // cw):
        cs = slice(cc * cw, (cc + 1) * cw)
        ext_ref[0:SUBLANES, :] = jnp.where(prev_ok, prev_ref[HALO - SUBLANES:HALO, cs].astype(F32), 0.0)
        ext_ref[SUBLANES:SUBLANES + TM, :] = cur_ref[:, cs].astype(F32)
        ext_ref[SUBLANES + TM:2 * SUBLANES + TM, :] = jnp.where(next_ok, next_ref[0:SUBLANES, cs].astype(F32), 0.0)
        acc = w_ref[0:1, cs] * ext_ref[pl.ds(SUBLANES - pad, TM), :]
        for tap in range(1, GDN_CONV):
            acc = acc + w_ref[tap:tap + 1, cs] * ext_ref[pl.ds(SUBLANES - pad + tap, TM), :]
        y = _silu(acc)
        for hh in range(cw // LANES):
            col = cc * cw + hh * LANES
            yh = y[:, hh * LANES:(hh + 1) * LANES]
            if col < 2 * nqk:
                yh = yh * lax.rsqrt(jnp.sum(yh * yh, axis=-1, keepdims=True) + RMS_EPS)
            if col < nqk:
                q_ref[:, col:col + LANES] = yh.astype(BF16)
            elif col < 2 * nqk:
                k_ref[:, col - nqk:col - nqk + LANES] = yh.astype(BF16)
            else:
                v_ref[:, col - 2 * nqk:col - 2 * nqk + LANES] = yh.astype(BF16)


def _gdn_conv(qkv, conv_w):
    b, t, c = qkv.shape
    nt = t // TM
    nqk = GDN_HEADS * GDN_DK
    prev, nxt = _halo_specs(nt, c)
    return pl.pallas_call(
        functools.partial(_gdn_conv_kernel, nt),
        grid=(b, nt),
        in_specs=[_tok_spec(c), prev, nxt, _full_spec(conv_w.shape)],
        out_specs=[_tok_spec(nqk), _tok_spec(nqk), _tok_spec(GDN_HEADS * GDN_DV)],
        out_shape=[jax.ShapeDtypeStruct((b, t, nqk), BF16),
                   jax.ShapeDtypeStruct((b, t, nqk), BF16),
                   jax.ShapeDtypeStruct((b, t, GDN_HEADS * GDN_DV), BF16)],
        scratch_shapes=[pltpu.VMEM((TM + 2 * SUBLANES, 512), F32)],
        compiler_params=_cparams(("parallel", "parallel")),
        name="gdn_conv",
    )(qkv, qkv, qkv, conv_w)


def _split(x):
    hi = x.astype(BF16)
    return hi, (x - hi.astype(F32)).astype(BF16)


def _dot16(a, b):
    return _dot(a.astype(BF16), b.astype(BF16))


def _dot_hi(a, b):
    ah, al = _split(a)
    bh, bl = _split(b)
    return _dot(ah, bh) + (_dot(ah, bl) + _dot(al, bh))


def _gdn_scan_kernel(reverse, nb, q_ref, k_ref, v_ref, ab_ref, cf_ref, o_ref,
                     s_ref, gc_ref, gct_ref, beta_ref, u_ref, wq_ref, kd_ref, at_ref):
    nchunk = TM // GDN_CHUNK
    c64 = GDN_CHUNK
    scale = GDN_DK ** -0.5
    chains = [(b, h) for b in range(nb) for h in range(GDN_HEADS)]

    @pl.when(pl.program_id(0) == 0)
    def _():
        s_ref[...] = jnp.zeros_like(s_ref)

    row = lax.broadcasted_iota(jnp.int32, (TM, TM), 0)
    col = lax.broadcasted_iota(jnp.int32, (TM, TM), 1)
    same = (row // c64) == (col // c64)
    tri = (col >= row) if reverse else (col <= row)
    cum = jnp.where(jnp.logical_and(same, tri), 1.0, 0.0)
    for b in range(nb):
        ab = ab_ref[b]
        beta_ref[b] = jax.nn.sigmoid(ab)
        x = ab + cf_ref[1:2, :]
        softplus = jnp.maximum(x, 0.0) + jnp.log1p(jnp.exp(-jnp.abs(x)))
        g = -cf_ref[0:1, :] * softplus
        gc = _dot(cum, g, precision=HIGHEST)
        gc_ref[b] = gc
        for c in range(nchunk):
            blk = gc[c * c64:(c + 1) * c64, :]
            gct_ref[b, c] = jnp.concatenate([blk, blk], axis=0).T

    ri = lax.broadcasted_iota(jnp.int32, (c64, LANES), 0)
    ln = lax.broadcasted_iota(jnp.int32, (c64, LANES), 1)
    ci = ln % c64
    left = ln < c64
    causal = (ri <= ci) if reverse else (ri >= ci)
    strict = (ri < ci) if reverse else (ri > ci)
    blk16 = (ri // 16) == (ci // 16)
    blk32 = (ri // 32) == (ci // 32)
    m_c1 = jnp.logical_and(strict, jnp.logical_and(blk32, jnp.logical_not(blk16)))[:, :c64]
    m_c2 = jnp.logical_and(strict, jnp.logical_not(blk32))[:, :c64]
    m_l0x = jnp.logical_and(jnp.logical_and(strict, blk16), jnp.logical_not(left))
    eye_l = jnp.where(jnp.logical_and(ri == ci, left), 1.0, 0.0)
    eye64 = eye_l[:, :c64]
    strict64 = strict[:, :c64]
    zpad = jnp.zeros((c64, LANES), F32)

    def prep(c, carry):
        r0 = pl.multiple_of(c * c64, c64)
        rows = pl.ds(r0, c64)
        r_last = r0 if reverse else r0 + (c64 - 1)
        gcb = [gc_ref[b, rows, :] for b in range(nb)]
        gctb = [gct_ref[b, c] for b in range(nb)]
        glb = [gc_ref[b, pl.ds(r_last, 1), :] for b in range(nb)]
        betab = [beta_ref[b, rows, :] for b in range(nb)]
        st = []
        for (b, h) in chains:
            hs = slice(h * GDN_DK, (h + 1) * GDN_DK)
            gl = SUBLANES + h
            k16 = k_ref[b, rows, hs]
            k32 = k16.astype(F32)
            beta = betab[b][:, h:h + 1]
            gcol = gcb[b][:, gl:gl + 1]
            dec2 = jnp.exp(jnp.minimum(gcol - gctb[b][gl:gl + 1, :], 0.0))
            kb = k32 * beta
            kk2 = _dot_nt(kb.astype(BF16), jnp.concatenate([k16, k16], axis=0))
            ll2 = kk2 * dec2
            st.append(dict(hs=hs, b=b, h=h, k32=k32, k16=k16, beta=beta, gcol=gcol, kb=kb,
                           glast=glb[b][:, gl:gl + 1], dec=dec2[:, :c64],
                           ll=ll2[:, :c64], x=jnp.where(m_l0x, -ll2, eye_l)))
        for _ in range(4):
            for s in st:
                x = s["x"]
                s["x"] = jnp.where(left, x, 0.0) + _dot16(x, jnp.concatenate([zpad, x], axis=0))
        for s in st:
            s["d"] = s["x"][:, :c64]
        for mask in (m_c1, m_c2):
            for s in st:
                s["t"] = _dot16(jnp.where(mask, s["ll"], 0.0), s["d"])
            for s in st:
                s["d"] = s["d"] - _dot16(s["d"], s["t"])
        for s in st:
            s["r"] = eye64 - s["d"] - _dot_hi(jnp.where(strict64, s["ll"], 0.0), s["d"])
        for s in st:
            s["d"] = s["d"] + _dot16(s["d"], s["r"])
        for s in st:
            eg = jnp.exp(s["gcol"])
            s["eg"] = eg
            v32 = v_ref[s["b"], rows, s["hs"]].astype(F32)
            s["u"] = _dot_hi(s["d"], v32 * s["beta"])
            s["w"] = _dot16(s["d"], s["kb"] * eg)
        for s in st:
            b, h, hs = s["b"], s["h"], s["hs"]
            qs = q_ref[b, rows, hs].astype(F32) * scale
            attn = _dot_nt(qs.astype(BF16), s["k16"]) * jnp.where(causal[:, :c64], s["dec"], 0.0)
            u_ref[b, rows, hs] = s["u"]
            wq_ref[b, c, h, 0:c64, :] = s["w"].astype(BF16)
            wq_ref[b, c, h, c64:2 * c64, :] = (qs * s["eg"]).astype(BF16)
            kd_ref[b, rows, hs] = (s["k32"] * jnp.exp(s["glast"] - s["gcol"])).astype(BF16)
            at_ref[b, c, h] = attn.astype(BF16)
        return carry

    lax.fori_loop(0, nchunk, prep, 0)

    def scan(step, carry):
        c = (nchunk - 1 - step) if reverse else step
        r0 = pl.multiple_of(c * c64, c64)
        rows = pl.ds(r0, c64)
        r_last = r0 if reverse else r0 + (c64 - 1)
        glb = [gc_ref[b, pl.ds(r_last, 1), :] for b in range(nb)]
        st = []
        for (b, h) in chains:
            s = s_ref[b, h]
            st.append(dict(b=b, h=h, hs=slice(h * GDN_DK, (h + 1) * GDN_DK), s=s,
                           ws=_dot(wq_ref[b, c, h], s.astype(BF16))))
        for d in st:
            v_new = u_ref[d["b"], rows, d["hs"]] - d["ws"][:c64]
            d["vn"] = v_new.astype(BF16)
        for d in st:
            b, h, hs = d["b"], d["h"], d["hs"]
            o = d["ws"][c64:] + _dot(at_ref[b, c, h], d["vn"])
            o_ref[b, rows, hs] = o.astype(BF16)
            egl = jnp.exp(glb[b][:, SUBLANES + h:SUBLANES + h + 1])
            s_ref[b, h] = d["s"] * egl + _dot_tn(kd_ref[b, rows, hs], d["vn"])
        return carry

    lax.fori_loop(0, nchunk, scan, 0)


def _gdn_scan(q, k, v, ab, cf, reverse):
    b, t, c = q.shape
    nt = t // TM
    dr = 1 if reverse else 0
    nchunk = TM // GDN_CHUNK

    def tile(s):
        lat = (nt - 1 - s) if reverse else (s - 1)
        return jnp.where(s == 0, nt - 1, lat)

    spec = pl.BlockSpec((b, TM, c), lambda s: (0, tile(s), 0))
    return pl.pallas_call(
        functools.partial(_gdn_scan_kernel, reverse, b),
        grid=(nt,),
        in_specs=[spec, spec, spec,
                  pl.BlockSpec((b, None, TM, LANES), lambda s: (0, dr, tile(s), 0)),
                  pl.BlockSpec((None, SUBLANES, LANES), lambda s: (dr, 0, 0))],
        out_specs=spec,
        out_shape=jax.ShapeDtypeStruct((b, t, c), BF16),
        scratch_shapes=[pltpu.VMEM((b, GDN_HEADS, GDN_DK, GDN_DV), F32),
                        pltpu.VMEM((b, TM, LANES), F32),
                        pltpu.VMEM((b, nchunk, LANES, LANES), F32),
                        pltpu.VMEM((b, TM, LANES), F32),
                        pltpu.VMEM((b, TM, c), F32),
                        pltpu.VMEM((b, nchunk, GDN_HEADS, 2 * GDN_CHUNK, GDN_DK), BF16),
                        pltpu.VMEM((b, TM, c), BF16),
                        pltpu.VMEM((b, nchunk, GDN_HEADS, GDN_CHUNK, GDN_CHUNK), BF16)],
        compiler_params=_cparams(("arbitrary",)),
        name="gdn_scan_bwd" if reverse else "gdn_scan_fwd",
    )(q, k, v, ab, cf)


def _gdn_out_kernel(of_ref, ob_ref, gate_ref, h_ref, mod_ref, on_ref, w_ref, g_ref, out_ref, y_ref):
    for hh in range(GDN_HEADS):
        hs = slice(hh * GDN_DV, (hh + 1) * GDN_DV)
        o = of_ref[:, hs].astype(F32) + ob_ref[:, hs].astype(F32)
        gt = gate_ref[:, hs].astype(F32)
        y_ref[:, hs] = (_rms(o) * on_ref[...] * _silu(gt)).astype(BF16)
    out = _dot(y_ref[...], w_ref[...])
    out_ref[...] = _residual(h_ref[...], out, g_ref[...], mod_ref[2:3, :])


def _gdn_out(o_f, o_b, gate, h, mod, o_norm, w_out, g):
    b, t, d = h.shape
    nt = t // TM
    c = o_f.shape[-1]
    return pl.pallas_call(
        _gdn_out_kernel,
        grid=(b, nt),
        in_specs=[_tok_spec(c), _tok_spec(c), _tok_spec(c), _tok_spec(d), _mod_spec(nt, d),
                  _full_spec((1, GDN_DV)), _full_spec(w_out.shape), _full_spec((1, d))],
        out_specs=_tok_spec(d),
        out_shape=jax.ShapeDtypeStruct((b, t, d), F32),
        scratch_shapes=[pltpu.VMEM((TM, c), BF16)],
        compiler_params=_cparams(("parallel", "parallel")),
        name="gdn_out",
    )(o_f, o_b, gate, h, mod, o_norm, w_out, g)


def _conf_in_kernel(h_ref, mod_ref, g_ref, w_ref, b_ref, u_ref):
    a16 = _norm_mod(h_ref[...], g_ref[...], mod_ref[0:1, :], mod_ref[1:2, :]).astype(BF16)
    d = u_ref.shape[-1]
    cw = 512
    for n in range(d // cw):
        lin = _dot(a16, w_ref[:, n * cw:(n + 1) * cw]) + b_ref[:, n * cw:(n + 1) * cw]
        gt = _dot(a16, w_ref[:, d + n * cw:d + (n + 1) * cw]) + b_ref[:, d + n * cw:d + (n + 1) * cw]
        u_ref[:, n * cw:(n + 1) * cw] = (lin * jax.nn.sigmoid(gt)).astype(BF16)


def _conf_in(h, mod, g, w_in, b_in):
    b, t, d = h.shape
    nt = t // TM
    return pl.pallas_call(
        _conf_in_kernel,
        grid=(b, nt),
        in_specs=[_tok_spec(d), _mod_spec(nt, d), _full_spec((1, d)),
                  _full_spec(w_in.shape), _full_spec(b_in.shape)],
        out_specs=_tok_spec(d),
        out_shape=jax.ShapeDtypeStruct((b, t, d), BF16),
        compiler_params=_cparams(("parallel", "parallel")),
        name="conf_in",
    )(h, mod, g, w_in, b_in)


def _conf_conv_kernel(nt, cur_ref, prev_ref, next_ref, dw_ref, dwb_ref, lng_ref, lnb_ref,
                      w_ref, b_ref, h_ref, mod_ref, g_ref, out_ref, ext_ref, sh_ref, y_ref):
    prev_ok, next_ok = _halo_valid(nt)
    pad = CONF_CONV // 2
    ext_ref[0:HALO, :] = jnp.where(prev_ok, prev_ref[...].astype(F32), 0.0)
    ext_ref[HALO:HALO + TM, :] = cur_ref[...].astype(F32)
    ext_ref[HALO + TM:2 * HALO + TM, :] = jnp.where(next_ok, next_ref[...].astype(F32), 0.0)
    for r in range(1, SUBLANES):
        sh_ref[r - 1] = ext_ref[pl.ds(r, sh_ref.shape[1]), :]
    rw, cw = 64, 256
    d = cur_ref.shape[-1]
    for cc in range(d // cw):
        cs = slice(cc * cw, (cc + 1) * cw)
        for rr in range(TM // rw):
            acc = None
            for tap in range(CONF_CONV):
                off = HALO - pad + tap
                r = off % SUBLANES
                src = ext_ref if r == 0 else sh_ref.at[r - 1]
                term = dw_ref[tap:tap + 1, cs] * src[pl.ds(off - r + rr * rw, rw), cs]
                acc = term if acc is None else acc + term
            y_ref[rr * rw:(rr + 1) * rw, cs] = acc + dwb_ref[:, cs]
    y = y_ref[...]
    yc = y - jnp.mean(y, axis=-1, keepdims=True)
    yn = yc * lax.rsqrt(jnp.mean(yc * yc, axis=-1, keepdims=True) + LN_EPS) * lng_ref[...] + lnb_ref[...]
    out = _dot(_silu(yn).astype(BF16), w_ref[...]) + b_ref[...]
    out_ref[...] = _residual(h_ref[...], out, g_ref[...], mod_ref[2:3, :])


def _conf_conv(u, dw, dw_b, ln_g, ln_b, w_out, b_out, h, mod, g):
    b, t, d = h.shape
    nt = t // TM
    prev, nxt = _halo_specs(nt, d)
    vec = _full_spec((1, d))
    return pl.pallas_call(
        functools.partial(_conf_conv_kernel, nt),
        grid=(b, nt),
        in_specs=[_tok_spec(d), prev, nxt, _full_spec(dw.shape), vec, vec, vec,
                  _full_spec(w_out.shape), vec, _tok_spec(d), _mod_spec(nt, d), vec],
        out_specs=_tok_spec(d),
        out_shape=jax.ShapeDtypeStruct((b, t, d), F32),
        scratch_shapes=[pltpu.VMEM((TM + 2 * HALO, d), F32),
                        pltpu.VMEM((SUBLANES - 1, TM + 2 * HALO - SUBLANES, d), F32),
                        pltpu.VMEM((TM, d), F32)],
        compiler_params=_cparams(("parallel", "parallel")),
        name="conf_conv",
    )(u, u, u, dw, dw_b, ln_g, ln_b, w_out, b_out, h, mod, g)


def _mla_proj_kernel(h_ref, mod_ref, g_ref, win_ref, qn_ref, kvn_ref, wuq_ref, wukv_ref, cs_ref,
                     q_ref, k_ref, v_ref):
    a16 = _norm_mod(h_ref[...], g_ref[...], mod_ref[0:1, :], mod_ref[1:2, :]).astype(BF16)
    p = _dot(a16, win_ref[...])
    cq = _rms(p[:, :MLA_Q_RANK]) * qn_ref[...]
    ckv = _rms(p[:, MLA_Q_RANK:MLA_Q_RANK + MLA_KV_RANK]) * kvn_ref[...]
    kr = p[:, MLA_Q_RANK + MLA_KV_RANK:]
    cs = cs_ref[...]
    kx = kr * cs
    krot = (kx + pltpu.roll(kx, MLA_ROPE, 1)).astype(BF16)
    scale = (MLA_NOPE + MLA_ROPE) ** -0.5 * LOG2E
    cq16 = cq.astype(BF16)
    ckv16 = ckv.astype(BF16)
    hw = MLA_NOPE + 2 * MLA_ROPE
    for hh in range(MLA_HEADS):
        qh = _dot(cq16, wuq_ref[:, hh * hw:(hh + 1) * hw]) * scale
        q_ref[:, hh * hw:hh * hw + MLA_NOPE] = qh[:, :MLA_NOPE].astype(BF16)
        q_ref[:, hh * hw + MLA_NOPE:(hh + 1) * hw] = (qh[:, MLA_NOPE:] * cs).astype(BF16)
        k_ref[:, hh * hw:hh * hw + MLA_NOPE] = _dot(
            ckv16, wukv_ref[:, hh * MLA_NOPE:(hh + 1) * MLA_NOPE]).astype(BF16)
        k_ref[:, hh * hw + MLA_NOPE:(hh + 1) * hw] = krot
        nk = MLA_HEADS * MLA_NOPE
        v_ref[:, 2 * hh * MLA_V:(2 * hh + 1) * MLA_V] = _dot(
            ckv16, wukv_ref[:, nk + hh * MLA_V:nk + (hh + 1) * MLA_V]).astype(BF16)
        v_ref[:, (2 * hh + 1) * MLA_V:(2 * hh + 2) * MLA_V] = jnp.ones((TM, MLA_V), BF16)


def _mla_proj(h, mod, g, w_in, q_norm, kv_norm, w_uq, w_ukv, cs_tab):
    b, t, d = h.shape
    nt = t // TM
    hw = MLA_NOPE + 2 * MLA_ROPE
    return pl.pallas_call(
        _mla_proj_kernel,
        grid=(b, nt),
        in_specs=[_tok_spec(d), _mod_spec(nt, d), _full_spec((1, d)), _full_spec(w_in.shape),
                  _full_spec(q_norm.shape), _full_spec(kv_norm.shape),
                  _full_spec(w_uq.shape), _full_spec(w_ukv.shape),
                  pl.BlockSpec((TM, LANES), lambda bb, j: (j, 0))],
        out_specs=[_tok_spec(MLA_HEADS * hw), _tok_spec(MLA_HEADS * hw), _tok_spec(MLA_HEADS * 2 * MLA_V)],
        out_shape=[jax.ShapeDtypeStruct((b, t, MLA_HEADS * hw), BF16),
                   jax.ShapeDtypeStruct((b, t, MLA_HEADS * hw), BF16),
                   jax.ShapeDtypeStruct((b, t, MLA_HEADS * 2 * MLA_V), BF16)],
        compiler_params=_cparams(("parallel", "parallel")),
        name="mla_proj",
    )(h, mod, g, w_in, q_norm, kv_norm, w_uq, w_ukv, cs_tab)


def _attn_kernel(tk, q_ref, k_ref, v_ref, o_ref, m_ref, acc_ref):
    q = q_ref[...]
    for i in range(k_ref.shape[0] // tk):
        rows = slice(i * tk, (i + 1) * tk)
        s = _dot_nt(q, k_ref[rows, :])
        m_blk = jnp.max(s, axis=-1, keepdims=True)
        if i == 0:
            m_new = m_blk
            acc_ref[...] = _dot(jnp.exp2(s - m_new).astype(BF16), v_ref[rows, :])
        else:
            m_old = m_ref[...]
            m_new = jnp.maximum(m_old, m_blk)
            acc_ref[...] = jnp.exp2(m_old - m_new) * acc_ref[...] + _dot(
                jnp.exp2(s - m_new).astype(BF16), v_ref[rows, :])
        m_ref[...] = m_new
    acc = acc_ref[...]
    o_ref[...] = (acc[:, :MLA_V] / acc[:, MLA_V:]).astype(BF16)


def _attention(q, k, v, tq, q_rows, q_row0, k_rows, k_row0, name):
    b = q.shape[0]
    hw = MLA_NOPE + 2 * MLA_ROPE
    tk = next(c for c in (768, 512, 256) if k_rows % c == 0)
    q0 = q_row0 // tq
    k0 = k_row0 // k_rows
    return pl.pallas_call(
        functools.partial(_attn_kernel, tk),
        grid=(b, MLA_HEADS, q_rows // tq),
        in_specs=[pl.BlockSpec((None, tq, hw), lambda bb, hh, i: (bb, q0 + i, hh)),
                  pl.BlockSpec((None, k_rows, hw), lambda bb, hh, i: (bb, k0, hh)),
                  pl.BlockSpec((None, k_rows, 2 * MLA_V), lambda bb, hh, i: (bb, k0, hh))],
        out_specs=pl.BlockSpec((None, tq, MLA_V), lambda bb, hh, i: (bb, i, hh)),
        out_shape=jax.ShapeDtypeStruct((b, q_rows, MLA_HEADS * MLA_V), BF16),
        scratch_shapes=[pltpu.VMEM((tq, 1), F32), pltpu.VMEM((tq, 2 * MLA_V), F32)],
        compiler_params=_cparams(("parallel", "parallel", "arbitrary")),
        name=name,
    )(q, k, v)


def _proj_out_kernel(o_ref, w_ref, h_ref, mod_ref, g_ref, out_ref):
    out = _dot(o_ref[...], w_ref[...])
    out_ref[...] = _residual(h_ref[...], out, g_ref[...], mod_ref[2:3, :])


def _proj_out(o, w, h, mod, g):
    b, t, d = h.shape
    nt = t // TM
    return pl.pallas_call(
        _proj_out_kernel,
        grid=(b, nt),
        in_specs=[_tok_spec(o.shape[-1]), _full_spec(w.shape), _tok_spec(d), _mod_spec(nt, d),
                  _full_spec((1, d))],
        out_specs=_tok_spec(d),
        out_shape=jax.ShapeDtypeStruct((b, t, d), F32),
        compiler_params=_cparams(("parallel", "parallel")),
        name="mla_out",
    )(o, w, h, mod, g)


def _ffn_kernel(fc, h_ref, mod_ref, g2_ref, wg_ref, wu_ref, wd_ref, g3_ref, out_ref, acc_ref):
    h = h_ref[...]
    a16 = _norm_mod(h, g2_ref[...], mod_ref[3:4, :], mod_ref[4:5, :]).astype(BF16)
    for c in range(wg_ref.shape[-1] // fc):
        fs = slice(c * fc, (c + 1) * fc)
        act = (_silu(_dot(a16, wg_ref[:, fs])) * _dot(a16, wu_ref[:, fs])).astype(BF16)
        part = _dot(act, wd_ref[fs, :])
        if c == 0:
            acc_ref[...] = part
        else:
            acc_ref[...] += part
    out_ref[...] = _residual(h, acc_ref[...], g3_ref[...], mod_ref[5:6, :])


def _ffn(h, mod, g2, w_gate, w_up, w_down, g3, n_tiles):
    b, t, d = h.shape
    nt = t // TM
    return pl.pallas_call(
        functools.partial(_ffn_kernel, 256),
        grid=(b, n_tiles),
        in_specs=[_tok_spec(d), _mod_spec(nt, d), _full_spec((1, d)), _full_spec(w_gate.shape),
                  _full_spec(w_up.shape), _full_spec(w_down.shape), _full_spec((1, d))],
        out_specs=_tok_spec(d),
        out_shape=jax.ShapeDtypeStruct((b, n_tiles * TM, d), F32),
        scratch_shapes=[pltpu.VMEM((TM, d), F32)],
        compiler_params=_cparams(("parallel", "parallel")),
        name="ffn",
    )(h, mod, g2, w_gate, w_up, w_down, g3)


def _moe_route_kernel(h_ref, mod_ref, g2_ref, wr_ref, a_ref, rt_ref):
    lane = lax.broadcasted_iota(jnp.int32, (TM, LANES), 1).astype(F32)
    a = _norm_mod(h_ref[...], g2_ref[...], mod_ref[3:4, :], mod_ref[4:5, :])
    for s in range(SUBLANES):
        a_ref[pl.ds(s, TM, stride=SUBLANES), :] = a[:, s * LANES:(s + 1) * LANES]
    logits = jnp.where(lane < N_EXPERTS, _dot(a, wr_ref[...], precision=HIGHEST), -jnp.inf)
    m1 = jnp.max(logits, axis=-1, keepdims=True)
    i1 = jnp.min(jnp.where(logits == m1, lane, float(LANES)), axis=-1, keepdims=True)
    rest = jnp.where(lane == i1, -jnp.inf, logits)
    m2 = jnp.max(rest, axis=-1, keepdims=True)
    i2 = jnp.min(jnp.where(rest == m2, lane, float(LANES)), axis=-1, keepdims=True)
    e2 = jnp.exp(m2 - m1)
    w1 = 1.0 / (1.0 + e2)
    rt_ref[...] = (jnp.where(lane == 0.0, i1, 0.0) + jnp.where(lane == 1.0, i2, 0.0)
                   + jnp.where(lane == 2.0, w1, 0.0) + jnp.where(lane == 3.0, e2 * w1, 0.0))


def _moe_route(h, mod, g2, w_router, n_tiles):
    b, t, d = h.shape
    nt = t // TM
    return pl.pallas_call(
        _moe_route_kernel,
        grid=(b, n_tiles),
        in_specs=[_tok_spec(d), _mod_spec(nt, d), _full_spec((1, d)), _full_spec(w_router.shape)],
        out_specs=[pl.BlockSpec((TM * SUBLANES, LANES), lambda bb, j: (bb * n_tiles + j, 0)),
                   _tok_spec(LANES)],
        out_shape=[jax.ShapeDtypeStruct((b * n_tiles * TM * SUBLANES, LANES), F32),
                   jax.ShapeDtypeStruct((b, n_tiles * TM, LANES), F32)],
        compiler_params=_cparams(("parallel", "parallel")),
        name="moe_route",
    )(h, mod, g2, w_router)


def _gather_tiles(idx_ref, base, n, src_hbm, dst, sem):
    def body(r, carry):
        row = pl.multiple_of(idx_ref[base + r], SUBLANES)
        pltpu.make_async_copy(src_hbm.at[pl.ds(row, SUBLANES), :],
                              dst.at[pl.ds(pl.multiple_of(r * SUBLANES, SUBLANES), SUBLANES), :], sem).start()
        return carry
    lax.fori_loop(0, n, body, 0, unroll=8)


def _wait_tiles(n, src_hbm, dst, sem):
    pltpu.make_async_copy(src_hbm.at[pl.ds(0, n * SUBLANES), :], dst, sem).wait()


def _untile(buf, n):
    return jnp.concatenate([buf[pl.ds(s, n, stride=SUBLANES), :] for s in range(SUBLANES)], axis=1)


def _moe_dispatch_kernel(dst_ref, last_ref, nact_ref, a_ref, x_hbm, stage_ref, sem, zsem):
    i = pl.program_id(0)
    n = pl.num_programs(0)
    rows = TM * SUBLANES
    slot = i % 2
    n_tiles_max = x_hbm.shape[0] // rows

    @pl.when(i == 0)
    def _():
        zero = stage_ref.at[1]
        zero[...] = jnp.zeros_like(zero)

        def zero_tile(t):
            return pltpu.make_async_copy(zero, x_hbm.at[pl.ds(pl.multiple_of(t * rows, rows), rows), :], zsem)

        for e in range(N_EXPERTS):
            zero_tile(last_ref[e]).start()
            pl.when(nact_ref[0] + e < n_tiles_max)(lambda e=e: zero_tile(nact_ref[0] + e).start())
        for e in range(N_EXPERTS):
            zero_tile(0).wait()
            pl.when(nact_ref[0] + e < n_tiles_max)(lambda: zero_tile(0).wait())

    stage = stage_ref.at[slot]
    stage[...] = a_ref[...]

    def body(r, carry):
        src = pl.multiple_of(r * SUBLANES, SUBLANES)
        for k in range(2):
            dst = pl.multiple_of(dst_ref[(i * TM + r) * 2 + k], SUBLANES)
            pltpu.make_async_copy(stage.at[pl.ds(src, SUBLANES), :], x_hbm.at[pl.ds(dst, SUBLANES), :],
                                  sem.at[slot]).start()
        return carry
    lax.fori_loop(0, TM, body, 0, unroll=4)

    def wait(sl):
        for k in range(2):
            pltpu.make_async_copy(stage_ref.at[sl], x_hbm.at[pl.ds(0, rows), :], sem.at[sl]).wait()

    @pl.when(i > 0)
    def _():
        wait(1 - slot)

    @pl.when(i == n - 1)
    def _():
        wait(slot)


def _moe_dispatch(dst_rows, last_tile, n_active, a2d, n_rows):
    n_steps = a2d.shape[0] // (TM * SUBLANES)
    return pl.pallas_call(
        _moe_dispatch_kernel,
        grid_spec=pltpu.PrefetchScalarGridSpec(
            num_scalar_prefetch=3,
            grid=(n_steps,),
            in_specs=[pl.BlockSpec((TM * SUBLANES, LANES), lambda i, ds, lt, na: (i, 0))],
            out_specs=pl.BlockSpec(memory_space=pl.ANY),
            scratch_shapes=[pltpu.VMEM((2, TM * SUBLANES, LANES), F32), pltpu.SemaphoreType.DMA((2,)),
                            pltpu.SemaphoreType.DMA(())]),
        out_shape=jax.ShapeDtypeStruct((n_rows * SUBLANES, LANES), F32),
        compiler_params=_cparams(("arbitrary",)),
        name="moe_dispatch",
    )(dst_rows, last_tile, n_active, a2d)


def _moe_experts_kernel(te_ref, nact_ref, x_ref, wg_ref, wu_ref, wd_ref, y_ref):
    i = pl.program_id(0)

    @pl.when(i < nact_ref[0])
    def _():
        x16 = _untile(x_ref, TM).astype(BF16)
        f = wg_ref.shape[-1]
        acc = None
        for c0 in range(0, f, 512):
            fs = slice(c0, min(c0 + 512, f))
            act = (_silu(_dot(x16, wg_ref[:, fs])) * _dot(x16, wu_ref[:, fs])).astype(BF16)
            part = _dot(act, wd_ref[fs, :])
            acc = part if acc is None else acc + part
        for s in range(SUBLANES):
            y_ref[pl.ds(s, TM, stride=SUBLANES), :] = acc[:, s * LANES:(s + 1) * LANES]

    @pl.when(i >= nact_ref[0])
    def _():
        y_ref[...] = jnp.zeros_like(y_ref)


def _moe_experts(tile_expert, n_active, x2d, w_gate, w_up, w_down):
    n_tiles = tile_expert.shape[0]
    _, d, f = w_gate.shape
    wspec_in = pl.BlockSpec((None, d, f), lambda i, te, na: (te[i], 0, 0))
    wspec_out = pl.BlockSpec((None, f, d), lambda i, te, na: (te[i], 0, 0))
    return pl.pallas_call(
        _moe_experts_kernel,
        grid_spec=pltpu.PrefetchScalarGridSpec(
            num_scalar_prefetch=2,
            grid=(n_tiles,),
            in_specs=[pl.BlockSpec((TM * SUBLANES, LANES), lambda i, te, na: (jnp.minimum(i, na[0] - 1), 0)),
                      wspec_in, wspec_in, wspec_out],
            out_specs=pl.BlockSpec((TM * SUBLANES, LANES), lambda i, te, na: (i, 0))),
        out_shape=jax.ShapeDtypeStruct((n_tiles * TM * SUBLANES, LANES), F32),
        compiler_params=_cparams(("arbitrary",)),
        name="moe_experts",
    )(tile_expert, n_active, x2d, w_gate, w_up, w_down)


def _moe_combine_kernel(n_tiles, dst_ref, y_hbm, rt_ref, h_ref, mod_ref, g3_ref, out_ref, ybuf, sem):
    step = pl.program_id(0) * n_tiles + pl.program_id(1)
    nsteps = pl.num_programs(0) * n_tiles

    @pl.when(step == 0)
    def _():
        _gather_tiles(dst_ref, 0, 2 * TM, y_hbm, ybuf.at[0], sem.at[0])

    @pl.when(step + 1 < nsteps)
    def _():
        nxt = (step + 1) % 2
        _gather_tiles(dst_ref, (step + 1) * 2 * TM, 2 * TM, y_hbm, ybuf.at[nxt], sem.at[nxt])

    cur = step % 2
    _wait_tiles(2 * TM, y_hbm, ybuf.at[cur], sem.at[cur])
    yb = ybuf.at[cur]
    w1 = rt_ref[:, 2:3]
    w2 = rt_ref[:, 3:4]
    f = jnp.concatenate(
        [w1 * yb[pl.ds(s, TM, stride=2 * SUBLANES), :] + w2 * yb[pl.ds(SUBLANES + s, TM, stride=2 * SUBLANES), :]
         for s in range(SUBLANES)], axis=1)
    out_ref[...] = _residual(h_ref[...], f, g3_ref[...], mod_ref[5:6, :])


def _moe_combine(dst_rows, y2d, rt, h, mod, g3, n_tiles):
    b, t, d = h.shape
    nt = t // TM
    tok = lambda c: pl.BlockSpec((None, TM, c), lambda bb, j, ds: (bb, j, 0))
    return pl.pallas_call(
        functools.partial(_moe_combine_kernel, n_tiles),
        grid_spec=pltpu.PrefetchScalarGridSpec(
            num_scalar_prefetch=1,
            grid=(b, n_tiles),
            in_specs=[pl.BlockSpec(memory_space=pl.ANY), tok(LANES), tok(d),
                      pl.BlockSpec((None, SUBLANES, d),
                                   lambda bb, j, ds: (jnp.where(j == nt - 1, 0, bb + 1), 0, 0)),
                      pl.BlockSpec((1, d), lambda bb, j, ds: (0, 0))],
            out_specs=tok(d),
            scratch_shapes=[pltpu.VMEM((2, 2 * TM * SUBLANES, LANES), F32), pltpu.SemaphoreType.DMA((2,))]),
        out_shape=jax.ShapeDtypeStruct((b, n_tiles * TM, d), F32),
        compiler_params=_cparams(("arbitrary", "arbitrary")),
        name="moe_combine",
    )(dst_rows, y2d, rt, h, mod, g3)


def _moe(h, mod, g2, w_router, w_gate, w_up, w_down, g3, n_tiles):
    b = h.shape[0]
    n_tok = b * n_tiles * TM
    a2d, rt = _moe_route(h, mod, g2, w_router, n_tiles)
    expert = rt.reshape(n_tok, LANES)[:, :2].astype(jnp.int32).reshape(-1)
    onehot = (expert[:, None] == jnp.arange(N_EXPERTS)[None, :]).astype(jnp.int32)
    rank = jnp.sum((jnp.cumsum(onehot, axis=0) - onehot) * onehot, axis=1)
    count = jnp.sum(onehot, axis=0)
    padded = (count + TM - 1) // TM * TM
    ends = jnp.cumsum(padded)
    dest = (ends - padded)[expert] + rank
    n_tiles_max = 2 * n_tok // TM + N_EXPERTS
    tile_start = jnp.arange(n_tiles_max, dtype=jnp.int32) * TM
    tile_expert = jnp.minimum(jnp.sum(ends[None, :] <= tile_start[:, None], axis=1), N_EXPERTS - 1)
    n_active = (ends[-1] // TM).reshape(1).astype(jnp.int32)
    last_tile = jnp.maximum(ends // TM - 1, 0).astype(jnp.int32)
    dst_rows = (dest * SUBLANES).astype(jnp.int32)
    x2d = _moe_dispatch(dst_rows, last_tile, n_active, a2d, n_tiles_max * TM)
    y2d = _moe_experts(tile_expert.astype(jnp.int32), n_active, x2d, w_gate, w_up, w_down)
    return _moe_combine(dst_rows, y2d, rt, h, mod, g3, n_tiles)


def _pad_rows(a, rows):
    return jnp.pad(a, ((0, rows - a.shape[0]),) + ((0, 0),) * (a.ndim - 1))


def _rope_table(seq, ctx_len):
    n_freq = MLA_ROPE // 4
    t = jnp.arange(seq)
    pos = jnp.stack([(t // GRID_W).astype(F32), (t % GRID_W).astype(F32)], axis=1)
    inv = ROPE_THETA ** (-jnp.arange(n_freq, dtype=F32) / n_freq)
    ang = pos[:, :, None] * inv
    cos = jnp.repeat(jnp.cos(ang)[:, :, None, :], 2, axis=2).reshape(seq, MLA_ROPE)
    sin = jnp.stack([-jnp.sin(ang), jnp.sin(ang)], axis=2).reshape(seq, MLA_ROPE)
    lat = jnp.concatenate([cos, sin], axis=1)
    ctx = jnp.concatenate([jnp.ones((ctx_len, MLA_ROPE), F32), jnp.zeros((ctx_len, MLA_ROPE), F32)], axis=1)
    return jnp.concatenate([lat, ctx], axis=0)


def _swap_halves(w):
    shp = w.shape
    return jnp.flip(w.reshape(shp[:-1] + (2, 2, MLA_ROPE // 4)), axis=-2).reshape(shp)


def kernel(x, c, ctx, c_ctx, w_mod, b_mod, norm_g, gdn_w_in, gdn_conv, gdn_a_log, gdn_dt_bias, gdn_o_norm, gdn_w_out, conf_w_in, conf_b_in, conf_dw, conf_dw_b, conf_ln_g, conf_ln_b, conf_w_out, conf_b_out, mla_w_in, mla_q_norm, mla_kv_norm, mla_w_uq, mla_w_ukv, mla_w_o, ffn_w_gate, ffn_w_up, ffn_w_down, moe_router, moe_w_gate, moe_w_up, moe_w_down):
    bsz, seq, d = x.shape
    ctx_len = ctx.shape[1]
    depth = w_mod.shape[0]
    assert ctx_len == TM and seq % (4 * TM) == 0 and bsz + 1 <= SUBLANES
    nt = (seq + ctx_len) // TM
    n_lat = seq // TM

    cvec = _pad_rows(jnp.concatenate([c_ctx[None, :], c], axis=0), SUBLANES)
    mods = _adaln(cvec, w_mod, b_mod)
    mods = mods.reshape(depth, SUBLANES, N_MOD, d)[:, :bsz + 1]
    mods = jnp.pad(mods, ((0, 0), (0, 0), (0, SUBLANES - N_MOD), (0, 0)))

    h = jnp.concatenate([x, ctx], axis=1)
    rope_cs = _rope_table(seq, ctx_len)

    n_mix = 3
    for i in range(depth):
        last = i == depth - 1
        mod = mods[i]
        g = norm_g[i].reshape(4, 1, d)
        kind, j = i % n_mix, i // n_mix
        if kind == 0:
            nqk = GDN_HEADS * GDN_DK
            nmain = 2 * nqk + 2 * GDN_HEADS * GDN_DV
            w_in = gdn_w_in[j]
            w_main = w_in[:, :nmain].astype(BF16)
            w_tail = w_in[:, nmain:]
            nh = GDN_HEADS
            w_ab = jnp.stack([
                jnp.pad(jnp.concatenate([w_tail[:, dr * nh:(dr + 1) * nh],
                                         w_tail[:, (2 + dr) * nh:(3 + dr) * nh]], axis=1),
                        ((0, 0), (0, LANES - 2 * nh))) for dr in range(2)])
            lane_pad = ((0, 0), (nh, LANES - 2 * nh))
            cf = jnp.stack([jnp.pad(jnp.exp(gdn_a_log[j]), lane_pad),
                            jnp.pad(gdn_dt_bias[j], lane_pad)], axis=1)
            cf = jnp.pad(cf, ((0, 0), (0, SUBLANES - 2), (0, 0)))
            qkv, gate, ab = _gdn_proj(h, mod, g[0], w_main, w_ab)
            q, k, v = _gdn_conv(qkv, _pad_rows(gdn_conv[j], SUBLANES))
            o_f = _gdn_scan(q, k, v, ab, cf, reverse=False)
            o_b = _gdn_scan(q, k, v, ab, cf, reverse=True)
            h = _gdn_out(o_f, o_b, gate, h, mod, gdn_o_norm[j].reshape(1, -1),
                         gdn_w_out[j].astype(BF16), g[1])
        elif kind == 1:
            u = _conf_in(h, mod, g[0], conf_w_in[j].astype(BF16), conf_b_in[j].reshape(1, -1))
            h = _conf_conv(u, _pad_rows(conf_dw[j], 32), conf_dw_b[j].reshape(1, -1),
                           conf_ln_g[j].reshape(1, -1), conf_ln_b[j].reshape(1, -1),
                           conf_w_out[j].astype(BF16), conf_b_out[j].reshape(1, -1), h, mod, g[1])
        else:
            w_in = mla_w_in[j]
            w_kr = w_in[:, MLA_Q_RANK + MLA_KV_RANK:]
            w_in2 = jnp.concatenate([w_in, _swap_halves(w_kr)], axis=1).astype(BF16)
            hq = MLA_NOPE + MLA_ROPE
            w_uq = mla_w_uq[j].reshape(MLA_Q_RANK, MLA_HEADS, hq)
            w_uq2 = jnp.concatenate([w_uq, _swap_halves(w_uq[..., MLA_NOPE:])], axis=-1)
            w_uq2 = w_uq2.reshape(MLA_Q_RANK, -1).astype(BF16)
            w_ukv = mla_w_ukv[j].reshape(MLA_KV_RANK, MLA_HEADS, MLA_NOPE + MLA_V)
            w_ukv2 = jnp.concatenate([w_ukv[..., :MLA_NOPE].reshape(MLA_KV_RANK, -1),
                                      w_ukv[..., MLA_NOPE:].reshape(MLA_KV_RANK, -1)], axis=1).astype(BF16)
            qa, ka, va = _mla_proj(h, mod, g[0], w_in2, mla_q_norm[j].reshape(1, -1),
                                   mla_kv_norm[j].reshape(1, -1), w_uq2, w_ukv2, rope_cs)
            o_l = _attention(qa, ka, va, 4 * TM, seq, 0, seq + ctx_len, 0, "attn_latent")
            o_c = _attention(qa, ka, va, TM, ctx_len, seq, ctx_len, seq, "attn_context")
            o = jnp.concatenate([o_l, o_c], axis=1)
            h = _proj_out(o, mla_w_o[j].astype(BF16), h, mod, g[1])

        n_tiles = n_lat if last else nt
        f = i // 2
        if i % 2 == 0:
            h = _ffn(h, mod, g[2], ffn_w_gate[f].astype(BF16), ffn_w_up[f].astype(BF16),
                     ffn_w_down[f].astype(BF16), g[3], n_tiles)
        else:
            w_r = jnp.pad(moe_router[f], ((0, 0), (0, LANES - N_EXPERTS)))
            h = _moe(h, mod, g[2], w_r, moe_w_gate[f].astype(BF16), moe_w_up[f].astype(BF16),
                     moe_w_down[f].astype(BF16), g[3], n_tiles)
    return h
```

```python
import functools

import jax
import jax.numpy as jnp
from jax import lax
from jax.experimental import pallas as pl
from jax.experimental.pallas import tpu as pltpu

F32 = jnp.float32
BF16 = jnp.bfloat16

N_MOD = 6
RMS_EPS = 1e-6
LN_EPS = 1e-5
GRID_W = 64
GDN_HEADS = 8
GDN_DK = 128
GDN_DV = 128
GDN_CONV = 5
GDN_CHUNK = 64
CONF_CONV = 31
MLA_HEADS = 8
MLA_NOPE = 128
MLA_ROPE = 64
MLA_V = 128
MLA_Q_RANK = 512
MLA_KV_RANK = 256
ROPE_THETA = 10000.0
LOG2E = 1.4426950408889634
N_EXPERTS = 8

LANES = 128
SUBLANES = 8
TM = 256
HALO = 16
PREP_CHUNKS = 2
VMEM_LIMIT = 56 * 1024 * 1024


def _cparams(sem):
    return pltpu.CompilerParams(dimension_semantics=sem, vmem_limit_bytes=VMEM_LIMIT)


def _dot(a, b, **kw):
    return jnp.dot(a, b, preferred_element_type=F32, **kw)


def _dot_nt(a, b):
    return lax.dot_general(a, b, (((1,), (1,)), ((), ())), preferred_element_type=F32)


def _dot_tn(a, b):
    return lax.dot_general(a, b, (((0,), (0,)), ((), ())), preferred_element_type=F32)


def _silu(x):
    return x * jax.nn.sigmoid(x)


def _rms(x, eps=RMS_EPS):
    return x * lax.rsqrt(jnp.mean(x * x, axis=-1, keepdims=True) + eps)


def _norm_mod(h, g, shift, scale):
    return _rms(h) * g * (1.0 + scale) + shift


def _residual(h, o, g, gate):
    return h + gate * (_rms(o) * g)


def _tok_spec(c):
    return pl.BlockSpec((None, TM, c), lambda b, j: (b, j, 0))


def _full_spec(shape):
    nd = len(shape)
    return pl.BlockSpec(shape, lambda *_: (0,) * nd)


def _mod_spec(nt, d):
    return pl.BlockSpec((None, SUBLANES, d), lambda b, j: (jnp.where(j == nt - 1, 0, b + 1), 0, 0))


def _halo_specs(nt, c):
    per = TM // HALO
    prev = pl.BlockSpec((None, HALO, c), lambda b, j: (b, jnp.maximum(j * per - 1, 0), 0))
    nxt = pl.BlockSpec((None, HALO, c), lambda b, j: (b, jnp.minimum((j + 1) * per, nt * per - 1), 0))
    return prev, nxt


def _halo_valid(nt):
    j = pl.program_id(1)
    prev_ok = jnp.logical_and(j != 0, j != nt - 1)
    next_ok = j < nt - 2
    return prev_ok, next_ok


def _adaln_kernel(c_ref, w_ref, b_ref, o_ref):
    o_ref[...] = _dot_hi(_silu(c_ref[...]), w_ref[...]) + b_ref[...]


def _adaln(cvec, w_mod, b_mod):
    depth, d, n = w_mod.shape
    tn = 1536
    return pl.pallas_call(
        _adaln_kernel,
        grid=(depth, n // tn),
        in_specs=[pl.BlockSpec((SUBLANES, d), lambda i, k: (0, 0)),
                  pl.BlockSpec((None, d, tn), lambda i, k: (i, 0, k)),
                  pl.BlockSpec((None, 1, tn), lambda i, k: (i, 0, k))],
        out_specs=pl.BlockSpec((None, SUBLANES, tn), lambda i, k: (i, 0, k)),
        out_shape=jax.ShapeDtypeStruct((depth, SUBLANES, n), F32),
        compiler_params=_cparams(("parallel", "parallel")),
        name="adaln",
    )(cvec, w_mod, b_mod.reshape(depth, 1, n))


def _gdn_proj_kernel(h_ref, mod_ref, g_ref, w_ref, wab_ref, qkv_ref, gate_ref, ab_ref):
    a = _norm_mod(h_ref[...], g_ref[...], mod_ref[0:1, :], mod_ref[1:2, :])
    a16 = a.astype(BF16)
    nq = qkv_ref.shape[-1]
    for n in range(nq // 512):
        qkv_ref[:, n * 512:(n + 1) * 512] = _dot(a16, w_ref[:, n * 512:(n + 1) * 512]).astype(BF16)
    for n in range(gate_ref.shape[-1] // 512):
        gate_ref[:, n * 512:(n + 1) * 512] = _dot(a16, w_ref[:, nq + n * 512:nq + (n + 1) * 512]).astype(BF16)
    ab_ref[...] = _dot_hi(a, wab_ref[...])


def _gdn_proj(h, mod, g, w_main, w_ab):
    b, t, d = h.shape
    nt = t // TM
    nq = 2 * GDN_HEADS * GDN_DK + GDN_HEADS * GDN_DV
    nv = GDN_HEADS * GDN_DV
    return pl.pallas_call(
        _gdn_proj_kernel,
        grid=(b, nt),
        in_specs=[_tok_spec(d), _mod_spec(nt, d), _full_spec((1, d)),
                  _full_spec(w_main.shape), _full_spec(w_ab.shape)],
        out_specs=[_tok_spec(nq), _tok_spec(nv), _tok_spec(LANES)],
        out_shape=[jax.ShapeDtypeStruct((b, t, nq), BF16),
                   jax.ShapeDtypeStruct((b, t, nv), BF16),
                   jax.ShapeDtypeStruct((b, t, LANES), F32)],
        compiler_params=_cparams(("parallel", "parallel")),
        name="gdn_proj",
    )(h, mod, g, w_main, w_ab)


def _gdn_conv_kernel(nt, cur_ref, prev_ref, next_ref, w_ref, q_ref, k_ref, v_ref, ext_ref):
    prev_ok, next_ok = _halo_valid(nt)
    cw = 512
    pad = GDN_CONV // 2
    nqk = GDN_HEADS * GDN_DK
    for cc in range(cur_ref.shape[-1] // cw):
        cs = slice(cc * cw, (cc + 1) * cw)
        ext_ref[0:SUBLANES, :] = jnp.where(prev_ok, prev_ref[HALO - SUBLANES:HALO, cs].astype(F32), 0.0)
        ext_ref[SUBLANES:SUBLANES + TM, :] = cur_ref[:, cs].astype(F32)
        ext_ref[SUBLANES + TM:2 * SUBLANES + TM, :] = jnp.where(next_ok, next_ref[0:SUBLANES, cs].astype(F32), 0.0)
        acc = w_ref[0:1, cs] * ext_ref[pl.ds(SUBLANES - pad, TM), :]
        for tap in range(1, GDN_CONV):
            acc = acc + w_ref[tap:tap + 1, cs] * ext_ref[pl.ds(SUBLANES - pad + tap, TM), :]
        y = _silu(acc)
        for hh in range(cw // LANES):
            col = cc * cw + hh * LANES
            yh = y[:, hh * LANES:(hh + 1) * LANES]
            if col < 2 * nqk:
                yh = yh * lax.rsqrt(jnp.sum(yh * yh, axis=-1, keepdims=True) + RMS_EPS)
            if col < nqk:
                q_ref[:, col:col + LANES] = yh.astype(BF16)
            elif col < 2 * nqk:
                k_ref[:, col - nqk:col - nqk + LANES] = yh.astype(BF16)
            else:
                v_ref[:, col - 2 * nqk:col - 2 * nqk + LANES] = yh.astype(BF16)


def _gdn_conv(qkv, conv_w):
    b, t, c = qkv.shape
    nt = t // TM
    nqk = GDN_HEADS * GDN_DK
    prev, nxt = _halo_specs(nt, c)
    return pl.pallas_call(
        functools.partial(_gdn_conv_kernel, nt),
        grid=(b, nt),
        in_specs=[_tok_spec(c), prev, nxt, _full_spec(conv_w.shape)],
        out_specs=[_tok_spec(nqk), _tok_spec(nqk), _tok_spec(GDN_HEADS * GDN_DV)],
        out_shape=[jax.ShapeDtypeStruct((b, t, nqk), BF16),
                   jax.ShapeDtypeStruct((b, t, nqk), BF16),
                   jax.ShapeDtypeStruct((b, t, GDN_HEADS * GDN_DV), BF16)],
        scratch_shapes=[pltpu.VMEM((TM + 2 * SUBLANES, 512), F32)],
        compiler_params=_cparams(("parallel", "parallel")),
        name="gdn_conv",
    )(qkv, qkv, qkv, conv_w)


def _split(x):
    hi = x.astype(BF16)
    return hi, (x - hi.astype(F32)).astype(BF16)


def _dot16(a, b):
    return _dot(a.astype(BF16), b.astype(BF16))


def _dot_hi(a, b):
    ah, al = _split(a)
    bh, bl = _split(b)
    return _dot(ah, bh) + (_dot(ah, bl) + _dot(al, bh))


def _gdn_scan_kernel(reverse, nb, q_ref, k_ref, v_ref, ab_ref, cf_ref, o_ref,
                     s_ref, gc_ref, gct_ref, beta_ref, u_ref, wq_ref, kd_ref, at_ref):
    nchunk = TM // GDN_CHUNK
    c64 = GDN_CHUNK
    scale = GDN_DK ** -0.5
    lane0 = 2 * GDN_HEADS if reverse else 0
    chains = [(b, h) for b in range(nb) for h in range(GDN_HEADS)]

    @pl.when(pl.program_id(0) == 0)
    def _():
        s_ref[...] = jnp.zeros_like(s_ref)

    row = lax.broadcasted_iota(jnp.int32, (TM, TM), 0)
    col = lax.broadcasted_iota(jnp.int32, (TM, TM), 1)
    same = (row // c64) == (col // c64)
    tri = (col >= row) if reverse else (col <= row)
    cum = jnp.where(jnp.logical_and(same, tri), 1.0, 0.0)
    for b in range(nb):
        ab = ab_ref[b]
        beta_ref[b] = jax.nn.sigmoid(ab)
        x = ab + cf_ref[1:2, :]
        softplus = jnp.maximum(x, 0.0) + jnp.log1p(jnp.exp(-jnp.abs(x)))
        g = -cf_ref[0:1, :] * softplus
        gc = _dot_hi(cum, g)
        gc_ref[b] = gc
        for c in range(nchunk):
            blk = gc[c * c64:(c + 1) * c64, :]
            gct_ref[b, c] = jnp.concatenate([blk, blk], axis=0).T

    ri = lax.broadcasted_iota(jnp.int32, (c64, LANES), 0)
    ln = lax.broadcasted_iota(jnp.int32, (c64, LANES), 1)
    ci = ln % c64
    left = ln < c64
    causal = (ri <= ci) if reverse else (ri >= ci)
    strict = (ri < ci) if reverse else (ri > ci)
    blk16 = (ri // 16) == (ci // 16)
    blk32 = (ri // 32) == (ci // 32)
    m_c1 = jnp.logical_and(strict, jnp.logical_and(blk32, jnp.logical_not(blk16)))[:, :c64]
    m_c2 = jnp.logical_and(strict, jnp.logical_not(blk32))[:, :c64]
    m_l0x = jnp.logical_and(jnp.logical_and(strict, blk16), jnp.logical_not(left))
    eye_l = jnp.where(jnp.logical_and(ri == ci, left), 1.0, 0.0)
    eye64 = eye_l[:, :c64]
    strict64 = strict[:, :c64]
    zpad = jnp.zeros((c64, LANES), F32)

    def prep(it, carry):
        st = []
        for b in range(nb):
            for j in range(PREP_CHUNKS):
                c = it * PREP_CHUNKS + j
                r0 = pl.multiple_of(c * c64, c64)
                rows = pl.ds(r0, c64)
                r_last = r0 if reverse else r0 + (c64 - 1)
                gcb = gc_ref[b, rows, :]
                gctb = gct_ref[b, c]
                glb = gc_ref[b, pl.ds(r_last, 1), :]
                betab = beta_ref[b, rows, :]
                for h in range(GDN_HEADS):
                    hs = slice(h * GDN_DK, (h + 1) * GDN_DK)
                    gl = lane0 + SUBLANES + h
                    k16 = k_ref[b, rows, hs]
                    k32 = k16.astype(F32)
                    beta = betab[:, lane0 + h:lane0 + h + 1]
                    gcol = gcb[:, gl:gl + 1]
                    dec2 = jnp.exp(jnp.minimum(gcol - gctb[gl:gl + 1, :], 0.0))
                    kb = k32 * beta
                    kk2 = _dot_nt(kb.astype(BF16), jnp.concatenate([k16, k16], axis=0))
                    ll2 = kk2 * dec2
                    st.append(dict(hs=hs, b=b, h=h, c=c, rows=rows, k32=k32, k16=k16, beta=beta, gcol=gcol,
                                   kb=kb, glast=glb[:, gl:gl + 1], dec=dec2[:, :c64],
                                   ll=ll2[:, :c64], x=jnp.where(m_l0x, -ll2, eye_l)))
        for _ in range(4):
            for s in st:
                x = s["x"]
                s["x"] = jnp.where(left, x, 0.0) + _dot16(x, jnp.concatenate([zpad, x], axis=0))
        for s in st:
            s["d"] = s["x"][:, :c64]
        for mask in (m_c1, m_c2):
            for s in st:
                s["t"] = _dot16(jnp.where(mask, s["ll"], 0.0), s["d"])
            for s in st:
                s["d"] = s["d"] - _dot16(s["d"], s["t"])
        for s in st:
            s["r"] = eye64 - s["d"] - _dot_hi(jnp.where(strict64, s["ll"], 0.0), s["d"])
        for s in st:
            s["d"] = s["d"] + _dot16(s["d"], s["r"])
        for s in st:
            eg = jnp.exp(s["gcol"])
            s["eg"] = eg
            v32 = v_ref[s["b"], s["rows"], s["hs"]].astype(F32)
            s["u"] = _dot_hi(s["d"], v32 * s["beta"])
            s["w"] = _dot16(s["d"], s["kb"] * eg)
        for s in st:
            b, h, hs, c, rows = s["b"], s["h"], s["hs"], s["c"], s["rows"]
            qs = q_ref[b, rows, hs].astype(F32) * scale
            attn = _dot_nt(qs.astype(BF16), s["k16"]) * jnp.where(causal[:, :c64], s["dec"], 0.0)
            u_ref[b, rows, hs] = s["u"]
            wq_ref[b, c, h, 0:c64, :] = s["w"].astype(BF16)
            wq_ref[b, c, h, c64:2 * c64, :] = (qs * s["eg"]).astype(BF16)
            kd_ref[b, rows, hs] = (s["k32"] * jnp.exp(s["glast"] - s["gcol"])).astype(BF16)
            at_ref[b, c, h] = attn.astype(BF16)
        return carry

    lax.fori_loop(0, nchunk // PREP_CHUNKS, prep, 0)

    def scan(step, carry):
        c = (nchunk - 1 - step) if reverse else step
        r0 = pl.multiple_of(c * c64, c64)
        rows = pl.ds(r0, c64)
        r_last = r0 if reverse else r0 + (c64 - 1)
        glb = [gc_ref[b, pl.ds(r_last, 1), :] for b in range(nb)]
        st = []
        for (b, h) in chains:
            s = s_ref[b, h]
            st.append(dict(b=b, h=h, hs=slice(h * GDN_DK, (h + 1) * GDN_DK), s=s,
                           ws=_dot(wq_ref[b, c, h], s.astype(BF16))))
        for d in st:
            v_new = u_ref[d["b"], rows, d["hs"]] - d["ws"][:c64]
            d["vn"] = v_new.astype(BF16)
        for d in st:
            b, h, hs = d["b"], d["h"], d["hs"]
            o = d["ws"][c64:] + _dot(at_ref[b, c, h], d["vn"])
            o_ref[b, rows, hs] = o.astype(BF16)
            egl = jnp.exp(glb[b][:, lane0 + SUBLANES + h:lane0 + SUBLANES + h + 1])
            s_ref[b, h] = d["s"] * egl + _dot_tn(kd_ref[b, rows, hs], d["vn"])
        return carry

    lax.fori_loop(0, nchunk, scan, 0)


def _gdn_scan(q, k, v, ab, cf, reverse):
    b, t, c = q.shape
    nt = t // TM
    nchunk = TM // GDN_CHUNK

    def tile(s):
        lat = (nt - 1 - s) if reverse else (s - 1)
        return jnp.where(s == 0, nt - 1, lat)

    spec = pl.BlockSpec((b, TM, c), lambda s: (0, tile(s), 0))
    return pl.pallas_call(
        functools.partial(_gdn_scan_kernel, reverse, b),
        grid=(nt,),
        in_specs=[spec, spec, spec,
                  pl.BlockSpec((b, TM, LANES), lambda s: (0, tile(s), 0)),
                  _full_spec((SUBLANES, LANES))],
        out_specs=spec,
        out_shape=jax.ShapeDtypeStruct((b, t, c), BF16),
        scratch_shapes=[pltpu.VMEM((b, GDN_HEADS, GDN_DK, GDN_DV), F32),
                        pltpu.VMEM((b, TM, LANES), F32),
                        pltpu.VMEM((b, nchunk, LANES, LANES), F32),
                        pltpu.VMEM((b, TM, LANES), F32),
                        pltpu.VMEM((b, TM, c), F32),
                        pltpu.VMEM((b, nchunk, GDN_HEADS, 2 * GDN_CHUNK, GDN_DK), BF16),
                        pltpu.VMEM((b, TM, c), BF16),
                        pltpu.VMEM((b, nchunk, GDN_HEADS, GDN_CHUNK, GDN_CHUNK), BF16)],
        compiler_params=_cparams(("arbitrary",)),
        name="gdn_scan_bwd" if reverse else "gdn_scan_fwd",
    )(q, k, v, ab, cf)


def _gdn_out_kernel(of_ref, ob_ref, gate_ref, h_ref, mod_ref, on_ref, w_ref, g_ref, out_ref, y_ref):
    for hh in range(GDN_HEADS):
        hs = slice(hh * GDN_DV, (hh + 1) * GDN_DV)
        o = of_ref[:, hs].astype(F32) + ob_ref[:, hs].astype(F32)
        gt = gate_ref[:, hs].astype(F32)
        y_ref[:, hs] = (_rms(o) * on_ref[...] * _silu(gt)).astype(BF16)
    out = _dot(y_ref[...], w_ref[...])
    out_ref[...] = _residual(h_ref[...], out, g_ref[...], mod_ref[2:3, :])


def _gdn_out(o_f, o_b, gate, h, mod, o_norm, w_out, g):
    b, t, d = h.shape
    nt = t // TM
    c = o_f.shape[-1]
    return pl.pallas_call(
        _gdn_out_kernel,
        grid=(b, nt),
        in_specs=[_tok_spec(c), _tok_spec(c), _tok_spec(c), _tok_spec(d), _mod_spec(nt, d),
                  _full_spec((1, GDN_DV)), _full_spec(w_out.shape), _full_spec((1, d))],
        out_specs=_tok_spec(d),
        out_shape=jax.ShapeDtypeStruct((b, t, d), F32),
        scratch_shapes=[pltpu.VMEM((TM, c), BF16)],
        compiler_params=_cparams(("parallel", "parallel")),
        name="gdn_out",
    )(o_f, o_b, gate, h, mod, o_norm, w_out, g)


def _conf_in_kernel(h_ref, mod_ref, g_ref, w_ref, b_ref, u_ref):
    a16 = _norm_mod(h_ref[...], g_ref[...], mod_ref[0:1, :], mod_ref[1:2, :]).astype(BF16)
    d = u_ref.shape[-1]
    cw = 512
    for n in range(d // cw):
        lin = _dot(a16, w_ref[:, n * cw:(n + 1) * cw]) + b_ref[:, n * cw:(n + 1) * cw]
        gt = _dot(a16, w_ref[:, d + n * cw:d + (n + 1) * cw]) + b_ref[:, d + n * cw:d + (n + 1) * cw]
        u_ref[:, n * cw:(n + 1) * cw] = (lin * jax.nn.sigmoid(gt)).astype(BF16)


def _conf_in(h, mod, g, w_in, b_in):
    b, t, d = h.shape
    nt = t // TM
    return pl.pallas_call(
        _conf_in_kernel,
        grid=(b, nt),
        in_specs=[_tok_spec(d), _mod_spec(nt, d), _full_spec((1, d)),
                  _full_spec(w_in.shape), _full_spec(b_in.shape)],
        out_specs=_tok_spec(d),
        out_shape=jax.ShapeDtypeStruct((b, t, d), BF16),
        compiler_params=_cparams(("parallel", "parallel")),
        name="conf_in",
    )(h, mod, g, w_in, b_in)


def _conf_conv_kernel(nt, cur_ref, prev_ref, next_ref, dw_ref, dwb_ref, lng_ref, lnb_ref,
                      w_ref, b_ref, h_ref, mod_ref, g_ref, out_ref, ext_ref, sh_ref, y_ref):
    prev_ok, next_ok = _halo_valid(nt)
    pad = CONF_CONV // 2
    ext_ref[0:HALO, :] = jnp.where(prev_ok, prev_ref[...].astype(F32), 0.0)
    ext_ref[HALO:HALO + TM, :] = cur_ref[...].astype(F32)
    ext_ref[HALO + TM:2 * HALO + TM, :] = jnp.where(next_ok, next_ref[...].astype(F32), 0.0)
    for r in range(1, SUBLANES):
        sh_ref[r - 1] = ext_ref[pl.ds(r, sh_ref.shape[1]), :]
    rw, cw = 64, 256
    d = cur_ref.shape[-1]
    for cc in range(d // cw):
        cs = slice(cc * cw, (cc + 1) * cw)
        for rr in range(TM // rw):
            acc = None
            for tap in range(CONF_CONV):
                off = HALO - pad + tap
                r = off % SUBLANES
                src = ext_ref if r == 0 else sh_ref.at[r - 1]
                term = dw_ref[tap:tap + 1, cs] * src[pl.ds(off - r + rr * rw, rw), cs]
                acc = term if acc is None else acc + term
            y_ref[rr * rw:(rr + 1) * rw, cs] = acc + dwb_ref[:, cs]
    y = y_ref[...]
    yc = y - jnp.mean(y, axis=-1, keepdims=True)
    yn = yc * lax.rsqrt(jnp.mean(yc * yc, axis=-1, keepdims=True) + LN_EPS) * lng_ref[...] + lnb_ref[...]
    out = _dot(_silu(yn).astype(BF16), w_ref[...]) + b_ref[...]
    out_ref[...] = _residual(h_ref[...], out, g_ref[...], mod_ref[2:3, :])


def _conf_conv(u, dw, dw_b, ln_g, ln_b, w_out, b_out, h, mod, g):
    b, t, d = h.shape
    nt = t // TM
    prev, nxt = _halo_specs(nt, d)
    vec = _full_spec((1, d))
    return pl.pallas_call(
        functools.partial(_conf_conv_kernel, nt),
        grid=(b, nt),
        in_specs=[_tok_spec(d), prev, nxt, _full_spec(dw.shape), vec, vec, vec,
                  _full_spec(w_out.shape), vec, _tok_spec(d), _mod_spec(nt, d), vec],
        out_specs=_tok_spec(d),
        out_shape=jax.ShapeDtypeStruct((b, t, d), F32),
        scratch_shapes=[pltpu.VMEM((TM + 2 * HALO, d), F32),
                        pltpu.VMEM((SUBLANES - 1, TM + 2 * HALO - SUBLANES, d), F32),
                        pltpu.VMEM((TM, d), F32)],
        compiler_params=_cparams(("parallel", "parallel")),
        name="conf_conv",
    )(u, u, u, dw, dw_b, ln_g, ln_b, w_out, b_out, h, mod, g)


def _mla_proj_kernel(h_ref, mod_ref, g_ref, win_ref, qn_ref, kvn_ref, wuq_ref, wukv_ref, cs_ref,
                     q_ref, k_ref, v_ref):
    a16 = _norm_mod(h_ref[...], g_ref[...], mod_ref[0:1, :], mod_ref[1:2, :]).astype(BF16)
    p = _dot(a16, win_ref[...])
    cq = _rms(p[:, :MLA_Q_RANK]) * qn_ref[...]
    ckv = _rms(p[:, MLA_Q_RANK:MLA_Q_RANK + MLA_KV_RANK]) * kvn_ref[...]
    kr = p[:, MLA_Q_RANK + MLA_KV_RANK:]
    cs = cs_ref[...]
    kx = kr * cs
    krot = (kx + pltpu.roll(kx, MLA_ROPE, 1)).astype(BF16)
    scale = (MLA_NOPE + MLA_ROPE) ** -0.5 * LOG2E
    cq16 = cq.astype(BF16)
    ckv16 = ckv.astype(BF16)
    hw = MLA_NOPE + 2 * MLA_ROPE
    for hh in range(MLA_HEADS):
        qh = _dot(cq16, wuq_ref[:, hh * hw:(hh + 1) * hw]) * scale
        q_ref[:, hh * hw:hh * hw + MLA_NOPE] = qh[:, :MLA_NOPE].astype(BF16)
        q_ref[:, hh * hw + MLA_NOPE:(hh + 1) * hw] = (qh[:, MLA_NOPE:] * cs).astype(BF16)
        k_ref[:, hh * hw:hh * hw + MLA_NOPE] = _dot(
            ckv16, wukv_ref[:, hh * MLA_NOPE:(hh + 1) * MLA_NOPE]).astype(BF16)
        k_ref[:, hh * hw + MLA_NOPE:(hh + 1) * hw] = krot
        nk = MLA_HEADS * MLA_NOPE
        v_ref[:, 2 * hh * MLA_V:(2 * hh + 1) * MLA_V] = _dot(
            ckv16, wukv_ref[:, nk + hh * MLA_V:nk + (hh + 1) * MLA_V]).astype(BF16)
        v_ref[:, (2 * hh + 1) * MLA_V:(2 * hh + 2) * MLA_V] = jnp.ones((TM, MLA_V), BF16)


def _mla_proj(h, mod, g, w_in, q_norm, kv_norm, w_uq, w_ukv, cs_tab):
    b, t, d = h.shape
    nt = t // TM
    hw = MLA_NOPE + 2 * MLA_ROPE
    return pl.pallas_call(
        _mla_proj_kernel,
        grid=(b, nt),
        in_specs=[_tok_spec(d), _mod_spec(nt, d), _full_spec((1, d)), _full_spec(w_in.shape),
                  _full_spec(q_norm.shape), _full_spec(kv_norm.shape),
                  _full_spec(w_uq.shape), _full_spec(w_ukv.shape),
                  pl.BlockSpec((TM, LANES), lambda bb, j: (j, 0))],
        out_specs=[_tok_spec(MLA_HEADS * hw), _tok_spec(MLA_HEADS * hw), _tok_spec(MLA_HEADS * 2 * MLA_V)],
        out_shape=[jax.ShapeDtypeStruct((b, t, MLA_HEADS * hw), BF16),
                   jax.ShapeDtypeStruct((b, t, MLA_HEADS * hw), BF16),
                   jax.ShapeDtypeStruct((b, t, MLA_HEADS * 2 * MLA_V), BF16)],
        compiler_params=_cparams(("parallel", "parallel")),
        name="mla_proj",
    )(h, mod, g, w_in, q_norm, kv_norm, w_uq, w_ukv, cs_tab)


def _attn_kernel(tk, q_ref, k_ref, v_ref, o_ref, m_ref, acc_ref):
    q = q_ref[...]
    for i in range(k_ref.shape[0] // tk):
        rows = slice(i * tk, (i + 1) * tk)
        s = _dot_nt(q, k_ref[rows, :])
        m_blk = jnp.max(s, axis=-1, keepdims=True)
        if i == 0:
            m_new = m_blk
            acc_ref[...] = _dot(jnp.exp2(s - m_new).astype(BF16), v_ref[rows, :])
        else:
            m_old = m_ref[...]
            m_new = jnp.maximum(m_old, m_blk)
            acc_ref[...] = jnp.exp2(m_old - m_new) * acc_ref[...] + _dot(
                jnp.exp2(s - m_new).astype(BF16), v_ref[rows, :])
        m_ref[...] = m_new
    acc = acc_ref[...]
    o_ref[...] = (acc[:, :MLA_V] / acc[:, MLA_V:]).astype(BF16)


def _attention(q, k, v, tq, q_rows, q_row0, k_rows, k_row0, name):
    b = q.shape[0]
    hw = MLA_NOPE + 2 * MLA_ROPE
    tk = next(c for c in (768, 512, 256) if k_rows % c == 0)
    q0 = q_row0 // tq
    k0 = k_row0 // k_rows
    return pl.pallas_call(
        functools.partial(_attn_kernel, tk),
        grid=(b, MLA_HEADS, q_rows // tq),
        in_specs=[pl.BlockSpec((None, tq, hw), lambda bb, hh, i: (bb, q0 + i, hh)),
                  pl.BlockSpec((None, k_rows, hw), lambda bb, hh, i: (bb, k0, hh)),
                  pl.BlockSpec((None, k_rows, 2 * MLA_V), lambda bb, hh, i: (bb, k0, hh))],
        out_specs=pl.BlockSpec((None, tq, MLA_V), lambda bb, hh, i: (bb, i, hh)),
        out_shape=jax.ShapeDtypeStruct((b, q_rows, MLA_HEADS * MLA_V), BF16),
        scratch_shapes=[pltpu.VMEM((tq, 1), F32), pltpu.VMEM((tq, 2 * MLA_V), F32)],
        compiler_params=_cparams(("parallel", "parallel", "arbitrary")),
        name=name,
    )(q, k, v)


def _proj_out_kernel(o_ref, w_ref, h_ref, mod_ref, g_ref, out_ref):
    out = _dot(o_ref[...], w_ref[...])
    out_ref[...] = _residual(h_ref[...], out, g_ref[...], mod_ref[2:3, :])


def _proj_out(o, w, h, mod, g):
    b, t, d = h.shape
    nt = t // TM
    return pl.pallas_call(
        _proj_out_kernel,
        grid=(b, nt),
        in_specs=[_tok_spec(o.shape[-1]), _full_spec(w.shape), _tok_spec(d), _mod_spec(nt, d),
                  _full_spec((1, d))],
        out_specs=_tok_spec(d),
        out_shape=jax.ShapeDtypeStruct((b, t, d), F32),
        compiler_params=_cparams(("parallel", "parallel")),
        name="mla_out",
    )(o, w, h, mod, g)


def _ffn_kernel(fc, h_ref, mod_ref, g2_ref, wg_ref, wu_ref, wd_ref, g3_ref, out_ref, acc_ref):
    h = h_ref[...]
    a16 = _norm_mod(h, g2_ref[...], mod_ref[3:4, :], mod_ref[4:5, :]).astype(BF16)
    for c in range(wg_ref.shape[-1] // fc):
        fs = slice(c * fc, (c + 1) * fc)
        act = (_silu(_dot(a16, wg_ref[:, fs])) * _dot(a16, wu_ref[:, fs])).astype(BF16)
        part = _dot(act, wd_ref[fs, :])
        if c == 0:
            acc_ref[...] = part
        else:
            acc_ref[...] += part
    out_ref[...] = _residual(h, acc_ref[...], g3_ref[...], mod_ref[5:6, :])


def _ffn(h, mod, g2, w_gate, w_up, w_down, g3, n_tiles):
    b, t, d = h.shape
    nt = t // TM
    return pl.pallas_call(
        functools.partial(_ffn_kernel, 1408),
        grid=(b, n_tiles),
        in_specs=[_tok_spec(d), _mod_spec(nt, d), _full_spec((1, d)), _full_spec(w_gate.shape),
                  _full_spec(w_up.shape), _full_spec(w_down.shape), _full_spec((1, d))],
        out_specs=_tok_spec(d),
        out_shape=jax.ShapeDtypeStruct((b, n_tiles * TM, d), F32),
        scratch_shapes=[pltpu.VMEM((TM, d), F32)],
        compiler_params=_cparams(("parallel", "parallel")),
        name="ffn",
    )(h, mod, g2, w_gate, w_up, w_down, g3)


def _moe_route_kernel(h_ref, mod_ref, g2_ref, wr_ref, a_ref, rt_ref):
    lane = lax.broadcasted_iota(jnp.int32, (TM, LANES), 1).astype(F32)
    a = _norm_mod(h_ref[...], g2_ref[...], mod_ref[3:4, :], mod_ref[4:5, :])
    for s in range(SUBLANES):
        a_ref[pl.ds(s, TM, stride=SUBLANES), :] = a[:, s * LANES:(s + 1) * LANES]
    logits = jnp.where(lane < N_EXPERTS, _dot_hi(a, wr_ref[...]), -jnp.inf)
    m1 = jnp.max(logits, axis=-1, keepdims=True)
    i1 = jnp.min(jnp.where(logits == m1, lane, float(LANES)), axis=-1, keepdims=True)
    rest = jnp.where(lane == i1, -jnp.inf, logits)
    m2 = jnp.max(rest, axis=-1, keepdims=True)
    i2 = jnp.min(jnp.where(rest == m2, lane, float(LANES)), axis=-1, keepdims=True)
    e2 = jnp.exp(m2 - m1)
    w1 = 1.0 / (1.0 + e2)
    rt_ref[...] = (jnp.where(lane == 0.0, i1, 0.0) + jnp.where(lane == 1.0, i2, 0.0)
                   + jnp.where(lane == 2.0, w1, 0.0) + jnp.where(lane == 3.0, e2 * w1, 0.0))


def _moe_route(h, mod, g2, w_router, n_tiles):
    b, t, d = h.shape
    nt = t // TM
    return pl.pallas_call(
        _moe_route_kernel,
        grid=(b, n_tiles),
        in_specs=[_tok_spec(d), _mod_spec(nt, d), _full_spec((1, d)), _full_spec(w_router.shape)],
        out_specs=[pl.BlockSpec((TM * SUBLANES, LANES), lambda bb, j: (bb * n_tiles + j, 0)),
                   _tok_spec(LANES)],
        out_shape=[jax.ShapeDtypeStruct((b * n_tiles * TM * SUBLANES, LANES), F32),
                   jax.ShapeDtypeStruct((b, n_tiles * TM, LANES), F32)],
        compiler_params=_cparams(("parallel", "parallel")),
        name="moe_route",
    )(h, mod, g2, w_router)


def _gather_tiles(idx_ref, base, n, src_hbm, dst, sem):
    def body(r, carry):
        row = pl.multiple_of(idx_ref[base + r], SUBLANES)
        pltpu.make_async_copy(src_hbm.at[pl.ds(row, SUBLANES), :],
                              dst.at[pl.ds(pl.multiple_of(r * SUBLANES, SUBLANES), SUBLANES), :], sem).start()
        return carry
    lax.fori_loop(0, n, body, 0, unroll=8)


def _wait_tiles(n, src_hbm, dst, sem):
    pltpu.make_async_copy(src_hbm.at[pl.ds(0, n * SUBLANES), :], dst, sem).wait()


def _untile(buf, n):
    return jnp.concatenate([buf[pl.ds(s, n, stride=SUBLANES), :] for s in range(SUBLANES)], axis=1)


def _moe_dispatch_kernel(dst_ref, last_ref, nact_ref, a_ref, x_hbm, stage_ref, sem, zsem):
    i = pl.program_id(0)
    n = pl.num_programs(0)
    rows = TM * SUBLANES
    slot = i % 2
    n_tiles_max = x_hbm.shape[0] // rows

    @pl.when(i == 0)
    def _():
        zero = stage_ref.at[1]
        zero[...] = jnp.zeros_like(zero)

        def zero_tile(t):
            return pltpu.make_async_copy(zero, x_hbm.at[pl.ds(pl.multiple_of(t * rows, rows), rows), :], zsem)

        for e in range(N_EXPERTS):
            zero_tile(last_ref[e]).start()
            pl.when(nact_ref[0] + e < n_tiles_max)(lambda e=e: zero_tile(nact_ref[0] + e).start())
        for e in range(N_EXPERTS):
            zero_tile(0).wait()
            pl.when(nact_ref[0] + e < n_tiles_max)(lambda: zero_tile(0).wait())

    stage = stage_ref.at[slot]
    stage[...] = a_ref[...]

    def body(r, carry):
        src = pl.multiple_of(r * SUBLANES, SUBLANES)
        for k in range(2):
            dst = pl.multiple_of(dst_ref[(i * TM + r) * 2 + k], SUBLANES)
            pltpu.make_async_copy(stage.at[pl.ds(src, SUBLANES), :], x_hbm.at[pl.ds(dst, SUBLANES), :],
                                  sem.at[slot]).start()
        return carry
    lax.fori_loop(0, TM, body, 0, unroll=4)

    def wait(sl):
        for k in range(2):
            pltpu.make_async_copy(stage_ref.at[sl], x_hbm.at[pl.ds(0, rows), :], sem.at[sl]).wait()

    @pl.when(i > 0)
    def _():
        wait(1 - slot)

    @pl.when(i == n - 1)
    def _():
        wait(slot)


def _moe_dispatch(dst_rows, last_tile, n_active, a2d, n_rows):
    n_steps = a2d.shape[0] // (TM * SUBLANES)
    return pl.pallas_call(
        _moe_dispatch_kernel,
        grid_spec=pltpu.PrefetchScalarGridSpec(
            num_scalar_prefetch=3,
            grid=(n_steps,),
            in_specs=[pl.BlockSpec((TM * SUBLANES, LANES), lambda i, ds, lt, na: (i, 0))],
            out_specs=pl.BlockSpec(memory_space=pl.ANY),
            scratch_shapes=[pltpu.VMEM((2, TM * SUBLANES, LANES), F32), pltpu.SemaphoreType.DMA((2,)),
                            pltpu.SemaphoreType.DMA(())]),
        out_shape=jax.ShapeDtypeStruct((n_rows * SUBLANES, LANES), F32),
        compiler_params=_cparams(("arbitrary",)),
        name="moe_dispatch",
    )(dst_rows, last_tile, n_active, a2d)


def _moe_experts_kernel(te_ref, nact_ref, x_ref, wg_ref, wu_ref, wd_ref, y_ref):
    i = pl.program_id(0)

    @pl.when(i < nact_ref[0])
    def _():
        x16 = _untile(x_ref, TM).astype(BF16)
        f = wg_ref.shape[-1]
        acc = None
        for c0 in range(0, f, 512):
            fs = slice(c0, min(c0 + 512, f))
            act = (_silu(_dot(x16, wg_ref[:, fs])) * _dot(x16, wu_ref[:, fs])).astype(BF16)
            part = _dot(act, wd_ref[fs, :])
            acc = part if acc is None else acc + part
        for s in range(SUBLANES):
            y_ref[pl.ds(s, TM, stride=SUBLANES), :] = acc[:, s * LANES:(s + 1) * LANES]

    @pl.when(i >= nact_ref[0])
    def _():
        y_ref[...] = jnp.zeros_like(y_ref)


def _moe_experts(tile_expert, n_active, x2d, w_gate, w_up, w_down):
    n_tiles = tile_expert.shape[0]
    _, d, f = w_gate.shape
    wspec_in = pl.BlockSpec((None, d, f), lambda i, te, na: (te[i], 0, 0))
    wspec_out = pl.BlockSpec((None, f, d), lambda i, te, na: (te[i], 0, 0))
    return pl.pallas_call(
        _moe_experts_kernel,
        grid_spec=pltpu.PrefetchScalarGridSpec(
            num_scalar_prefetch=2,
            grid=(n_tiles,),
            in_specs=[pl.BlockSpec((TM * SUBLANES, LANES), lambda i, te, na: (jnp.minimum(i, na[0] - 1), 0)),
                      wspec_in, wspec_in, wspec_out],
            out_specs=pl.BlockSpec((TM * SUBLANES, LANES), lambda i, te, na: (i, 0))),
        out_shape=jax.ShapeDtypeStruct((n_tiles * TM * SUBLANES, LANES), F32),
        compiler_params=_cparams(("arbitrary",)),
        name="moe_experts",
    )(tile_expert, n_active, x2d, w_gate, w_up, w_down)


def _moe_combine_kernel(n_tiles, dst_ref, y_hbm, rt_ref, h_ref, mod_ref, g3_ref, out_ref, ybuf, sem):
    step = pl.program_id(0) * n_tiles + pl.program_id(1)
    nsteps = pl.num_programs(0) * n_tiles

    @pl.when(step == 0)
    def _():
        _gather_tiles(dst_ref, 0, 2 * TM, y_hbm, ybuf.at[0], sem.at[0])

    @pl.when(step + 1 < nsteps)
    def _():
        nxt = (step + 1) % 2
        _gather_tiles(dst_ref, (step + 1) * 2 * TM, 2 * TM, y_hbm, ybuf.at[nxt], sem.at[nxt])

    cur = step % 2
    _wait_tiles(2 * TM, y_hbm, ybuf.at[cur], sem.at[cur])
    yb = ybuf.at[cur]
    w1 = rt_ref[:, 2:3]
    w2 = rt_ref[:, 3:4]
    f = jnp.concatenate(
        [w1 * yb[pl.ds(s, TM, stride=2 * SUBLANES), :] + w2 * yb[pl.ds(SUBLANES + s, TM, stride=2 * SUBLANES), :]
         for s in range(SUBLANES)], axis=1)
    out_ref[...] = _residual(h_ref[...], f, g3_ref[...], mod_ref[5:6, :])


def _moe_combine(dst_rows, y2d, rt, h, mod, g3, n_tiles):
    b, t, d = h.shape
    nt = t // TM
    tok = lambda c: pl.BlockSpec((None, TM, c), lambda bb, j, ds: (bb, j, 0))
    return pl.pallas_call(
        functools.partial(_moe_combine_kernel, n_tiles),
        grid_spec=pltpu.PrefetchScalarGridSpec(
            num_scalar_prefetch=1,
            grid=(b, n_tiles),
            in_specs=[pl.BlockSpec(memory_space=pl.ANY), tok(LANES), tok(d),
                      pl.BlockSpec((None, SUBLANES, d),
                                   lambda bb, j, ds: (jnp.where(j == nt - 1, 0, bb + 1), 0, 0)),
                      pl.BlockSpec((1, d), lambda bb, j, ds: (0, 0))],
            out_specs=tok(d),
            scratch_shapes=[pltpu.VMEM((2, 2 * TM * SUBLANES, LANES), F32), pltpu.SemaphoreType.DMA((2,))]),
        out_shape=jax.ShapeDtypeStruct((b, n_tiles * TM, d), F32),
        compiler_params=_cparams(("arbitrary", "arbitrary")),
        name="moe_combine",
    )(dst_rows, y2d, rt, h, mod, g3)


def _moe(h, mod, g2, w_router, w_gate, w_up, w_down, g3, n_tiles):
    b = h.shape[0]
    n_tok = b * n_tiles * TM
    a2d, rt = _moe_route(h, mod, g2, w_router, n_tiles)
    expert = rt.reshape(n_tok, LANES)[:, :2].astype(jnp.int32).reshape(-1)
    onehot = (expert[:, None] == jnp.arange(N_EXPERTS)[None, :]).astype(jnp.int32)
    rank = jnp.sum((jnp.cumsum(onehot, axis=0) - onehot) * onehot, axis=1)
    count = jnp.sum(onehot, axis=0)
    padded = (count + TM - 1) // TM * TM
    ends = jnp.cumsum(padded)
    dest = (ends - padded)[expert] + rank
    n_tiles_max = 2 * n_tok // TM + N_EXPERTS
    tile_start = jnp.arange(n_tiles_max, dtype=jnp.int32) * TM
    tile_expert = jnp.minimum(jnp.sum(ends[None, :] <= tile_start[:, None], axis=1), N_EXPERTS - 1)
    n_active = (ends[-1] // TM).reshape(1).astype(jnp.int32)
    last_tile = jnp.maximum(ends // TM - 1, 0).astype(jnp.int32)
    dst_rows = (dest * SUBLANES).astype(jnp.int32)
    x2d = _moe_dispatch(dst_rows, last_tile, n_active, a2d, n_tiles_max * TM)
    y2d = _moe_experts(tile_expert.astype(jnp.int32), n_active, x2d, w_gate, w_up, w_down)
    return _moe_combine(dst_rows, y2d, rt, h, mod, g3, n_tiles)


def _pad_rows(a, rows):
    return jnp.pad(a, ((0, rows - a.shape[0]),) + ((0, 0),) * (a.ndim - 1))


def _rope_table(seq, ctx_len):
    n_freq = MLA_ROPE // 4
    t = jnp.arange(seq)
    pos = jnp.stack([(t // GRID_W).astype(F32), (t % GRID_W).astype(F32)], axis=1)
    inv = ROPE_THETA ** (-jnp.arange(n_freq, dtype=F32) / n_freq)
    ang = pos[:, :, None] * inv
    cos = jnp.repeat(jnp.cos(ang)[:, :, None, :], 2, axis=2).reshape(seq, MLA_ROPE)
    sin = jnp.stack([-jnp.sin(ang), jnp.sin(ang)], axis=2).reshape(seq, MLA_ROPE)
    lat = jnp.concatenate([cos, sin], axis=1)
    ctx = jnp.concatenate([jnp.ones((ctx_len, MLA_ROPE), F32), jnp.zeros((ctx_len, MLA_ROPE), F32)], axis=1)
    return jnp.concatenate([lat, ctx], axis=0)


def _swap_halves(w):
    shp = w.shape
    return jnp.flip(w.reshape(shp[:-1] + (2, 2, MLA_ROPE // 4)), axis=-2).reshape(shp)


def kernel(x, c, ctx, c_ctx, w_mod, b_mod, norm_g, gdn_w_in, gdn_conv, gdn_a_log, gdn_dt_bias, gdn_o_norm, gdn_w_out, conf_w_in, conf_b_in, conf_dw, conf_dw_b, conf_ln_g, conf_ln_b, conf_w_out, conf_b_out, mla_w_in, mla_q_norm, mla_kv_norm, mla_w_uq, mla_w_ukv, mla_w_o, ffn_w_gate, ffn_w_up, ffn_w_down, moe_router, moe_w_gate, moe_w_up, moe_w_down):
    bsz, seq, d = x.shape
    ctx_len = ctx.shape[1]
    depth = w_mod.shape[0]
    assert ctx_len == TM and seq % (4 * TM) == 0 and bsz + 1 <= SUBLANES
    nt = (seq + ctx_len) // TM
    n_lat = seq // TM

    cvec = _pad_rows(jnp.concatenate([c_ctx[None, :], c], axis=0), SUBLANES)
    mods = _adaln(cvec, w_mod, b_mod)
    mods = mods.reshape(depth, SUBLANES, N_MOD, d)[:, :bsz + 1]
    mods = jnp.pad(mods, ((0, 0), (0, 0), (0, SUBLANES - N_MOD), (0, 0)))

    h = jnp.concatenate([x, ctx], axis=1)
    rope_cs = _rope_table(seq, ctx_len)

    n_mix = 3
    for i in range(depth):
        last = i == depth - 1
        mod = mods[i]
        g = norm_g[i].reshape(4, 1, d)
        kind, j = i % n_mix, i // n_mix
        if kind == 0:
            nqk = GDN_HEADS * GDN_DK
            nmain = 2 * nqk + 2 * GDN_HEADS * GDN_DV
            w_in = gdn_w_in[j]
            w_main = w_in[:, :nmain].astype(BF16)
            w_tail = w_in[:, nmain:]
            nh = GDN_HEADS
            w_ab = jnp.concatenate([w_tail[:, (2 * part + dr) * nh:(2 * part + dr + 1) * nh]
                                    for dr in range(2) for part in range(2)], axis=1)
            w_ab = jnp.pad(w_ab, ((0, 0), (0, LANES - 4 * nh)))

            def decay_lanes(p):
                z = jnp.zeros((nh,), F32)
                return jnp.pad(jnp.concatenate([z, p[0], z, p[1]]), (0, LANES - 4 * nh))

            cf = _pad_rows(jnp.stack([decay_lanes(jnp.exp(gdn_a_log[j])), decay_lanes(gdn_dt_bias[j])]),
                           SUBLANES)
            qkv, gate, ab = _gdn_proj(h, mod, g[0], w_main, w_ab)
            q, k, v = _gdn_conv(qkv, _pad_rows(gdn_conv[j], SUBLANES))
            o_f = _gdn_scan(q, k, v, ab, cf, reverse=False)
            o_b = _gdn_scan(q, k, v, ab, cf, reverse=True)
            h = _gdn_out(o_f, o_b, gate, h, mod, gdn_o_norm[j].reshape(1, -1),
                         gdn_w_out[j].astype(BF16), g[1])
        elif kind == 1:
            u = _conf_in(h, mod, g[0], conf_w_in[j].astype(BF16), conf_b_in[j].reshape(1, -1))
            h = _conf_conv(u, _pad_rows(conf_dw[j], 32), conf_dw_b[j].reshape(1, -1),
                           conf_ln_g[j].reshape(1, -1), conf_ln_b[j].reshape(1, -1),
                           conf_w_out[j].astype(BF16), conf_b_out[j].reshape(1, -1), h, mod, g[1])
        else:
            w_in = mla_w_in[j]
            w_kr = w_in[:, MLA_Q_RANK + MLA_KV_RANK:]
            w_in2 = jnp.concatenate([w_in, _swap_halves(w_kr)], axis=1).astype(BF16)
            hq = MLA_NOPE + MLA_ROPE
            w_uq = mla_w_uq[j].reshape(MLA_Q_RANK, MLA_HEADS, hq)
            w_uq2 = jnp.concatenate([w_uq, _swap_halves(w_uq[..., MLA_NOPE:])], axis=-1)
            w_uq2 = w_uq2.reshape(MLA_Q_RANK, -1).astype(BF16)
            w_ukv = mla_w_ukv[j].reshape(MLA_KV_RANK, MLA_HEADS, MLA_NOPE + MLA_V)
            w_ukv2 = jnp.concatenate([w_ukv[..., :MLA_NOPE].reshape(MLA_KV_RANK, -1),
                                      w_ukv[..., MLA_NOPE:].reshape(MLA_KV_RANK, -1)], axis=1).astype(BF16)
            qa, ka, va = _mla_proj(h, mod, g[0], w_in2, mla_q_norm[j].reshape(1, -1),
                                   mla_kv_norm[j].reshape(1, -1), w_uq2, w_ukv2, rope_cs)
            o_l = _attention(qa, ka, va, 4 * TM, seq, 0, seq + ctx_len, 0, "attn_latent")
            o_c = _attention(qa, ka, va, TM, ctx_len, seq, ctx_len, seq, "attn_context")
            o = jnp.concatenate([o_l, o_c], axis=1)
            h = _proj_out(o, mla_w_o[j].astype(BF16), h, mod, g[1])

        n_tiles = n_lat if last else nt
        f = i // 2
        if i % 2 == 0:
            h = _ffn(h, mod, g[2], ffn_w_gate[f].astype(BF16), ffn_w_up[f].astype(BF16),
                     ffn_w_down[f].astype(BF16), g[3], n_tiles)
        else:
            w_r = jnp.pad(moe_router[f], ((0, 0), (0, LANES - N_EXPERTS)))
            h = _moe(h, mod, g[2], w_r, moe_w_gate[f].astype(BF16), moe_w_up[f].astype(BF16),
                     moe_w_down[f].astype(BF16), g[3], n_tiles)
    return h
```

```python
import functools

import jax
import jax.numpy as jnp
from jax import lax
from jax.experimental import pallas as pl
from jax.experimental.pallas import tpu as pltpu

F32 = jnp.float32
BF16 = jnp.bfloat16

N_MOD = 6
RMS_EPS = 1e-6
LN_EPS = 1e-5
GRID_W = 64
GDN_HEADS = 8
GDN_DK = 128
GDN_DV = 128
GDN_CONV = 5
GDN_CHUNK = 64
CONF_CONV = 31
MLA_HEADS = 8
MLA_NOPE = 128
MLA_ROPE = 64
MLA_V = 128
MLA_Q_RANK = 512
MLA_KV_RANK = 256
ROPE_THETA = 10000.0
LOG2E = 1.4426950408889634
N_EXPERTS = 8

LANES = 128
SUBLANES = 8
TM = 256
HALO = 16
PREP_CHUNKS = 2
VMEM_LIMIT = 56 * 1024 * 1024


def _cparams(sem):
    return pltpu.CompilerParams(dimension_semantics=sem, vmem_limit_bytes=VMEM_LIMIT)


def _dot(a, b, **kw):
    return jnp.dot(a, b, preferred_element_type=F32, **kw)


def _dot_nt(a, b):
    return lax.dot_general(a, b, (((1,), (1,)), ((), ())), preferred_element_type=F32)


def _dot_tn(a, b):
    return lax.dot_general(a, b, (((0,), (0,)), ((), ())), preferred_element_type=F32)


def _silu(x):
    return x * jax.nn.sigmoid(x)


def _rms(x, eps=RMS_EPS):
    return x * lax.rsqrt(jnp.mean(x * x, axis=-1, keepdims=True) + eps)


def _norm_mod(h, g, shift, scale):
    return _rms(h) * g * (1.0 + scale) + shift


def _residual(h, o, g, gate):
    return h + gate * (_rms(o) * g)


def _tok_spec(c):
    return pl.BlockSpec((None, TM, c), lambda b, j: (b, j, 0))


def _full_spec(shape):
    nd = len(shape)
    return pl.BlockSpec(shape, lambda *_: (0,) * nd)


def _mod_spec(nt, d):
    return pl.BlockSpec((None, SUBLANES, d), lambda b, j: (jnp.where(j == nt - 1, 0, b + 1), 0, 0))


def _halo_specs(nt, c):
    per = TM // HALO
    prev = pl.BlockSpec((None, HALO, c), lambda b, j: (b, jnp.maximum(j * per - 1, 0), 0))
    nxt = pl.BlockSpec((None, HALO, c), lambda b, j: (b, jnp.minimum((j + 1) * per, nt * per - 1), 0))
    return prev, nxt


def _halo_valid(nt):
    j = pl.program_id(1)
    prev_ok = jnp.logical_and(j != 0, j != nt - 1)
    next_ok = j < nt - 2
    return prev_ok, next_ok


def _adaln_kernel(c_ref, w_ref, b_ref, o_ref):
    o_ref[...] = _dot_hi(_silu(c_ref[...]), w_ref[...]) + b_ref[...]


def _adaln(cvec, w_mod, b_mod):
    depth, d, n = w_mod.shape
    tn = 1536
    return pl.pallas_call(
        _adaln_kernel,
        grid=(depth, n // tn),
        in_specs=[pl.BlockSpec((SUBLANES, d), lambda i, k: (0, 0)),
                  pl.BlockSpec((None, d, tn), lambda i, k: (i, 0, k)),
                  pl.BlockSpec((None, 1, tn), lambda i, k: (i, 0, k))],
        out_specs=pl.BlockSpec((None, SUBLANES, tn), lambda i, k: (i, 0, k)),
        out_shape=jax.ShapeDtypeStruct((depth, SUBLANES, n), F32),
        compiler_params=_cparams(("parallel", "parallel")),
        name="adaln",
    )(cvec, w_mod, b_mod.reshape(depth, 1, n))


def _gdn_in_kernel(nt, h_ref, hp_ref, hn_ref, mod_ref, g_ref, w_ref, wab_ref, cw_ref,
                   q_ref, k_ref, v_ref, gate_ref, ab_ref, ext_ref):
    prev_ok, next_ok = _halo_valid(nt)
    g, shift, scale = g_ref[...], mod_ref[0:1, :], mod_ref[1:2, :]
    a = _norm_mod(h_ref[...], g, shift, scale)
    a16 = a.astype(BF16)
    ax16 = jnp.concatenate([_norm_mod(hp_ref[...], g, shift, scale), a,
                            _norm_mod(hn_ref[...], g, shift, scale)], axis=0).astype(BF16)
    cw = 512
    pad = GDN_CONV // 2
    nqk = GDN_HEADS * GDN_DK
    nq = 2 * nqk + GDN_HEADS * GDN_DV
    for cc in range(nq // cw):
        cs = slice(cc * cw, (cc + 1) * cw)
        p = _dot(ax16, w_ref[:, cs])
        ext = ext_ref.at[cc]
        ext[0:SUBLANES, :] = jnp.where(prev_ok, p[0:SUBLANES], 0.0)
        ext[SUBLANES:SUBLANES + TM, :] = p[SUBLANES:SUBLANES + TM]
        ext[SUBLANES + TM:2 * SUBLANES + TM, :] = jnp.where(next_ok, p[SUBLANES + TM:], 0.0)
        acc = cw_ref[0:1, cs] * ext[pl.ds(SUBLANES - pad, TM), :]
        for tap in range(1, GDN_CONV):
            acc = acc + cw_ref[tap:tap + 1, cs] * ext[pl.ds(SUBLANES - pad + tap, TM), :]
        y = _silu(acc)
        for hh in range(cw // LANES):
            col = cc * cw + hh * LANES
            yh = y[:, hh * LANES:(hh + 1) * LANES]
            if col < 2 * nqk:
                yh = yh * lax.rsqrt(jnp.sum(yh * yh, axis=-1, keepdims=True) + RMS_EPS)
            if col < nqk:
                q_ref[:, col:col + LANES] = yh.astype(BF16)
            elif col < 2 * nqk:
                k_ref[:, col - nqk:col - nqk + LANES] = yh.astype(BF16)
            else:
                v_ref[:, col - 2 * nqk:col - 2 * nqk + LANES] = yh.astype(BF16)
    for n in range(gate_ref.shape[-1] // cw):
        gate_ref[:, n * cw:(n + 1) * cw] = _dot(a16, w_ref[:, nq + n * cw:nq + (n + 1) * cw]).astype(BF16)
    ab_ref[...] = _dot_hi(a, wab_ref[...])


def _gdn_in(h, mod, g, w_main, w_ab, conv_w):
    b, t, d = h.shape
    nt = t // TM
    nqk = GDN_HEADS * GDN_DK
    nv = GDN_HEADS * GDN_DV
    per = TM // SUBLANES
    prev = pl.BlockSpec((None, SUBLANES, d), lambda bb, j: (bb, jnp.maximum(j * per - 1, 0), 0))
    nxt = pl.BlockSpec((None, SUBLANES, d), lambda bb, j: (bb, jnp.minimum((j + 1) * per, nt * per - 1), 0))
    return pl.pallas_call(
        functools.partial(_gdn_in_kernel, nt),
        grid=(b, nt),
        in_specs=[_tok_spec(d), prev, nxt, _mod_spec(nt, d), _full_spec((1, d)),
                  _full_spec(w_main.shape), _full_spec(w_ab.shape), _full_spec(conv_w.shape)],
        out_specs=[_tok_spec(nqk), _tok_spec(nqk), _tok_spec(nv), _tok_spec(nv), _tok_spec(LANES)],
        out_shape=[jax.ShapeDtypeStruct((b, t, nqk), BF16),
                   jax.ShapeDtypeStruct((b, t, nqk), BF16),
                   jax.ShapeDtypeStruct((b, t, nv), BF16),
                   jax.ShapeDtypeStruct((b, t, nv), BF16),
                   jax.ShapeDtypeStruct((b, t, LANES), F32)],
        scratch_shapes=[pltpu.VMEM(((2 * nqk + nv) // 512, TM + 2 * SUBLANES, 512), F32)],
        compiler_params=_cparams(("parallel", "parallel")),
        name="gdn_in",
    )(h, h, h, mod, g, w_main, w_ab, conv_w)


def _split(x):
    hi = x.astype(BF16)
    return hi, (x - hi.astype(F32)).astype(BF16)


def _dot16(a, b):
    return _dot(a.astype(BF16), b.astype(BF16))


def _dot_hi(a, b):
    ah, al = _split(a)
    bh, bl = _split(b)
    return _dot(ah, bh) + (_dot(ah, bl) + _dot(al, bh))


def _gdn_scan_kernel(reverse, nb, q_ref, k_ref, v_ref, ab_ref, cf_ref, o_ref,
                     s_ref, gc_ref, gct_ref, beta_ref, u_ref, wq_ref, kd_ref, at_ref):
    nchunk = TM // GDN_CHUNK
    c64 = GDN_CHUNK
    scale = GDN_DK ** -0.5
    lane0 = 2 * GDN_HEADS if reverse else 0
    chains = [(b, h) for b in range(nb) for h in range(GDN_HEADS)]

    @pl.when(pl.program_id(0) == 0)
    def _():
        s_ref[...] = jnp.zeros_like(s_ref)

    row = lax.broadcasted_iota(jnp.int32, (TM, TM), 0)
    col = lax.broadcasted_iota(jnp.int32, (TM, TM), 1)
    same = (row // c64) == (col // c64)
    tri = (col >= row) if reverse else (col <= row)
    cum = jnp.where(jnp.logical_and(same, tri), 1.0, 0.0)
    for b in range(nb):
        ab = ab_ref[b]
        beta_ref[b] = jax.nn.sigmoid(ab)
        x = ab + cf_ref[1:2, :]
        softplus = jnp.maximum(x, 0.0) + jnp.log1p(jnp.exp(-jnp.abs(x)))
        g = -cf_ref[0:1, :] * softplus
        gc = _dot_hi(cum, g)
        gc_ref[b] = gc
        for c in range(nchunk):
            blk = gc[c * c64:(c + 1) * c64, :]
            gct_ref[b, c] = jnp.concatenate([blk, blk], axis=0).T

    ri = lax.broadcasted_iota(jnp.int32, (c64, LANES), 0)
    ln = lax.broadcasted_iota(jnp.int32, (c64, LANES), 1)
    ci = ln % c64
    left = ln < c64
    causal = (ri <= ci) if reverse else (ri >= ci)
    strict = (ri < ci) if reverse else (ri > ci)
    blk16 = (ri // 16) == (ci // 16)
    blk32 = (ri // 32) == (ci // 32)
    m_c1 = jnp.logical_and(strict, jnp.logical_and(blk32, jnp.logical_not(blk16)))[:, :c64]
    m_c2 = jnp.logical_and(strict, jnp.logical_not(blk32))[:, :c64]
    m_l0x = jnp.logical_and(jnp.logical_and(strict, blk16), jnp.logical_not(left))
    eye_l = jnp.where(jnp.logical_and(ri == ci, left), 1.0, 0.0)
    eye64 = eye_l[:, :c64]
    strict64 = strict[:, :c64]
    zpad = jnp.zeros((c64, LANES), F32)

    def prep(it, carry):
        st = []
        for b in range(nb):
            for j in range(PREP_CHUNKS):
                c = it * PREP_CHUNKS + j
                r0 = pl.multiple_of(c * c64, c64)
                rows = pl.ds(r0, c64)
                r_last = r0 if reverse else r0 + (c64 - 1)
                gcb = gc_ref[b, rows, :]
                gctb = gct_ref[b, c]
                glb = gc_ref[b, pl.ds(r_last, 1), :]
                betab = beta_ref[b, rows, :]
                for h in range(GDN_HEADS):
                    hs = slice(h * GDN_DK, (h + 1) * GDN_DK)
                    gl = lane0 + SUBLANES + h
                    k16 = k_ref[b, rows, hs]
                    k32 = k16.astype(F32)
                    beta = betab[:, lane0 + h:lane0 + h + 1]
                    gcol = gcb[:, gl:gl + 1]
                    dec2 = jnp.exp(jnp.minimum(gcol - gctb[gl:gl + 1, :], 0.0))
                    kb = k32 * beta
                    kk2 = _dot_nt(kb.astype(BF16), jnp.concatenate([k16, k16], axis=0))
                    ll2 = kk2 * dec2
                    st.append(dict(hs=hs, b=b, h=h, c=c, rows=rows, k32=k32, k16=k16, beta=beta, gcol=gcol,
                                   kb=kb, glast=glb[:, gl:gl + 1], dec=dec2[:, :c64],
                                   ll=ll2[:, :c64], x=jnp.where(m_l0x, -ll2, eye_l)))
        for _ in range(4):
            for s in st:
                x = s["x"]
                s["x"] = jnp.where(left, x, 0.0) + _dot16(x, jnp.concatenate([zpad, x], axis=0))
        for s in st:
            s["d"] = s["x"][:, :c64]
        for mask in (m_c1, m_c2):
            for s in st:
                s["t"] = _dot16(jnp.where(mask, s["ll"], 0.0), s["d"])
            for s in st:
                s["d"] = s["d"] - _dot16(s["d"], s["t"])
        for s in st:
            s["r"] = eye64 - s["d"] - _dot_hi(jnp.where(strict64, s["ll"], 0.0), s["d"])
        for s in st:
            s["d"] = s["d"] + _dot16(s["d"], s["r"])
        for s in st:
            eg = jnp.exp(s["gcol"])
            s["eg"] = eg
            v32 = v_ref[s["b"], s["rows"], s["hs"]].astype(F32)
            s["u"] = _dot_hi(s["d"], v32 * s["beta"])
            s["w"] = _dot16(s["d"], s["kb"] * eg)
        for s in st:
            b, h, hs, c, rows = s["b"], s["h"], s["hs"], s["c"], s["rows"]
            qs = q_ref[b, rows, hs].astype(F32) * scale
            attn = _dot_nt(qs.astype(BF16), s["k16"]) * jnp.where(causal[:, :c64], s["dec"], 0.0)
            u_ref[b, rows, hs] = s["u"]
            wq_ref[b, c, h, 0:c64, :] = s["w"].astype(BF16)
            wq_ref[b, c, h, c64:2 * c64, :] = (qs * s["eg"]).astype(BF16)
            kd_ref[b, rows, hs] = (s["k32"] * jnp.exp(s["glast"] - s["gcol"])).astype(BF16)
            at_ref[b, c, h] = attn.astype(BF16)
        return carry

    lax.fori_loop(0, nchunk // PREP_CHUNKS, prep, 0)

    def scan(step, carry):
        c = (nchunk - 1 - step) if reverse else step
        r0 = pl.multiple_of(c * c64, c64)
        rows = pl.ds(r0, c64)
        r_last = r0 if reverse else r0 + (c64 - 1)
        glb = [gc_ref[b, pl.ds(r_last, 1), :] for b in range(nb)]
        st = []
        for (b, h) in chains:
            s = s_ref[b, h]
            st.append(dict(b=b, h=h, hs=slice(h * GDN_DK, (h + 1) * GDN_DK), s=s,
                           ws=_dot(wq_ref[b, c, h], s.astype(BF16))))
        for d in st:
            v_new = u_ref[d["b"], rows, d["hs"]] - d["ws"][:c64]
            d["vn"] = v_new.astype(BF16)
        for d in st:
            b, h, hs = d["b"], d["h"], d["hs"]
            o = d["ws"][c64:] + _dot(at_ref[b, c, h], d["vn"])
            o_ref[b, rows, hs] = o.astype(BF16)
            egl = jnp.exp(glb[b][:, lane0 + SUBLANES + h:lane0 + SUBLANES + h + 1])
            s_ref[b, h] = d["s"] * egl + _dot_tn(kd_ref[b, rows, hs], d["vn"])
        return carry

    lax.fori_loop(0, nchunk, scan, 0)


def _gdn_scan(q, k, v, ab, cf, reverse):
    b, t, c = q.shape
    nt = t // TM
    nchunk = TM // GDN_CHUNK

    def tile(s):
        lat = (nt - 1 - s) if reverse else (s - 1)
        return jnp.where(s == 0, nt - 1, lat)

    spec = pl.BlockSpec((b, TM, c), lambda s: (0, tile(s), 0))
    return pl.pallas_call(
        functools.partial(_gdn_scan_kernel, reverse, b),
        grid=(nt,),
        in_specs=[spec, spec, spec,
                  pl.BlockSpec((b, TM, LANES), lambda s: (0, tile(s), 0)),
                  _full_spec((SUBLANES, LANES))],
        out_specs=spec,
        out_shape=jax.ShapeDtypeStruct((b, t, c), BF16),
        scratch_shapes=[pltpu.VMEM((b, GDN_HEADS, GDN_DK, GDN_DV), F32),
                        pltpu.VMEM((b, TM, LANES), F32),
                        pltpu.VMEM((b, nchunk, LANES, LANES), F32),
                        pltpu.VMEM((b, TM, LANES), F32),
                        pltpu.VMEM((b, TM, c), F32),
                        pltpu.VMEM((b, nchunk, GDN_HEADS, 2 * GDN_CHUNK, GDN_DK), BF16),
                        pltpu.VMEM((b, TM, c), BF16),
                        pltpu.VMEM((b, nchunk, GDN_HEADS, GDN_CHUNK, GDN_CHUNK), BF16)],
        compiler_params=_cparams(("arbitrary",)),
        name="gdn_scan_bwd" if reverse else "gdn_scan_fwd",
    )(q, k, v, ab, cf)


def _gdn_out_kernel(of_ref, ob_ref, gate_ref, h_ref, mod_ref, on_ref, w_ref, g_ref, out_ref, y_ref):
    for hh in range(GDN_HEADS):
        hs = slice(hh * GDN_DV, (hh + 1) * GDN_DV)
        o = of_ref[:, hs].astype(F32) + ob_ref[:, hs].astype(F32)
        gt = gate_ref[:, hs].astype(F32)
        y_ref[:, hs] = (_rms(o) * on_ref[...] * _silu(gt)).astype(BF16)
    out = _dot(y_ref[...], w_ref[...])
    out_ref[...] = _residual(h_ref[...], out, g_ref[...], mod_ref[2:3, :])


def _gdn_out(o_f, o_b, gate, h, mod, o_norm, w_out, g):
    b, t, d = h.shape
    nt = t // TM
    c = o_f.shape[-1]
    return pl.pallas_call(
        _gdn_out_kernel,
        grid=(b, nt),
        in_specs=[_tok_spec(c), _tok_spec(c), _tok_spec(c), _tok_spec(d), _mod_spec(nt, d),
                  _full_spec((1, GDN_DV)), _full_spec(w_out.shape), _full_spec((1, d))],
        out_specs=_tok_spec(d),
        out_shape=jax.ShapeDtypeStruct((b, t, d), F32),
        scratch_shapes=[pltpu.VMEM((TM, c), BF16)],
        compiler_params=_cparams(("parallel", "parallel")),
        name="gdn_out",
    )(o_f, o_b, gate, h, mod, o_norm, w_out, g)


def _conf_kernel(nt, h_ref, hp_ref, hn_ref, mod_ref, g0_ref, win_ref, bin_ref, dw_ref, dwb_ref,
                 lng_ref, lnb_ref, w_ref, b_ref, g1_ref, out_ref, ext_ref, sh_ref, y_ref):
    prev_ok, next_ok = _halo_valid(nt)
    g0, shift, scale = g0_ref[...], mod_ref[0:1, :], mod_ref[1:2, :]
    h = h_ref[...]
    ax16 = jnp.concatenate([_norm_mod(hp_ref[...], g0, shift, scale), _norm_mod(h, g0, shift, scale),
                            _norm_mod(hn_ref[...], g0, shift, scale)], axis=0).astype(BF16)
    d = h.shape[-1]
    cw = 512
    for n in range(d // cw):
        cs = slice(n * cw, (n + 1) * cw)
        gs = slice(d + n * cw, d + (n + 1) * cw)
        u = (_dot(ax16, win_ref[:, cs]) + bin_ref[:, cs]) * jax.nn.sigmoid(_dot(ax16, win_ref[:, gs]) + bin_ref[:, gs])
        ext_ref[0:HALO, cs] = jnp.where(prev_ok, u[0:HALO], 0.0)
        ext_ref[HALO:HALO + TM, cs] = u[HALO:HALO + TM]
        ext_ref[HALO + TM:2 * HALO + TM, cs] = jnp.where(next_ok, u[HALO + TM:], 0.0)
    pad = CONF_CONV // 2
    for r in range(1, SUBLANES):
        sh_ref[r - 1] = ext_ref[pl.ds(r, sh_ref.shape[1]), :]
    rw, cw = 64, 256
    for cc in range(d // cw):
        cs = slice(cc * cw, (cc + 1) * cw)
        for rr in range(TM // rw):
            acc = None
            for tap in range(CONF_CONV):
                off = HALO - pad + tap
                r = off % SUBLANES
                src = ext_ref if r == 0 else sh_ref.at[r - 1]
                term = dw_ref[tap:tap + 1, cs] * src[pl.ds(off - r + rr * rw, rw), cs]
                acc = term if acc is None else acc + term
            y_ref[rr * rw:(rr + 1) * rw, cs] = acc + dwb_ref[:, cs]
    y = y_ref[...]
    yc = y - jnp.mean(y, axis=-1, keepdims=True)
    yn = yc * lax.rsqrt(jnp.mean(yc * yc, axis=-1, keepdims=True) + LN_EPS) * lng_ref[...] + lnb_ref[...]
    out = _dot(_silu(yn).astype(BF16), w_ref[...]) + b_ref[...]
    out_ref[...] = _residual(h, out, g1_ref[...], mod_ref[2:3, :])


def _conformer(h, mod, g0, w_in, b_in, dw, dw_b, ln_g, ln_b, w_out, b_out, g1):
    b, t, d = h.shape
    nt = t // TM
    prev, nxt = _halo_specs(nt, d)
    vec = _full_spec((1, d))
    return pl.pallas_call(
        functools.partial(_conf_kernel, nt),
        grid=(b, nt),
        in_specs=[_tok_spec(d), prev, nxt, _mod_spec(nt, d), vec, _full_spec(w_in.shape), _full_spec(b_in.shape),
                  _full_spec(dw.shape), vec, vec, vec, _full_spec(w_out.shape), vec, vec],
        out_specs=_tok_spec(d),
        out_shape=jax.ShapeDtypeStruct((b, t, d), F32),
        scratch_shapes=[pltpu.VMEM((TM + 2 * HALO, d), F32),
                        pltpu.VMEM((SUBLANES - 1, TM + 2 * HALO - SUBLANES, d), F32),
                        pltpu.VMEM((TM, d), F32)],
        compiler_params=_cparams(("parallel", "parallel")),
        name="conformer",
    )(h, h, h, mod, g0, w_in, b_in, dw, dw_b, ln_g, ln_b, w_out, b_out, g1)


def _mla_proj_kernel(h_ref, mod_ref, g_ref, win_ref, qn_ref, kvn_ref, wuq_ref, wukv_ref, cs_ref,
                     q_ref, k_ref, v_ref):
    a16 = _norm_mod(h_ref[...], g_ref[...], mod_ref[0:1, :], mod_ref[1:2, :]).astype(BF16)
    p = _dot(a16, win_ref[...])
    cq = _rms(p[:, :MLA_Q_RANK]) * qn_ref[...]
    ckv = _rms(p[:, MLA_Q_RANK:MLA_Q_RANK + MLA_KV_RANK]) * kvn_ref[...]
    kr = p[:, MLA_Q_RANK + MLA_KV_RANK:]
    cs = cs_ref[...]
    kx = kr * cs
    krot = (kx + pltpu.roll(kx, MLA_ROPE, 1)).astype(BF16)
    scale = (MLA_NOPE + MLA_ROPE) ** -0.5 * LOG2E
    cq16 = cq.astype(BF16)
    ckv16 = ckv.astype(BF16)
    hw = MLA_NOPE + 2 * MLA_ROPE
    for hh in range(MLA_HEADS):
        qh = _dot(cq16, wuq_ref[:, hh * hw:(hh + 1) * hw]) * scale
        q_ref[:, hh * hw:hh * hw + MLA_NOPE] = qh[:, :MLA_NOPE].astype(BF16)
        q_ref[:, hh * hw + MLA_NOPE:(hh + 1) * hw] = (qh[:, MLA_NOPE:] * cs).astype(BF16)
        k_ref[:, hh * hw:hh * hw + MLA_NOPE] = _dot(
            ckv16, wukv_ref[:, hh * MLA_NOPE:(hh + 1) * MLA_NOPE]).astype(BF16)
        k_ref[:, hh * hw + MLA_NOPE:(hh + 1) * hw] = krot
        nk = MLA_HEADS * MLA_NOPE
        v_ref[:, 2 * hh * MLA_V:(2 * hh + 1) * MLA_V] = _dot(
            ckv16, wukv_ref[:, nk + hh * MLA_V:nk + (hh + 1) * MLA_V]).astype(BF16)
        v_ref[:, (2 * hh + 1) * MLA_V:(2 * hh + 2) * MLA_V] = jnp.ones((TM, MLA_V), BF16)


def _mla_proj(h, mod, g, w_in, q_norm, kv_norm, w_uq, w_ukv, cs_tab):
    b, t, d = h.shape
    nt = t // TM
    hw = MLA_NOPE + 2 * MLA_ROPE
    return pl.pallas_call(
        _mla_proj_kernel,
        grid=(b, nt),
        in_specs=[_tok_spec(d), _mod_spec(nt, d), _full_spec((1, d)), _full_spec(w_in.shape),
                  _full_spec(q_norm.shape), _full_spec(kv_norm.shape),
                  _full_spec(w_uq.shape), _full_spec(w_ukv.shape),
                  pl.BlockSpec((TM, LANES), lambda bb, j: (j, 0))],
        out_specs=[_tok_spec(MLA_HEADS * hw), _tok_spec(MLA_HEADS * hw), _tok_spec(MLA_HEADS * 2 * MLA_V)],
        out_shape=[jax.ShapeDtypeStruct((b, t, MLA_HEADS * hw), BF16),
                   jax.ShapeDtypeStruct((b, t, MLA_HEADS * hw), BF16),
                   jax.ShapeDtypeStruct((b, t, MLA_HEADS * 2 * MLA_V), BF16)],
        compiler_params=_cparams(("parallel", "parallel")),
        name="mla_proj",
    )(h, mod, g, w_in, q_norm, kv_norm, w_uq, w_ukv, cs_tab)


def _attn_kernel(tk, q_ref, k_ref, v_ref, o_ref, m_ref, acc_ref):
    q = q_ref[...]
    for i in range(k_ref.shape[0] // tk):
        rows = slice(i * tk, (i + 1) * tk)
        s = _dot_nt(q, k_ref[rows, :])
        m_blk = jnp.max(s, axis=-1, keepdims=True)
        if i == 0:
            m_new = m_blk
            acc_ref[...] = _dot(jnp.exp2(s - m_new).astype(BF16), v_ref[rows, :])
        else:
            m_old = m_ref[...]
            m_new = jnp.maximum(m_old, m_blk)
            acc_ref[...] = jnp.exp2(m_old - m_new) * acc_ref[...] + _dot(
                jnp.exp2(s - m_new).astype(BF16), v_ref[rows, :])
        m_ref[...] = m_new
    acc = acc_ref[...]
    o_ref[...] = (acc[:, :MLA_V] / acc[:, MLA_V:]).astype(BF16)


def _attention(q, k, v, tq, q_rows, q_row0, k_rows, k_row0, name):
    b = q.shape[0]
    hw = MLA_NOPE + 2 * MLA_ROPE
    tk = next(c for c in (768, 512, 256) if k_rows % c == 0)
    q0 = q_row0 // tq
    k0 = k_row0 // k_rows
    return pl.pallas_call(
        functools.partial(_attn_kernel, tk),
        grid=(b, MLA_HEADS, q_rows // tq),
        in_specs=[pl.BlockSpec((None, tq, hw), lambda bb, hh, i: (bb, q0 + i, hh)),
                  pl.BlockSpec((None, k_rows, hw), lambda bb, hh, i: (bb, k0, hh)),
                  pl.BlockSpec((None, k_rows, 2 * MLA_V), lambda bb, hh, i: (bb, k0, hh))],
        out_specs=pl.BlockSpec((None, tq, MLA_V), lambda bb, hh, i: (bb, i, hh)),
        out_shape=jax.ShapeDtypeStruct((b, q_rows, MLA_HEADS * MLA_V), BF16),
        scratch_shapes=[pltpu.VMEM((tq, 1), F32), pltpu.VMEM((tq, 2 * MLA_V), F32)],
        compiler_params=_cparams(("parallel", "parallel", "arbitrary")),
        name=name,
    )(q, k, v)


def _proj_out_kernel(o_ref, w_ref, h_ref, mod_ref, g_ref, out_ref):
    out = _dot(o_ref[...], w_ref[...])
    out_ref[...] = _residual(h_ref[...], out, g_ref[...], mod_ref[2:3, :])


def _proj_out(o, w, h, mod, g):
    b, t, d = h.shape
    nt = t // TM
    return pl.pallas_call(
        _proj_out_kernel,
        grid=(b, nt),
        in_specs=[_tok_spec(o.shape[-1]), _full_spec(w.shape), _tok_spec(d), _mod_spec(nt, d),
                  _full_spec((1, d))],
        out_specs=_tok_spec(d),
        out_shape=jax.ShapeDtypeStruct((b, t, d), F32),
        compiler_params=_cparams(("parallel", "parallel")),
        name="mla_out",
    )(o, w, h, mod, g)


def _ffn_kernel(fc, h_ref, mod_ref, g2_ref, wg_ref, wu_ref, wd_ref, g3_ref, out_ref, acc_ref):
    h = h_ref[...]
    a16 = _norm_mod(h, g2_ref[...], mod_ref[3:4, :], mod_ref[4:5, :]).astype(BF16)
    for c in range(wg_ref.shape[-1] // fc):
        fs = slice(c * fc, (c + 1) * fc)
        act = (_silu(_dot(a16, wg_ref[:, fs])) * _dot(a16, wu_ref[:, fs])).astype(BF16)
        part = _dot(act, wd_ref[fs, :])
        if c == 0:
            acc_ref[...] = part
        else:
            acc_ref[...] += part
    out_ref[...] = _residual(h, acc_ref[...], g3_ref[...], mod_ref[5:6, :])


def _ffn(h, mod, g2, w_gate, w_up, w_down, g3, n_tiles):
    b, t, d = h.shape
    nt = t // TM
    return pl.pallas_call(
        functools.partial(_ffn_kernel, 1408),
        grid=(b, n_tiles),
        in_specs=[_tok_spec(d), _mod_spec(nt, d), _full_spec((1, d)), _full_spec(w_gate.shape),
                  _full_spec(w_up.shape), _full_spec(w_down.shape), _full_spec((1, d))],
        out_specs=_tok_spec(d),
        out_shape=jax.ShapeDtypeStruct((b, n_tiles * TM, d), F32),
        scratch_shapes=[pltpu.VMEM((TM, d), F32)],
        compiler_params=_cparams(("parallel", "parallel")),
        name="ffn",
    )(h, mod, g2, w_gate, w_up, w_down, g3)


def _moe_route_kernel(h_ref, mod_ref, g2_ref, wr_ref, a_ref, rt_ref):
    lane = lax.broadcasted_iota(jnp.int32, (TM, LANES), 1).astype(F32)
    a = _norm_mod(h_ref[...], g2_ref[...], mod_ref[3:4, :], mod_ref[4:5, :])
    for s in range(SUBLANES):
        a_ref[pl.ds(s, TM, stride=SUBLANES), :] = a[:, s * LANES:(s + 1) * LANES]
    logits = jnp.where(lane < N_EXPERTS, _dot_hi(a, wr_ref[...]), -jnp.inf)
    m1 = jnp.max(logits, axis=-1, keepdims=True)
    i1 = jnp.min(jnp.where(logits == m1, lane, float(LANES)), axis=-1, keepdims=True)
    rest = jnp.where(lane == i1, -jnp.inf, logits)
    m2 = jnp.max(rest, axis=-1, keepdims=True)
    i2 = jnp.min(jnp.where(rest == m2, lane, float(LANES)), axis=-1, keepdims=True)
    e2 = jnp.exp(m2 - m1)
    w1 = 1.0 / (1.0 + e2)
    rt_ref[...] = (jnp.where(lane == 0.0, i1, 0.0) + jnp.where(lane == 1.0, i2, 0.0)
                   + jnp.where(lane == 2.0, w1, 0.0) + jnp.where(lane == 3.0, e2 * w1, 0.0))


def _moe_route(h, mod, g2, w_router, n_tiles):
    b, t, d = h.shape
    nt = t // TM
    return pl.pallas_call(
        _moe_route_kernel,
        grid=(b, n_tiles),
        in_specs=[_tok_spec(d), _mod_spec(nt, d), _full_spec((1, d)), _full_spec(w_router.shape)],
        out_specs=[pl.BlockSpec((TM * SUBLANES, LANES), lambda bb, j: (bb * n_tiles + j, 0)),
                   _tok_spec(LANES)],
        out_shape=[jax.ShapeDtypeStruct((b * n_tiles * TM * SUBLANES, LANES), F32),
                   jax.ShapeDtypeStruct((b, n_tiles * TM, LANES), F32)],
        compiler_params=_cparams(("parallel", "parallel")),
        name="moe_route",
    )(h, mod, g2, w_router)


def _gather_tiles(idx_ref, base, n, src_hbm, dst, sem):
    def body(r, carry):
        row = pl.multiple_of(idx_ref[base + r], SUBLANES)
        pltpu.make_async_copy(src_hbm.at[pl.ds(row, SUBLANES), :],
                              dst.at[pl.ds(pl.multiple_of(r * SUBLANES, SUBLANES), SUBLANES), :], sem).start()
        return carry
    lax.fori_loop(0, n, body, 0, unroll=8)


def _wait_tiles(n, src_hbm, dst, sem):
    pltpu.make_async_copy(src_hbm.at[pl.ds(0, n * SUBLANES), :], dst, sem).wait()


def _untile(buf, n):
    return jnp.concatenate([buf[pl.ds(s, n, stride=SUBLANES), :] for s in range(SUBLANES)], axis=1)


def _moe_dispatch_kernel(dst_ref, last_ref, nact_ref, a_ref, x_hbm, stage_ref, sem, zsem):
    i = pl.program_id(0)
    n = pl.num_programs(0)
    rows = TM * SUBLANES
    slot = i % 2
    n_tiles_max = x_hbm.shape[0] // rows

    @pl.when(i == 0)
    def _():
        zero = stage_ref.at[1]
        zero[...] = jnp.zeros_like(zero)

        def zero_tile(t):
            return pltpu.make_async_copy(zero, x_hbm.at[pl.ds(pl.multiple_of(t * rows, rows), rows), :], zsem)

        for e in range(N_EXPERTS):
            zero_tile(last_ref[e]).start()
            pl.when(nact_ref[0] + e < n_tiles_max)(lambda e=e: zero_tile(nact_ref[0] + e).start())
        for e in range(N_EXPERTS):
            zero_tile(0).wait()
            pl.when(nact_ref[0] + e < n_tiles_max)(lambda: zero_tile(0).wait())

    stage = stage_ref.at[slot]
    stage[...] = a_ref[...]

    def body(r, carry):
        src = pl.multiple_of(r * SUBLANES, SUBLANES)
        for k in range(2):
            dst = pl.multiple_of(dst_ref[(i * TM + r) * 2 + k], SUBLANES)
            pltpu.make_async_copy(stage.at[pl.ds(src, SUBLANES), :], x_hbm.at[pl.ds(dst, SUBLANES), :],
                                  sem.at[slot]).start()
        return carry
    lax.fori_loop(0, TM, body, 0, unroll=4)

    def wait(sl):
        for k in range(2):
            pltpu.make_async_copy(stage_ref.at[sl], x_hbm.at[pl.ds(0, rows), :], sem.at[sl]).wait()

    @pl.when(i > 0)
    def _():
        wait(1 - slot)

    @pl.when(i == n - 1)
    def _():
        wait(slot)


def _moe_dispatch(dst_rows, last_tile, n_active, a2d, n_rows):
    n_steps = a2d.shape[0] // (TM * SUBLANES)
    return pl.pallas_call(
        _moe_dispatch_kernel,
        grid_spec=pltpu.PrefetchScalarGridSpec(
            num_scalar_prefetch=3,
            grid=(n_steps,),
            in_specs=[pl.BlockSpec((TM * SUBLANES, LANES), lambda i, ds, lt, na: (i, 0))],
            out_specs=pl.BlockSpec(memory_space=pl.ANY),
            scratch_shapes=[pltpu.VMEM((2, TM * SUBLANES, LANES), F32), pltpu.SemaphoreType.DMA((2,)),
                            pltpu.SemaphoreType.DMA(())]),
        out_shape=jax.ShapeDtypeStruct((n_rows * SUBLANES, LANES), F32),
        compiler_params=_cparams(("arbitrary",)),
        name="moe_dispatch",
    )(dst_rows, last_tile, n_active, a2d)


def _moe_experts_kernel(te_ref, nact_ref, x_ref, wg_ref, wu_ref, wd_ref, y_ref):
    i = pl.program_id(0)

    @pl.when(i < nact_ref[0])
    def _():
        x16 = _untile(x_ref, TM).astype(BF16)
        f = wg_ref.shape[-1]
        acc = None
        for c0 in range(0, f, 512):
            fs = slice(c0, min(c0 + 512, f))
            act = (_silu(_dot(x16, wg_ref[:, fs])) * _dot(x16, wu_ref[:, fs])).astype(BF16)
            part = _dot(act, wd_ref[fs, :])
            acc = part if acc is None else acc + part
        for s in range(SUBLANES):
            y_ref[pl.ds(s, TM, stride=SUBLANES), :] = acc[:, s * LANES:(s + 1) * LANES]

    @pl.when(i >= nact_ref[0])
    def _():
        y_ref[...] = jnp.zeros_like(y_ref)


def _moe_experts(tile_expert, n_active, x2d, w_gate, w_up, w_down):
    n_tiles = tile_expert.shape[0]
    _, d, f = w_gate.shape
    wspec_in = pl.BlockSpec((None, d, f), lambda i, te, na: (te[i], 0, 0))
    wspec_out = pl.BlockSpec((None, f, d), lambda i, te, na: (te[i], 0, 0))
    return pl.pallas_call(
        _moe_experts_kernel,
        grid_spec=pltpu.PrefetchScalarGridSpec(
            num_scalar_prefetch=2,
            grid=(n_tiles,),
            in_specs=[pl.BlockSpec((TM * SUBLANES, LANES), lambda i, te, na: (jnp.minimum(i, na[0] - 1), 0)),
                      wspec_in, wspec_in, wspec_out],
            out_specs=pl.BlockSpec((TM * SUBLANES, LANES), lambda i, te, na: (i, 0))),
        out_shape=jax.ShapeDtypeStruct((n_tiles * TM * SUBLANES, LANES), F32),
        compiler_params=_cparams(("arbitrary",)),
        name="moe_experts",
    )(tile_expert, n_active, x2d, w_gate, w_up, w_down)


def _moe_combine_kernel(n_tiles, dst_ref, y_hbm, rt_ref, h_ref, mod_ref, g3_ref, out_ref, ybuf, sem):
    step = pl.program_id(0) * n_tiles + pl.program_id(1)
    nsteps = pl.num_programs(0) * n_tiles

    @pl.when(step == 0)
    def _():
        _gather_tiles(dst_ref, 0, 2 * TM, y_hbm, ybuf.at[0], sem.at[0])

    @pl.when(step + 1 < nsteps)
    def _():
        nxt = (step + 1) % 2
        _gather_tiles(dst_ref, (step + 1) * 2 * TM, 2 * TM, y_hbm, ybuf.at[nxt], sem.at[nxt])

    cur = step % 2
    _wait_tiles(2 * TM, y_hbm, ybuf.at[cur], sem.at[cur])
    yb = ybuf.at[cur]
    w1 = rt_ref[:, 2:3]
    w2 = rt_ref[:, 3:4]
    f = jnp.concatenate(
        [w1 * yb[pl.ds(s, TM, stride=2 * SUBLANES), :] + w2 * yb[pl.ds(SUBLANES + s, TM, stride=2 * SUBLANES), :]
         for s in range(SUBLANES)], axis=1)
    out_ref[...] = _residual(h_ref[...], f, g3_ref[...], mod_ref[5:6, :])


def _moe_combine(dst_rows, y2d, rt, h, mod, g3, n_tiles):
    b, t, d = h.shape
    nt = t // TM
    tok = lambda c: pl.BlockSpec((None, TM, c), lambda bb, j, ds: (bb, j, 0))
    return pl.pallas_call(
        functools.partial(_moe_combine_kernel, n_tiles),
        grid_spec=pltpu.PrefetchScalarGridSpec(
            num_scalar_prefetch=1,
            grid=(b, n_tiles),
            in_specs=[pl.BlockSpec(memory_space=pl.ANY), tok(LANES), tok(d),
                      pl.BlockSpec((None, SUBLANES, d),
                                   lambda bb, j, ds: (jnp.where(j == nt - 1, 0, bb + 1), 0, 0)),
                      pl.BlockSpec((1, d), lambda bb, j, ds: (0, 0))],
            out_specs=tok(d),
            scratch_shapes=[pltpu.VMEM((2, 2 * TM * SUBLANES, LANES), F32), pltpu.SemaphoreType.DMA((2,))]),
        out_shape=jax.ShapeDtypeStruct((b, n_tiles * TM, d), F32),
        compiler_params=_cparams(("arbitrary", "arbitrary")),
        name="moe_combine",
    )(dst_rows, y2d, rt, h, mod, g3)


def _moe(h, mod, g2, w_router, w_gate, w_up, w_down, g3, n_tiles):
    b = h.shape[0]
    n_tok = b * n_tiles * TM
    a2d, rt = _moe_route(h, mod, g2, w_router, n_tiles)
    expert = rt.reshape(n_tok, LANES)[:, :2].astype(jnp.int32).reshape(-1)
    onehot = (expert[:, None] == jnp.arange(N_EXPERTS)[None, :]).astype(jnp.int32)
    rank = jnp.sum((jnp.cumsum(onehot, axis=0) - onehot) * onehot, axis=1)
    count = jnp.sum(onehot, axis=0)
    padded = (count + TM - 1) // TM * TM
    ends = jnp.cumsum(padded)
    dest = (ends - padded)[expert] + rank
    n_tiles_max = 2 * n_tok // TM + N_EXPERTS
    tile_start = jnp.arange(n_tiles_max, dtype=jnp.int32) * TM
    tile_expert = jnp.minimum(jnp.sum(ends[None, :] <= tile_start[:, None], axis=1), N_EXPERTS - 1)
    n_active = (ends[-1] // TM).reshape(1).astype(jnp.int32)
    last_tile = jnp.maximum(ends // TM - 1, 0).astype(jnp.int32)
    dst_rows = (dest * SUBLANES).astype(jnp.int32)
    x2d = _moe_dispatch(dst_rows, last_tile, n_active, a2d, n_tiles_max * TM)
    y2d = _moe_experts(tile_expert.astype(jnp.int32), n_active, x2d, w_gate, w_up, w_down)
    return _moe_combine(dst_rows, y2d, rt, h, mod, g3, n_tiles)


def _pad_rows(a, rows):
    return jnp.pad(a, ((0, rows - a.shape[0]),) + ((0, 0),) * (a.ndim - 1))


def _rope_table(seq, ctx_len):
    n_freq = MLA_ROPE // 4
    t = jnp.arange(seq)
    pos = jnp.stack([(t // GRID_W).astype(F32), (t % GRID_W).astype(F32)], axis=1)
    inv = ROPE_THETA ** (-jnp.arange(n_freq, dtype=F32) / n_freq)
    ang = pos[:, :, None] * inv
    cos = jnp.repeat(jnp.cos(ang)[:, :, None, :], 2, axis=2).reshape(seq, MLA_ROPE)
    sin = jnp.stack([-jnp.sin(ang), jnp.sin(ang)], axis=2).reshape(seq, MLA_ROPE)
    lat = jnp.concatenate([cos, sin], axis=1)
    ctx = jnp.concatenate([jnp.ones((ctx_len, MLA_ROPE), F32), jnp.zeros((ctx_len, MLA_ROPE), F32)], axis=1)
    return jnp.concatenate([lat, ctx], axis=0)


def _swap_halves(w):
    shp = w.shape
    return jnp.flip(w.reshape(shp[:-1] + (2, 2, MLA_ROPE // 4)), axis=-2).reshape(shp)


def kernel(x, c, ctx, c_ctx, w_mod, b_mod, norm_g, gdn_w_in, gdn_conv, gdn_a_log, gdn_dt_bias, gdn_o_norm, gdn_w_out, conf_w_in, conf_b_in, conf_dw, conf_dw_b, conf_ln_g, conf_ln_b, conf_w_out, conf_b_out, mla_w_in, mla_q_norm, mla_kv_norm, mla_w_uq, mla_w_ukv, mla_w_o, ffn_w_gate, ffn_w_up, ffn_w_down, moe_router, moe_w_gate, moe_w_up, moe_w_down):
    bsz, seq, d = x.shape
    ctx_len = ctx.shape[1]
    depth = w_mod.shape[0]
    assert ctx_len == TM and seq % (4 * TM) == 0 and bsz + 1 <= SUBLANES
    nt = (seq + ctx_len) // TM
    n_lat = seq // TM

    cvec = _pad_rows(jnp.concatenate([c_ctx[None, :], c], axis=0), SUBLANES)
    mods = _adaln(cvec, w_mod, b_mod)
    mods = mods.reshape(depth, SUBLANES, N_MOD, d)[:, :bsz + 1]
    mods = jnp.pad(mods, ((0, 0), (0, 0), (0, SUBLANES - N_MOD), (0, 0)))

    h = jnp.concatenate([x, ctx], axis=1)
    rope_cs = _rope_table(seq, ctx_len)

    n_mix = 3
    for i in range(depth):
        last = i == depth - 1
        mod = mods[i]
        g = norm_g[i].reshape(4, 1, d)
        kind, j = i % n_mix, i // n_mix
        if kind == 0:
            nqk = GDN_HEADS * GDN_DK
            nmain = 2 * nqk + 2 * GDN_HEADS * GDN_DV
            w_in = gdn_w_in[j]
            w_main = w_in[:, :nmain].astype(BF16)
            w_tail = w_in[:, nmain:]
            nh = GDN_HEADS
            w_ab = jnp.concatenate([w_tail[:, (2 * part + dr) * nh:(2 * part + dr + 1) * nh]
                                    for dr in range(2) for part in range(2)], axis=1)
            w_ab = jnp.pad(w_ab, ((0, 0), (0, LANES - 4 * nh)))

            def decay_lanes(p):
                z = jnp.zeros((nh,), F32)
                return jnp.pad(jnp.concatenate([z, p[0], z, p[1]]), (0, LANES - 4 * nh))

            cf = _pad_rows(jnp.stack([decay_lanes(jnp.exp(gdn_a_log[j])), decay_lanes(gdn_dt_bias[j])]),
                           SUBLANES)
            q, k, v, gate, ab = _gdn_in(h, mod, g[0], w_main, w_ab, _pad_rows(gdn_conv[j], SUBLANES))
            o_f = _gdn_scan(q, k, v, ab, cf, reverse=False)
            o_b = _gdn_scan(q, k, v, ab, cf, reverse=True)
            h = _gdn_out(o_f, o_b, gate, h, mod, gdn_o_norm[j].reshape(1, -1),
                         gdn_w_out[j].astype(BF16), g[1])
        elif kind == 1:
            h = _conformer(h, mod, g[0], conf_w_in[j].astype(BF16), conf_b_in[j].reshape(1, -1),
                           _pad_rows(conf_dw[j], 32), conf_dw_b[j].reshape(1, -1),
                           conf_ln_g[j].reshape(1, -1), conf_ln_b[j].reshape(1, -1),
                           conf_w_out[j].astype(BF16), conf_b_out[j].reshape(1, -1), g[1])
        else:
            w_in = mla_w_in[j]
            w_kr = w_in[:, MLA_Q_RANK + MLA_KV_RANK:]
            w_in2 = jnp.concatenate([w_in, _swap_halves(w_kr)], axis=1).astype(BF16)
            hq = MLA_NOPE + MLA_ROPE
            w_uq = mla_w_uq[j].reshape(MLA_Q_RANK, MLA_HEADS, hq)
            w_uq2 = jnp.concatenate([w_uq, _swap_halves(w_uq[..., MLA_NOPE:])], axis=-1)
            w_uq2 = w_uq2.reshape(MLA_Q_RANK, -1).astype(BF16)
            w_ukv = mla_w_ukv[j].reshape(MLA_KV_RANK, MLA_HEADS, MLA_NOPE + MLA_V)
            w_ukv2 = jnp.concatenate([w_ukv[..., :MLA_NOPE].reshape(MLA_KV_RANK, -1),
                                      w_ukv[..., MLA_NOPE:].reshape(MLA_KV_RANK, -1)], axis=1).astype(BF16)
            qa, ka, va = _mla_proj(h, mod, g[0], w_in2, mla_q_norm[j].reshape(1, -1),
                                   mla_kv_norm[j].reshape(1, -1), w_uq2, w_ukv2, rope_cs)
            o_l = _attention(qa, ka, va, 4 * TM, seq, 0, seq + ctx_len, 0, "attn_latent")
            o_c = _attention(qa, ka, va, TM, ctx_len, seq, ctx_len, seq, "attn_context")
            o = jnp.concatenate([o_l, o_c], axis=1)
            h = _proj_out(o, mla_w_o[j].astype(BF16), h, mod, g[1])

        n_tiles = n_lat if last else nt
        f = i // 2
        if i % 2 == 0:
            h = _ffn(h, mod, g[2], ffn_w_gate[f].astype(BF16), ffn_w_up[f].astype(BF16),
                     ffn_w_down[f].astype(BF16), g[3], n_tiles)
        else:
            w_r = jnp.pad(moe_router[f], ((0, 0), (0, LANES - N_EXPERTS)))
            h = _moe(h, mod, g[2], w_r, moe_w_gate[f].astype(BF16), moe_w_up[f].astype(BF16),
                     moe_w_down[f].astype(BF16), g[3], n_tiles)
    return h
```

```python
import functools

import jax
import jax.numpy as jnp
from jax import lax
from jax.experimental import pallas as pl
from jax.experimental.pallas import tpu as pltpu

F32 = jnp.float32
BF16 = jnp.bfloat16

N_MOD = 6
RMS_EPS = 1e-6
LN_EPS = 1e-5
GRID_W = 64
GDN_HEADS = 8
GDN_DK = 128
GDN_DV = 128
GDN_CONV = 5
GDN_CHUNK = 64
CONF_CONV = 31
MLA_HEADS = 8
MLA_NOPE = 128
MLA_ROPE = 64
MLA_V = 128
MLA_Q_RANK = 512
MLA_KV_RANK = 256
ROPE_THETA = 10000.0
LOG2E = 1.4426950408889634
N_EXPERTS = 8

LANES = 128
SUBLANES = 8
TM = 256
HALO = 16
PREP_CHUNKS = 2
VMEM_LIMIT = 56 * 1024 * 1024


def _cparams(sem):
    return pltpu.CompilerParams(dimension_semantics=sem, vmem_limit_bytes=VMEM_LIMIT)


def _dot(a, b, **kw):
    return jnp.dot(a, b, preferred_element_type=F32, **kw)


def _dot_nt(a, b):
    return lax.dot_general(a, b, (((1,), (1,)), ((), ())), preferred_element_type=F32)


def _dot_tn(a, b):
    return lax.dot_general(a, b, (((0,), (0,)), ((), ())), preferred_element_type=F32)


def _silu(x):
    return x * jax.nn.sigmoid(x)


def _rms(x, eps=RMS_EPS):
    return x * lax.rsqrt(jnp.mean(x * x, axis=-1, keepdims=True) + eps)


def _norm_mod(h, g, shift, scale):
    return _rms(h) * g * (1.0 + scale) + shift


def _residual(h, o, g, gate):
    return h + gate * (_rms(o) * g)


def _tok_spec(c):
    return pl.BlockSpec((None, TM, c), lambda b, j: (b, j, 0))


def _full_spec(shape):
    nd = len(shape)
    return pl.BlockSpec(shape, lambda *_: (0,) * nd)


def _mod_spec(nt, d):
    return pl.BlockSpec((None, SUBLANES, d), lambda b, j: (jnp.where(j == nt - 1, 0, b + 1), 0, 0))


def _halo_specs(nt, c):
    per = TM // HALO
    prev = pl.BlockSpec((None, HALO, c), lambda b, j: (b, jnp.maximum(j * per - 1, 0), 0))
    nxt = pl.BlockSpec((None, HALO, c), lambda b, j: (b, jnp.minimum((j + 1) * per, nt * per - 1), 0))
    return prev, nxt


def _halo_valid(nt):
    j = pl.program_id(1)
    prev_ok = jnp.logical_and(j != 0, j != nt - 1)
    next_ok = j < nt - 2
    return prev_ok, next_ok


def _adaln_kernel(c_ref, w_ref, b_ref, o_ref):
    o_ref[...] = _dot_hi(_silu(c_ref[...]), w_ref[...]) + b_ref[...]


def _adaln(cvec, w_mod, b_mod):
    depth, d, n = w_mod.shape
    tn = 1536
    return pl.pallas_call(
        _adaln_kernel,
        grid=(depth, n // tn),
        in_specs=[pl.BlockSpec((SUBLANES, d), lambda i, k: (0, 0)),
                  pl.BlockSpec((None, d, tn), lambda i, k: (i, 0, k)),
                  pl.BlockSpec((None, 1, tn), lambda i, k: (i, 0, k))],
        out_specs=pl.BlockSpec((None, SUBLANES, tn), lambda i, k: (i, 0, k)),
        out_shape=jax.ShapeDtypeStruct((depth, SUBLANES, n), F32),
        compiler_params=_cparams(("parallel", "parallel")),
        name="adaln",
    )(cvec, w_mod, b_mod.reshape(depth, 1, n))


def _gdn_in_kernel(nt, h_ref, hp_ref, hn_ref, mod_ref, g_ref, w_ref, wab_ref, cw_ref,
                   q_ref, k_ref, v_ref, gate_ref, ab_ref, ext_ref):
    prev_ok, next_ok = _halo_valid(nt)
    g, shift, scale = g_ref[...], mod_ref[0:1, :], mod_ref[1:2, :]
    a = _norm_mod(h_ref[...], g, shift, scale)
    a16 = a.astype(BF16)
    ax16 = jnp.concatenate([_norm_mod(hp_ref[...], g, shift, scale), a,
                            _norm_mod(hn_ref[...], g, shift, scale)], axis=0).astype(BF16)
    cw = 512
    pad = GDN_CONV // 2
    nqk = GDN_HEADS * GDN_DK
    nq = 2 * nqk + GDN_HEADS * GDN_DV
    for cc in range(nq // cw):
        cs = slice(cc * cw, (cc + 1) * cw)
        p = _dot(ax16, w_ref[:, cs])
        ext = ext_ref.at[cc]
        ext[0:SUBLANES, :] = jnp.where(prev_ok, p[0:SUBLANES], 0.0)
        ext[SUBLANES:SUBLANES + TM, :] = p[SUBLANES:SUBLANES + TM]
        ext[SUBLANES + TM:2 * SUBLANES + TM, :] = jnp.where(next_ok, p[SUBLANES + TM:], 0.0)
        acc = cw_ref[0:1, cs] * ext[pl.ds(SUBLANES - pad, TM), :]
        for tap in range(1, GDN_CONV):
            acc = acc + cw_ref[tap:tap + 1, cs] * ext[pl.ds(SUBLANES - pad + tap, TM), :]
        y = _silu(acc)
        for hh in range(cw // LANES):
            col = cc * cw + hh * LANES
            yh = y[:, hh * LANES:(hh + 1) * LANES]
            if col < 2 * nqk:
                yh = yh * lax.rsqrt(jnp.sum(yh * yh, axis=-1, keepdims=True) + RMS_EPS)
            if col < nqk:
                q_ref[:, col:col + LANES] = yh.astype(BF16)
            elif col < 2 * nqk:
                k_ref[:, col - nqk:col - nqk + LANES] = yh.astype(BF16)
            else:
                v_ref[:, col - 2 * nqk:col - 2 * nqk + LANES] = yh.astype(BF16)
    for n in range(gate_ref.shape[-1] // cw):
        gate_ref[:, n * cw:(n + 1) * cw] = _dot(a16, w_ref[:, nq + n * cw:nq + (n + 1) * cw]).astype(BF16)
    ab_ref[...] = _dot_hi(a, wab_ref[...])


def _gdn_in(h, mod, g, w_main, w_ab, conv_w):
    b, t, d = h.shape
    nt = t // TM
    nqk = GDN_HEADS * GDN_DK
    nv = GDN_HEADS * GDN_DV
    per = TM // SUBLANES
    prev = pl.BlockSpec((None, SUBLANES, d), lambda bb, j: (bb, jnp.maximum(j * per - 1, 0), 0))
    nxt = pl.BlockSpec((None, SUBLANES, d), lambda bb, j: (bb, jnp.minimum((j + 1) * per, nt * per - 1), 0))
    return pl.pallas_call(
        functools.partial(_gdn_in_kernel, nt),
        grid=(b, nt),
        in_specs=[_tok_spec(d), prev, nxt, _mod_spec(nt, d), _full_spec((1, d)),
                  _full_spec(w_main.shape), _full_spec(w_ab.shape), _full_spec(conv_w.shape)],
        out_specs=[_tok_spec(nqk), _tok_spec(nqk), _tok_spec(nv), _tok_spec(nv), _tok_spec(LANES)],
        out_shape=[jax.ShapeDtypeStruct((b, t, nqk), BF16),
                   jax.ShapeDtypeStruct((b, t, nqk), BF16),
                   jax.ShapeDtypeStruct((b, t, nv), BF16),
                   jax.ShapeDtypeStruct((b, t, nv), BF16),
                   jax.ShapeDtypeStruct((b, t, LANES), F32)],
        scratch_shapes=[pltpu.VMEM(((2 * nqk + nv) // 512, TM + 2 * SUBLANES, 512), F32)],
        compiler_params=_cparams(("parallel", "parallel")),
        name="gdn_in",
    )(h, h, h, mod, g, w_main, w_ab, conv_w)


def _split(x):
    hi = x.astype(BF16)
    return hi, (x - hi.astype(F32)).astype(BF16)


def _dot16(a, b):
    return _dot(a.astype(BF16), b.astype(BF16))


def _dot_hi(a, b):
    ah, al = _split(a)
    bh, bl = _split(b)
    return _dot(ah, bh) + (_dot(ah, bl) + _dot(al, bh))


def _gdn_scan_kernel(reverse, nb, q_ref, k_ref, v_ref, ab_ref, cf_ref, o_ref,
                     s_ref, gc_ref, gct_ref, beta_ref, u_ref, wq_ref, kd_ref, at_ref):
    nchunk = TM // GDN_CHUNK
    c64 = GDN_CHUNK
    scale = GDN_DK ** -0.5
    lane0 = 2 * GDN_HEADS if reverse else 0
    chains = [(b, h) for b in range(nb) for h in range(GDN_HEADS)]

    @pl.when(pl.program_id(0) == 0)
    def _():
        s_ref[...] = jnp.zeros_like(s_ref)

    row = lax.broadcasted_iota(jnp.int32, (TM, TM), 0)
    col = lax.broadcasted_iota(jnp.int32, (TM, TM), 1)
    same = (row // c64) == (col // c64)
    tri = (col >= row) if reverse else (col <= row)
    cum = jnp.where(jnp.logical_and(same, tri), 1.0, 0.0)
    for b in range(nb):
        ab = ab_ref[b]
        beta_ref[b] = jax.nn.sigmoid(ab)
        x = ab + cf_ref[1:2, :]
        softplus = jnp.maximum(x, 0.0) + jnp.log1p(jnp.exp(-jnp.abs(x)))
        g = -cf_ref[0:1, :] * softplus
        gc = _dot_hi(cum, g)
        gc_ref[b] = gc
        for c in range(nchunk):
            blk = gc[c * c64:(c + 1) * c64, :]
            gct_ref[b, c] = jnp.concatenate([blk, blk], axis=0).T

    ri = lax.broadcasted_iota(jnp.int32, (c64, LANES), 0)
    ln = lax.broadcasted_iota(jnp.int32, (c64, LANES), 1)
    ci = ln % c64
    left = ln < c64
    causal = (ri <= ci) if reverse else (ri >= ci)
    strict = (ri < ci) if reverse else (ri > ci)
    blk16 = (ri // 16) == (ci // 16)
    blk32 = (ri // 32) == (ci // 32)
    m_c1 = jnp.logical_and(strict, jnp.logical_and(blk32, jnp.logical_not(blk16)))[:, :c64]
    m_c2 = jnp.logical_and(strict, jnp.logical_not(blk32))[:, :c64]
    m_l0x = jnp.logical_and(jnp.logical_and(strict, blk16), jnp.logical_not(left))
    eye_l = jnp.where(jnp.logical_and(ri == ci, left), 1.0, 0.0)
    eye64 = eye_l[:, :c64]
    strict64 = strict[:, :c64]
    zpad = jnp.zeros((c64, LANES), F32)

    def prep(it, carry):
        st = []
        for b in range(nb):
            for j in range(PREP_CHUNKS):
                c = it * PREP_CHUNKS + j
                r0 = pl.multiple_of(c * c64, c64)
                rows = pl.ds(r0, c64)
                r_last = r0 if reverse else r0 + (c64 - 1)
                gcb = gc_ref[b, rows, :]
                gctb = gct_ref[b, c]
                glb = gc_ref[b, pl.ds(r_last, 1), :]
                betab = beta_ref[b, rows, :]
                for h in range(GDN_HEADS):
                    hs = slice(h * GDN_DK, (h + 1) * GDN_DK)
                    gl = lane0 + SUBLANES + h
                    k16 = k_ref[b, rows, hs]
                    k32 = k16.astype(F32)
                    beta = betab[:, lane0 + h:lane0 + h + 1]
                    gcol = gcb[:, gl:gl + 1]
                    dec2 = jnp.exp(jnp.minimum(gcol - gctb[gl:gl + 1, :], 0.0))
                    kb = k32 * beta
                    kk2 = _dot_nt(kb.astype(BF16), jnp.concatenate([k16, k16], axis=0))
                    ll2 = kk2 * dec2
                    st.append(dict(hs=hs, b=b, h=h, c=c, rows=rows, k32=k32, k16=k16, beta=beta, gcol=gcol,
                                   kb=kb, glast=glb[:, gl:gl + 1], dec=dec2[:, :c64],
                                   ll=ll2[:, :c64], x=jnp.where(m_l0x, -ll2, eye_l)))
        for _ in range(4):
            for s in st:
                x = s["x"]
                s["x"] = jnp.where(left, x, 0.0) + _dot16(x, jnp.concatenate([zpad, x], axis=0))
        for s in st:
            s["d"] = s["x"][:, :c64]
        for mask in (m_c1, m_c2):
            for s in st:
                s["t"] = _dot16(jnp.where(mask, s["ll"], 0.0), s["d"])
            for s in st:
                s["d"] = s["d"] - _dot16(s["d"], s["t"])
        for s in st:
            s["r"] = eye64 - s["d"] - _dot_hi(jnp.where(strict64, s["ll"], 0.0), s["d"])
        for s in st:
            s["d"] = s["d"] + _dot16(s["d"], s["r"])
        for s in st:
            eg = jnp.exp(s["gcol"])
            s["eg"] = eg
            v32 = v_ref[s["b"], s["rows"], s["hs"]].astype(F32)
            s["u"] = _dot_hi(s["d"], v32 * s["beta"])
            s["w"] = _dot16(s["d"], s["kb"] * eg)
        for s in st:
            b, h, hs, c, rows = s["b"], s["h"], s["hs"], s["c"], s["rows"]
            qs = q_ref[b, rows, hs].astype(F32) * scale
            attn = _dot_nt(qs.astype(BF16), s["k16"]) * jnp.where(causal[:, :c64], s["dec"], 0.0)
            u_ref[b, rows, hs] = s["u"]
            wq_ref[b, c, h, 0:c64, :] = s["w"].astype(BF16)
            wq_ref[b, c, h, c64:2 * c64, :] = (qs * s["eg"]).astype(BF16)
            kd_ref[b, rows, hs] = (s["k32"] * jnp.exp(s["glast"] - s["gcol"])).astype(BF16)
            at_ref[b, c, h] = attn.astype(BF16)
        return carry

    lax.fori_loop(0, nchunk // PREP_CHUNKS, prep, 0)

    def scan(step, carry):
        c = (nchunk - 1 - step) if reverse else step
        r0 = pl.multiple_of(c * c64, c64)
        rows = pl.ds(r0, c64)
        r_last = r0 if reverse else r0 + (c64 - 1)
        glb = [gc_ref[b, pl.ds(r_last, 1), :] for b in range(nb)]
        st = []
        for (b, h) in chains:
            s = s_ref[b, h]
            st.append(dict(b=b, h=h, hs=slice(h * GDN_DK, (h + 1) * GDN_DK), s=s,
                           ws=_dot(wq_ref[b, c, h], s.astype(BF16))))
        for d in st:
            v_new = u_ref[d["b"], rows, d["hs"]] - d["ws"][:c64]
            d["vn"] = v_new.astype(BF16)
        for d in st:
            b, h, hs = d["b"], d["h"], d["hs"]
            o = d["ws"][c64:] + _dot(at_ref[b, c, h], d["vn"])
            o_ref[b, rows, hs] = o.astype(BF16)
            egl = jnp.exp(glb[b][:, lane0 + SUBLANES + h:lane0 + SUBLANES + h + 1])
            s_ref[b, h] = d["s"] * egl + _dot_tn(kd_ref[b, rows, hs], d["vn"])
        return carry

    lax.fori_loop(0, nchunk, scan, 0)


def _gdn_scan(q, k, v, ab, cf, reverse):
    b, t, c = q.shape
    nt = t // TM
    nchunk = TM // GDN_CHUNK

    def tile(s):
        lat = (nt - 1 - s) if reverse else (s - 1)
        return jnp.where(s == 0, nt - 1, lat)

    spec = pl.BlockSpec((b, TM, c), lambda s: (0, tile(s), 0))
    return pl.pallas_call(
        functools.partial(_gdn_scan_kernel, reverse, b),
        grid=(nt,),
        in_specs=[spec, spec, spec,
                  pl.BlockSpec((b, TM, LANES), lambda s: (0, tile(s), 0)),
                  _full_spec((SUBLANES, LANES))],
        out_specs=spec,
        out_shape=jax.ShapeDtypeStruct((b, t, c), BF16),
        scratch_shapes=[pltpu.VMEM((b, GDN_HEADS, GDN_DK, GDN_DV), F32),
                        pltpu.VMEM((b, TM, LANES), F32),
                        pltpu.VMEM((b, nchunk, LANES, LANES), F32),
                        pltpu.VMEM((b, TM, LANES), F32),
                        pltpu.VMEM((b, TM, c), F32),
                        pltpu.VMEM((b, nchunk, GDN_HEADS, 2 * GDN_CHUNK, GDN_DK), BF16),
                        pltpu.VMEM((b, TM, c), BF16),
                        pltpu.VMEM((b, nchunk, GDN_HEADS, GDN_CHUNK, GDN_CHUNK), BF16)],
        compiler_params=_cparams(("arbitrary",)),
        name="gdn_scan_bwd" if reverse else "gdn_scan_fwd",
    )(q, k, v, ab, cf)


def _gdn_out_kernel(of_ref, ob_ref, gate_ref, h_ref, mod_ref, on_ref, w_ref, g_ref, out_ref, y_ref):
    for hh in range(GDN_HEADS):
        hs = slice(hh * GDN_DV, (hh + 1) * GDN_DV)
        o = of_ref[:, hs].astype(F32) + ob_ref[:, hs].astype(F32)
        gt = gate_ref[:, hs].astype(F32)
        y_ref[:, hs] = (_rms(o) * on_ref[...] * _silu(gt)).astype(BF16)
    out = _dot(y_ref[...], w_ref[...])
    out_ref[...] = _residual(h_ref[...], out, g_ref[...], mod_ref[2:3, :])


def _gdn_out(o_f, o_b, gate, h, mod, o_norm, w_out, g):
    b, t, d = h.shape
    nt = t // TM
    c = o_f.shape[-1]
    return pl.pallas_call(
        _gdn_out_kernel,
        grid=(b, nt),
        in_specs=[_tok_spec(c), _tok_spec(c), _tok_spec(c), _tok_spec(d), _mod_spec(nt, d),
                  _full_spec((1, GDN_DV)), _full_spec(w_out.shape), _full_spec((1, d))],
        out_specs=_tok_spec(d),
        out_shape=jax.ShapeDtypeStruct((b, t, d), F32),
        scratch_shapes=[pltpu.VMEM((TM, c), BF16)],
        compiler_params=_cparams(("parallel", "parallel")),
        name="gdn_out",
    )(o_f, o_b, gate, h, mod, o_norm, w_out, g)


def _conf_kernel(nt, h_ref, hp_ref, hn_ref, mod_ref, g0_ref, win_ref, bin_ref, dw_ref, dwb_ref,
                 lng_ref, lnb_ref, w_ref, b_ref, g1_ref, out_ref, ext_ref, sh_ref, y_ref):
    prev_ok, next_ok = _halo_valid(nt)
    g0, shift, scale = g0_ref[...], mod_ref[0:1, :], mod_ref[1:2, :]
    h = h_ref[...]
    ax16 = jnp.concatenate([_norm_mod(hp_ref[...], g0, shift, scale), _norm_mod(h, g0, shift, scale),
                            _norm_mod(hn_ref[...], g0, shift, scale)], axis=0).astype(BF16)
    d = h.shape[-1]
    cw = 512
    for n in range(d // cw):
        cs = slice(n * cw, (n + 1) * cw)
        gs = slice(d + n * cw, d + (n + 1) * cw)
        u = (_dot(ax16, win_ref[:, cs]) + bin_ref[:, cs]) * jax.nn.sigmoid(_dot(ax16, win_ref[:, gs]) + bin_ref[:, gs])
        ext_ref[0:HALO, cs] = jnp.where(prev_ok, u[0:HALO], 0.0)
        ext_ref[HALO:HALO + TM, cs] = u[HALO:HALO + TM]
        ext_ref[HALO + TM:2 * HALO + TM, cs] = jnp.where(next_ok, u[HALO + TM:], 0.0)
    pad = CONF_CONV // 2
    for r in range(1, SUBLANES):
        sh_ref[r - 1] = ext_ref[pl.ds(r, sh_ref.shape[1]), :]
    rw, cw = 64, 256
    for cc in range(d // cw):
        cs = slice(cc * cw, (cc + 1) * cw)
        for rr in range(TM // rw):
            acc = None
            for tap in range(CONF_CONV):
                off = HALO - pad + tap
                r = off % SUBLANES
                src = ext_ref if r == 0 else sh_ref.at[r - 1]
                term = dw_ref[tap:tap + 1, cs] * src[pl.ds(off - r + rr * rw, rw), cs]
                acc = term if acc is None else acc + term
            y_ref[rr * rw:(rr + 1) * rw, cs] = acc + dwb_ref[:, cs]
    y = y_ref[...]
    yc = y - jnp.mean(y, axis=-1, keepdims=True)
    yn = yc * lax.rsqrt(jnp.mean(yc * yc, axis=-1, keepdims=True) + LN_EPS) * lng_ref[...] + lnb_ref[...]
    out = _dot(_silu(yn).astype(BF16), w_ref[...]) + b_ref[...]
    out_ref[...] = _residual(h, out, g1_ref[...], mod_ref[2:3, :])


def _conformer(h, mod, g0, w_in, b_in, dw, dw_b, ln_g, ln_b, w_out, b_out, g1):
    b, t, d = h.shape
    nt = t // TM
    prev, nxt = _halo_specs(nt, d)
    vec = _full_spec((1, d))
    return pl.pallas_call(
        functools.partial(_conf_kernel, nt),
        grid=(b, nt),
        in_specs=[_tok_spec(d), prev, nxt, _mod_spec(nt, d), vec, _full_spec(w_in.shape), _full_spec(b_in.shape),
                  _full_spec(dw.shape), vec, vec, vec, _full_spec(w_out.shape), vec, vec],
        out_specs=_tok_spec(d),
        out_shape=jax.ShapeDtypeStruct((b, t, d), F32),
        scratch_shapes=[pltpu.VMEM((TM + 2 * HALO, d), F32),
                        pltpu.VMEM((SUBLANES - 1, TM + 2 * HALO - SUBLANES, d), F32),
                        pltpu.VMEM((TM, d), F32)],
        compiler_params=_cparams(("parallel", "parallel")),
        name="conformer",
    )(h, h, h, mod, g0, w_in, b_in, dw, dw_b, ln_g, ln_b, w_out, b_out, g1)


def _mla_proj_kernel(h_ref, mod_ref, g_ref, win_ref, qn_ref, kvn_ref, wuq_ref, wukv_ref, cs_ref,
                     q_ref, k_ref, v_ref):
    a16 = _norm_mod(h_ref[...], g_ref[...], mod_ref[0:1, :], mod_ref[1:2, :]).astype(BF16)
    p = _dot(a16, win_ref[...])
    cq = _rms(p[:, :MLA_Q_RANK]) * qn_ref[...]
    ckv = _rms(p[:, MLA_Q_RANK:MLA_Q_RANK + MLA_KV_RANK]) * kvn_ref[...]
    kr = p[:, MLA_Q_RANK + MLA_KV_RANK:]
    cs = cs_ref[...]
    kx = kr * cs
    krot = (kx + pltpu.roll(kx, MLA_ROPE, 1)).astype(BF16)
    scale = (MLA_NOPE + MLA_ROPE) ** -0.5 * LOG2E
    cq16 = cq.astype(BF16)
    ckv16 = ckv.astype(BF16)
    hw = MLA_NOPE + 2 * MLA_ROPE
    for hh in range(MLA_HEADS):
        qh = _dot(cq16, wuq_ref[:, hh * hw:(hh + 1) * hw]) * scale
        q_ref[:, hh * hw:hh * hw + MLA_NOPE] = qh[:, :MLA_NOPE].astype(BF16)
        q_ref[:, hh * hw + MLA_NOPE:(hh + 1) * hw] = (qh[:, MLA_NOPE:] * cs).astype(BF16)
        k_ref[:, hh * hw:hh * hw + MLA_NOPE] = _dot(
            ckv16, wukv_ref[:, hh * MLA_NOPE:(hh + 1) * MLA_NOPE]).astype(BF16)
        k_ref[:, hh * hw + MLA_NOPE:(hh + 1) * hw] = krot
        nk = MLA_HEADS * MLA_NOPE
        v_ref[:, 2 * hh * MLA_V:(2 * hh + 1) * MLA_V] = _dot(
            ckv16, wukv_ref[:, nk + hh * MLA_V:nk + (hh + 1) * MLA_V]).astype(BF16)
        v_ref[:, (2 * hh + 1) * MLA_V:(2 * hh + 2) * MLA_V] = jnp.ones((TM, MLA_V), BF16)


def _mla_proj(h, mod, g, w_in, q_norm, kv_norm, w_uq, w_ukv, cs_tab):
    b, t, d = h.shape
    nt = t // TM
    hw = MLA_NOPE + 2 * MLA_ROPE
    return pl.pallas_call(
        _mla_proj_kernel,
        grid=(b, nt),
        in_specs=[_tok_spec(d), _mod_spec(nt, d), _full_spec((1, d)), _full_spec(w_in.shape),
                  _full_spec(q_norm.shape), _full_spec(kv_norm.shape),
                  _full_spec(w_uq.shape), _full_spec(w_ukv.shape),
                  pl.BlockSpec((TM, LANES), lambda bb, j: (j, 0))],
        out_specs=[_tok_spec(MLA_HEADS * hw), _tok_spec(MLA_HEADS * hw), _tok_spec(MLA_HEADS * 2 * MLA_V)],
        out_shape=[jax.ShapeDtypeStruct((b, t, MLA_HEADS * hw), BF16),
                   jax.ShapeDtypeStruct((b, t, MLA_HEADS * hw), BF16),
                   jax.ShapeDtypeStruct((b, t, MLA_HEADS * 2 * MLA_V), BF16)],
        compiler_params=_cparams(("parallel", "parallel")),
        name="mla_proj",
    )(h, mod, g, w_in, q_norm, kv_norm, w_uq, w_ukv, cs_tab)


def _attn_kernel(tk, q_ref, k_ref, v_ref, o_ref, m_ref, acc_ref):
    q = q_ref[...]
    for i in range(k_ref.shape[0] // tk):
        rows = slice(i * tk, (i + 1) * tk)
        s = _dot_nt(q, k_ref[rows, :])
        m_blk = jnp.max(s, axis=-1, keepdims=True)
        if i == 0:
            m_new = m_blk
            acc_ref[...] = _dot(jnp.exp2(s - m_new).astype(BF16), v_ref[rows, :])
        else:
            m_old = m_ref[...]
            m_new = jnp.maximum(m_old, m_blk)
            acc_ref[...] = jnp.exp2(m_old - m_new) * acc_ref[...] + _dot(
                jnp.exp2(s - m_new).astype(BF16), v_ref[rows, :])
        m_ref[...] = m_new
    acc = acc_ref[...]
    o_ref[...] = (acc[:, :MLA_V] / acc[:, MLA_V:]).astype(BF16)


def _attention(q, k, v, tq, q_rows, q_row0, k_rows, k_row0, name):
    b = q.shape[0]
    hw = MLA_NOPE + 2 * MLA_ROPE
    tk = next(c for c in (768, 512, 256) if k_rows % c == 0)
    q0 = q_row0 // tq
    k0 = k_row0 // k_rows
    return pl.pallas_call(
        functools.partial(_attn_kernel, tk),
        grid=(b, MLA_HEADS, q_rows // tq),
        in_specs=[pl.BlockSpec((None, tq, hw), lambda bb, hh, i: (bb, q0 + i, hh)),
                  pl.BlockSpec((None, k_rows, hw), lambda bb, hh, i: (bb, k0, hh)),
                  pl.BlockSpec((None, k_rows, 2 * MLA_V), lambda bb, hh, i: (bb, k0, hh))],
        out_specs=pl.BlockSpec((None, tq, MLA_V), lambda bb, hh, i: (bb, i, hh)),
        out_shape=jax.ShapeDtypeStruct((b, q_rows, MLA_HEADS * MLA_V), BF16),
        scratch_shapes=[pltpu.VMEM((tq, 1), F32), pltpu.VMEM((tq, 2 * MLA_V), F32)],
        compiler_params=_cparams(("parallel", "parallel", "arbitrary")),
        name=name,
    )(q, k, v)


def _proj_out_kernel(o_ref, w_ref, h_ref, mod_ref, g_ref, out_ref):
    out = _dot(o_ref[...], w_ref[...])
    out_ref[...] = _residual(h_ref[...], out, g_ref[...], mod_ref[2:3, :])


def _proj_out(o, w, h, mod, g):
    b, t, d = h.shape
    nt = t // TM
    return pl.pallas_call(
        _proj_out_kernel,
        grid=(b, nt),
        in_specs=[_tok_spec(o.shape[-1]), _full_spec(w.shape), _tok_spec(d), _mod_spec(nt, d),
                  _full_spec((1, d))],
        out_specs=_tok_spec(d),
        out_shape=jax.ShapeDtypeStruct((b, t, d), F32),
        compiler_params=_cparams(("parallel", "parallel")),
        name="mla_out",
    )(o, w, h, mod, g)


def _ffn_kernel(fc, tps, nt, h_ref, mod_ref, g2_ref, wg_ref, wu_ref, wd_ref, g3_ref, out_ref, acc_ref):
    b = pl.program_id(0)
    j0 = pl.program_id(1) * tps
    hs = [h_ref[s * TM:(s + 1) * TM, :] for s in range(tps)]
    mods = [mod_ref[jnp.where(j0 + s == nt - 1, 0, b + 1)] for s in range(tps)]
    a16 = jnp.concatenate([_norm_mod(hs[s], g2_ref[...], mods[s][3:4, :], mods[s][4:5, :])
                           for s in range(tps)], axis=0).astype(BF16)
    for c in range(wg_ref.shape[-1] // fc):
        fs = slice(c * fc, (c + 1) * fc)
        act = (_silu(_dot(a16, wg_ref[:, fs])) * _dot(a16, wu_ref[:, fs])).astype(BF16)
        part = _dot(act, wd_ref[fs, :])
        if c == 0:
            acc_ref[...] = part
        else:
            acc_ref[...] += part
    for s in range(tps):
        rows = slice(s * TM, (s + 1) * TM)
        out_ref[rows, :] = _residual(hs[s], acc_ref[rows, :], g3_ref[...], mods[s][5:6, :])


def _ffn(h, mod, g2, w_gate, w_up, w_down, g3, n_tiles):
    b, t, d = h.shape
    nt = t // TM
    tps = 3 if n_tiles % 3 == 0 else 1
    tok = pl.BlockSpec((None, tps * TM, d), lambda bb, j: (bb, j, 0))

    def resident(shape):
        return pl.BlockSpec(shape, lambda *_: (0,) * len(shape), pipeline_mode=pl.Buffered(1))

    return pl.pallas_call(
        functools.partial(_ffn_kernel, 1408, tps, nt),
        grid=(b, n_tiles // tps),
        in_specs=[tok, _full_spec(mod.shape), _full_spec((1, d)), resident(w_gate.shape),
                  resident(w_up.shape), resident(w_down.shape), _full_spec((1, d))],
        out_specs=tok,
        out_shape=jax.ShapeDtypeStruct((b, n_tiles * TM, d), F32),
        scratch_shapes=[pltpu.VMEM((tps * TM, d), F32)],
        compiler_params=_cparams(("parallel", "parallel")),
        name="ffn",
    )(h, mod, g2, w_gate, w_up, w_down, g3)


def _moe_route_kernel(h_ref, mod_ref, g2_ref, wr_ref, a_ref, rt_ref):
    lane = lax.broadcasted_iota(jnp.int32, (TM, LANES), 1).astype(F32)
    a = _norm_mod(h_ref[...], g2_ref[...], mod_ref[3:4, :], mod_ref[4:5, :])
    for s in range(SUBLANES):
        a_ref[pl.ds(s, TM, stride=SUBLANES), :] = a[:, s * LANES:(s + 1) * LANES]
    logits = jnp.where(lane < N_EXPERTS, _dot_hi(a, wr_ref[...]), -jnp.inf)
    m1 = jnp.max(logits, axis=-1, keepdims=True)
    i1 = jnp.min(jnp.where(logits == m1, lane, float(LANES)), axis=-1, keepdims=True)
    rest = jnp.where(lane == i1, -jnp.inf, logits)
    m2 = jnp.max(rest, axis=-1, keepdims=True)
    i2 = jnp.min(jnp.where(rest == m2, lane, float(LANES)), axis=-1, keepdims=True)
    e2 = jnp.exp(m2 - m1)
    w1 = 1.0 / (1.0 + e2)
    rt_ref[...] = (jnp.where(lane == 0.0, i1, 0.0) + jnp.where(lane == 1.0, i2, 0.0)
                   + jnp.where(lane == 2.0, w1, 0.0) + jnp.where(lane == 3.0, e2 * w1, 0.0))


def _moe_route(h, mod, g2, w_router, n_tiles):
    b, t, d = h.shape
    nt = t // TM
    return pl.pallas_call(
        _moe_route_kernel,
        grid=(b, n_tiles),
        in_specs=[_tok_spec(d), _mod_spec(nt, d), _full_spec((1, d)), _full_spec(w_router.shape)],
        out_specs=[pl.BlockSpec((TM * SUBLANES, LANES), lambda bb, j: (bb * n_tiles + j, 0)),
                   _tok_spec(LANES)],
        out_shape=[jax.ShapeDtypeStruct((b * n_tiles * TM * SUBLANES, LANES), F32),
                   jax.ShapeDtypeStruct((b, n_tiles * TM, LANES), F32)],
        compiler_params=_cparams(("parallel", "parallel")),
        name="moe_route",
    )(h, mod, g2, w_router)


def _gather_tiles(idx_ref, base, n, src_hbm, dst, sem):
    def body(r, carry):
        row = pl.multiple_of(idx_ref[base + r], SUBLANES)
        pltpu.make_async_copy(src_hbm.at[pl.ds(row, SUBLANES), :],
                              dst.at[pl.ds(pl.multiple_of(r * SUBLANES, SUBLANES), SUBLANES), :], sem).start()
        return carry
    lax.fori_loop(0, n, body, 0, unroll=8)


def _wait_tiles(n, src_hbm, dst, sem):
    pltpu.make_async_copy(src_hbm.at[pl.ds(0, n * SUBLANES), :], dst, sem).wait()


def _untile(buf, n):
    return jnp.concatenate([buf[pl.ds(s, n, stride=SUBLANES), :] for s in range(SUBLANES)], axis=1)


def _moe_dispatch_kernel(dst_ref, last_ref, nact_ref, a_ref, x_hbm, stage_ref, sem, zsem):
    i = pl.program_id(0)
    n = pl.num_programs(0)
    rows = TM * SUBLANES
    slot = i % 2
    n_tiles_max = x_hbm.shape[0] // rows

    @pl.when(i == 0)
    def _():
        zero = stage_ref.at[1]
        zero[...] = jnp.zeros_like(zero)

        def zero_tile(t):
            return pltpu.make_async_copy(zero, x_hbm.at[pl.ds(pl.multiple_of(t * rows, rows), rows), :], zsem)

        for e in range(N_EXPERTS):
            zero_tile(last_ref[e]).start()
            pl.when(nact_ref[0] + e < n_tiles_max)(lambda e=e: zero_tile(nact_ref[0] + e).start())
        for e in range(N_EXPERTS):
            zero_tile(0).wait()
            pl.when(nact_ref[0] + e < n_tiles_max)(lambda: zero_tile(0).wait())

    stage = stage_ref.at[slot]
    stage[...] = a_ref[...]

    def body(r, carry):
        src = pl.multiple_of(r * SUBLANES, SUBLANES)
        for k in range(2):
            dst = pl.multiple_of(dst_ref[(i * TM + r) * 2 + k], SUBLANES)
            pltpu.make_async_copy(stage.at[pl.ds(src, SUBLANES), :], x_hbm.at[pl.ds(dst, SUBLANES), :],
                                  sem.at[slot]).start()
        return carry
    lax.fori_loop(0, TM, body, 0, unroll=4)

    def wait(sl):
        for k in range(2):
            pltpu.make_async_copy(stage_ref.at[sl], x_hbm.at[pl.ds(0, rows), :], sem.at[sl]).wait()

    @pl.when(i > 0)
    def _():
        wait(1 - slot)

    @pl.when(i == n - 1)
    def _():
        wait(slot)


def _moe_dispatch(dst_rows, last_tile, n_active, a2d, n_rows):
    n_steps = a2d.shape[0] // (TM * SUBLANES)
    return pl.pallas_call(
        _moe_dispatch_kernel,
        grid_spec=pltpu.PrefetchScalarGridSpec(
            num_scalar_prefetch=3,
            grid=(n_steps,),
            in_specs=[pl.BlockSpec((TM * SUBLANES, LANES), lambda i, ds, lt, na: (i, 0))],
            out_specs=pl.BlockSpec(memory_space=pl.ANY),
            scratch_shapes=[pltpu.VMEM((2, TM * SUBLANES, LANES), F32), pltpu.SemaphoreType.DMA((2,)),
                            pltpu.SemaphoreType.DMA(())]),
        out_shape=jax.ShapeDtypeStruct((n_rows * SUBLANES, LANES), F32),
        compiler_params=_cparams(("arbitrary",)),
        name="moe_dispatch",
    )(dst_rows, last_tile, n_active, a2d)


def _moe_experts_kernel(te_ref, nact_ref, x_ref, wg_ref, wu_ref, wd_ref, y_ref, wg16, wu16, wd16):
    i = pl.program_id(0)

    @pl.when(jnp.logical_or(i == 0, te_ref[i] != te_ref[jnp.maximum(i - 1, 0)]))
    def _():
        wg16[...] = wg_ref[...].astype(BF16)
        wu16[...] = wu_ref[...].astype(BF16)
        wd16[...] = wd_ref[...].astype(BF16)

    @pl.when(i < nact_ref[0])
    def _():
        x16 = _untile(x_ref, TM).astype(BF16)
        f = wg_ref.shape[-1]
        acc = None
        for c0 in range(0, f, 512):
            fs = slice(c0, min(c0 + 512, f))
            act = (_silu(_dot(x16, wg16[:, fs])) * _dot(x16, wu16[:, fs])).astype(BF16)
            part = _dot(act, wd16[fs, :])
            acc = part if acc is None else acc + part
        for s in range(SUBLANES):
            y_ref[pl.ds(s, TM, stride=SUBLANES), :] = acc[:, s * LANES:(s + 1) * LANES]

    @pl.when(i >= nact_ref[0])
    def _():
        y_ref[...] = jnp.zeros_like(y_ref)


def _moe_experts(tile_expert, n_active, x2d, w_gate, w_up, w_down, layer):
    n_tiles = tile_expert.shape[0]
    _, _, d, f = w_gate.shape
    wspec_in = pl.BlockSpec((None, None, d, f), lambda i, te, na: (layer, te[i], 0, 0))
    wspec_out = pl.BlockSpec((None, None, f, d), lambda i, te, na: (layer, te[i], 0, 0))
    return pl.pallas_call(
        _moe_experts_kernel,
        grid_spec=pltpu.PrefetchScalarGridSpec(
            num_scalar_prefetch=2,
            grid=(n_tiles,),
            in_specs=[pl.BlockSpec((TM * SUBLANES, LANES), lambda i, te, na: (jnp.minimum(i, na[0] - 1), 0)),
                      wspec_in, wspec_in, wspec_out],
            out_specs=pl.BlockSpec((TM * SUBLANES, LANES), lambda i, te, na: (i, 0)),
            scratch_shapes=[pltpu.VMEM((d, f), BF16), pltpu.VMEM((d, f), BF16), pltpu.VMEM((f, d), BF16)]),
        out_shape=jax.ShapeDtypeStruct((n_tiles * TM * SUBLANES, LANES), F32),
        compiler_params=_cparams(("arbitrary",)),
        name="moe_experts",
    )(tile_expert, n_active, x2d, w_gate, w_up, w_down)


def _moe_combine_kernel(n_tiles, dst_ref, y_hbm, rt_ref, h_ref, mod_ref, g3_ref, out_ref, ybuf, sem):
    step = pl.program_id(0) * n_tiles + pl.program_id(1)
    nsteps = pl.num_programs(0) * n_tiles

    @pl.when(step == 0)
    def _():
        _gather_tiles(dst_ref, 0, 2 * TM, y_hbm, ybuf.at[0], sem.at[0])

    @pl.when(step + 1 < nsteps)
    def _():
        nxt = (step + 1) % 2
        _gather_tiles(dst_ref, (step + 1) * 2 * TM, 2 * TM, y_hbm, ybuf.at[nxt], sem.at[nxt])

    cur = step % 2
    _wait_tiles(2 * TM, y_hbm, ybuf.at[cur], sem.at[cur])
    yb = ybuf.at[cur]
    w1 = rt_ref[:, 2:3]
    w2 = rt_ref[:, 3:4]
    f = jnp.concatenate(
        [w1 * yb[pl.ds(s, TM, stride=2 * SUBLANES), :] + w2 * yb[pl.ds(SUBLANES + s, TM, stride=2 * SUBLANES), :]
         for s in range(SUBLANES)], axis=1)
    out_ref[...] = _residual(h_ref[...], f, g3_ref[...], mod_ref[5:6, :])


def _moe_combine(dst_rows, y2d, rt, h, mod, g3, n_tiles):
    b, t, d = h.shape
    nt = t // TM
    tok = lambda c: pl.BlockSpec((None, TM, c), lambda bb, j, ds: (bb, j, 0))
    return pl.pallas_call(
        functools.partial(_moe_combine_kernel, n_tiles),
        grid_spec=pltpu.PrefetchScalarGridSpec(
            num_scalar_prefetch=1,
            grid=(b, n_tiles),
            in_specs=[pl.BlockSpec(memory_space=pl.ANY), tok(LANES), tok(d),
                      pl.BlockSpec((None, SUBLANES, d),
                                   lambda bb, j, ds: (jnp.where(j == nt - 1, 0, bb + 1), 0, 0)),
                      pl.BlockSpec((1, d), lambda bb, j, ds: (0, 0))],
            out_specs=tok(d),
            scratch_shapes=[pltpu.VMEM((2, 2 * TM * SUBLANES, LANES), F32), pltpu.SemaphoreType.DMA((2,))]),
        out_shape=jax.ShapeDtypeStruct((b, n_tiles * TM, d), F32),
        compiler_params=_cparams(("arbitrary", "arbitrary")),
        name="moe_combine",
    )(dst_rows, y2d, rt, h, mod, g3)


def _moe(h, mod, g2, w_router, w_gate, w_up, w_down, layer, g3, n_tiles):
    b = h.shape[0]
    n_tok = b * n_tiles * TM
    a2d, rt = _moe_route(h, mod, g2, w_router, n_tiles)
    expert = rt.reshape(n_tok, LANES)[:, :2].astype(jnp.int32).reshape(-1)
    onehot = (expert[:, None] == jnp.arange(N_EXPERTS)[None, :]).astype(jnp.int32)
    rank = jnp.sum((jnp.cumsum(onehot, axis=0) - onehot) * onehot, axis=1)
    count = jnp.sum(onehot, axis=0)
    padded = (count + TM - 1) // TM * TM
    ends = jnp.cumsum(padded)
    dest = (ends - padded)[expert] + rank
    n_tiles_max = 2 * n_tok // TM + N_EXPERTS
    tile_start = jnp.arange(n_tiles_max, dtype=jnp.int32) * TM
    tile_expert = jnp.minimum(jnp.sum(ends[None, :] <= tile_start[:, None], axis=1), N_EXPERTS - 1)
    n_active = (ends[-1] // TM).reshape(1).astype(jnp.int32)
    last_tile = jnp.maximum(ends // TM - 1, 0).astype(jnp.int32)
    dst_rows = (dest * SUBLANES).astype(jnp.int32)
    x2d = _moe_dispatch(dst_rows, last_tile, n_active, a2d, n_tiles_max * TM)
    y2d = _moe_experts(tile_expert.astype(jnp.int32), n_active, x2d, w_gate, w_up, w_down, layer)
    return _moe_combine(dst_rows, y2d, rt, h, mod, g3, n_tiles)


def _pad_rows(a, rows):
    return jnp.pad(a, ((0, rows - a.shape[0]),) + ((0, 0),) * (a.ndim - 1))


def _rope_table(seq, ctx_len):
    n_freq = MLA_ROPE // 4
    t = jnp.arange(seq)
    pos = jnp.stack([(t // GRID_W).astype(F32), (t % GRID_W).astype(F32)], axis=1)
    inv = ROPE_THETA ** (-jnp.arange(n_freq, dtype=F32) / n_freq)
    ang = pos[:, :, None] * inv
    cos = jnp.repeat(jnp.cos(ang)[:, :, None, :], 2, axis=2).reshape(seq, MLA_ROPE)
    sin = jnp.stack([-jnp.sin(ang), jnp.sin(ang)], axis=2).reshape(seq, MLA_ROPE)
    lat = jnp.concatenate([cos, sin], axis=1)
    ctx = jnp.concatenate([jnp.ones((ctx_len, MLA_ROPE), F32), jnp.zeros((ctx_len, MLA_ROPE), F32)], axis=1)
    return jnp.concatenate([lat, ctx], axis=0)


def _swap_halves(w):
    shp = w.shape
    return jnp.flip(w.reshape(shp[:-1] + (2, 2, MLA_ROPE // 4)), axis=-2).reshape(shp)


def kernel(x, c, ctx, c_ctx, w_mod, b_mod, norm_g, gdn_w_in, gdn_conv, gdn_a_log, gdn_dt_bias, gdn_o_norm, gdn_w_out, conf_w_in, conf_b_in, conf_dw, conf_dw_b, conf_ln_g, conf_ln_b, conf_w_out, conf_b_out, mla_w_in, mla_q_norm, mla_kv_norm, mla_w_uq, mla_w_ukv, mla_w_o, ffn_w_gate, ffn_w_up, ffn_w_down, moe_router, moe_w_gate, moe_w_up, moe_w_down):
    bsz, seq, d = x.shape
    ctx_len = ctx.shape[1]
    depth = w_mod.shape[0]
    assert ctx_len == TM and seq % (4 * TM) == 0 and bsz + 1 <= SUBLANES
    nt = (seq + ctx_len) // TM
    n_lat = seq // TM

    cvec = _pad_rows(jnp.concatenate([c_ctx[None, :], c], axis=0), SUBLANES)
    mods = _adaln(cvec, w_mod, b_mod)
    mods = mods.reshape(depth, SUBLANES, N_MOD, d)[:, :bsz + 1]
    mods = jnp.pad(mods, ((0, 0), (0, 0), (0, SUBLANES - N_MOD), (0, 0)))

    h = jnp.concatenate([x, ctx], axis=1)
    rope_cs = _rope_table(seq, ctx_len)

    n_mix = 3
    for i in range(depth):
        last = i == depth - 1
        mod = mods[i]
        g = norm_g[i].reshape(4, 1, d)
        kind, j = i % n_mix, i // n_mix
        if kind == 0:
            nqk = GDN_HEADS * GDN_DK
            nmain = 2 * nqk + 2 * GDN_HEADS * GDN_DV
            w_in = gdn_w_in[j]
            w_main = w_in[:, :nmain].astype(BF16)
            w_tail = w_in[:, nmain:]
            nh = GDN_HEADS
            w_ab = jnp.concatenate([w_tail[:, (2 * part + dr) * nh:(2 * part + dr + 1) * nh]
                                    for dr in range(2) for part in range(2)], axis=1)
            w_ab = jnp.pad(w_ab, ((0, 0), (0, LANES - 4 * nh)))

            def decay_lanes(p):
                z = jnp.zeros((nh,), F32)
                return jnp.pad(jnp.concatenate([z, p[0], z, p[1]]), (0, LANES - 4 * nh))

            cf = _pad_rows(jnp.stack([decay_lanes(jnp.exp(gdn_a_log[j])), decay_lanes(gdn_dt_bias[j])]),
                           SUBLANES)
            q, k, v, gate, ab = _gdn_in(h, mod, g[0], w_main, w_ab, _pad_rows(gdn_conv[j], SUBLANES))
            o_f = _gdn_scan(q, k, v, ab, cf, reverse=False)
            o_b = _gdn_scan(q, k, v, ab, cf, reverse=True)
            h = _gdn_out(o_f, o_b, gate, h, mod, gdn_o_norm[j].reshape(1, -1),
                         gdn_w_out[j].astype(BF16), g[1])
        elif kind == 1:
            h = _conformer(h, mod, g[0], conf_w_in[j].astype(BF16), conf_b_in[j].reshape(1, -1),
                           _pad_rows(conf_dw[j], 32), conf_dw_b[j].reshape(1, -1),
                           conf_ln_g[j].reshape(1, -1), conf_ln_b[j].reshape(1, -1),
                           conf_w_out[j].astype(BF16), conf_b_out[j].reshape(1, -1), g[1])
        else:
            w_in = mla_w_in[j]
            w_kr = w_in[:, MLA_Q_RANK + MLA_KV_RANK:]
            w_in2 = jnp.concatenate([w_in, _swap_halves(w_kr)], axis=1).astype(BF16)
            hq = MLA_NOPE + MLA_ROPE
            w_uq = mla_w_uq[j].reshape(MLA_Q_RANK, MLA_HEADS, hq)
            w_uq2 = jnp.concatenate([w_uq, _swap_halves(w_uq[..., MLA_NOPE:])], axis=-1)
            w_uq2 = w_uq2.reshape(MLA_Q_RANK, -1).astype(BF16)
            w_ukv = mla_w_ukv[j].reshape(MLA_KV_RANK, MLA_HEADS, MLA_NOPE + MLA_V)
            w_ukv2 = jnp.concatenate([w_ukv[..., :MLA_NOPE].reshape(MLA_KV_RANK, -1),
                                      w_ukv[..., MLA_NOPE:].reshape(MLA_KV_RANK, -1)], axis=1).astype(BF16)
            qa, ka, va = _mla_proj(h, mod, g[0], w_in2, mla_q_norm[j].reshape(1, -1),
                                   mla_kv_norm[j].reshape(1, -1), w_uq2, w_ukv2, rope_cs)
            o_l = _attention(qa, ka, va, 4 * TM, seq, 0, seq + ctx_len, 0, "attn_latent")
            o_c = _attention(qa, ka, va, TM, ctx_len, seq, ctx_len, seq, "attn_context")
            o = jnp.concatenate([o_l, o_c], axis=1)
            h = _proj_out(o, mla_w_o[j].astype(BF16), h, mod, g[1])

        n_tiles = n_lat if last else nt
        f = i // 2
        if i % 2 == 0:
            h = _ffn(h, mod, g[2], ffn_w_gate[f].astype(BF16), ffn_w_up[f].astype(BF16),
                     ffn_w_down[f].astype(BF16), g[3], n_tiles)
        else:
            w_r = jnp.pad(moe_router[f], ((0, 0), (0, LANES - N_EXPERTS)))
            h = _moe(h, mod, g[2], w_r, moe_w_gate, moe_w_up, moe_w_down, f, g[3], n_tiles)
    return h
```

```python
import functools

import jax
import jax.numpy as jnp
from jax import lax
from jax.experimental import pallas as pl
from jax.experimental.pallas import tpu as pltpu

F32 = jnp.float32
BF16 = jnp.bfloat16

N_MOD = 6
RMS_EPS = 1e-6
LN_EPS = 1e-5
GRID_W = 64
GDN_HEADS = 8
GDN_DK = 128
GDN_DV = 128
GDN_CONV = 5
GDN_CHUNK = 64
CONF_CONV = 31
MLA_HEADS = 8
MLA_NOPE = 128
MLA_ROPE = 64
MLA_V = 128
MLA_Q_RANK = 512
MLA_KV_RANK = 256
ROPE_THETA = 10000.0
LOG2E = 1.4426950408889634
N_EXPERTS = 8

LANES = 128
SUBLANES = 8
TM = 256
HALO = 16
PREP_CHUNKS = 2
VMEM_LIMIT = 56 * 1024 * 1024


def _cparams(sem):
    return pltpu.CompilerParams(dimension_semantics=sem, vmem_limit_bytes=VMEM_LIMIT)


def _dot(a, b, **kw):
    return jnp.dot(a, b, preferred_element_type=F32, **kw)


def _dot_nt(a, b):
    return lax.dot_general(a, b, (((1,), (1,)), ((), ())), preferred_element_type=F32)


def _dot_tn(a, b):
    return lax.dot_general(a, b, (((0,), (0,)), ((), ())), preferred_element_type=F32)


def _silu(x):
    return x * jax.nn.sigmoid(x)


def _rms(x, eps=RMS_EPS):
    return x * lax.rsqrt(jnp.mean(x * x, axis=-1, keepdims=True) + eps)


def _norm_mod(h, g, shift, scale):
    return _rms(h) * g * (1.0 + scale) + shift


def _residual(h, o, g, gate):
    return h + gate * (_rms(o) * g)


def _tok_spec(c):
    return pl.BlockSpec((None, TM, c), lambda b, j: (b, j, 0))


def _full_spec(shape):
    nd = len(shape)
    return pl.BlockSpec(shape, lambda *_: (0,) * nd)


def _mod_spec(nt, d):
    return pl.BlockSpec((None, SUBLANES, d), lambda b, j: (jnp.where(j == nt - 1, 0, b + 1), 0, 0))


def _halo_specs(nt, c):
    per = TM // HALO
    prev = pl.BlockSpec((None, HALO, c), lambda b, j: (b, jnp.maximum(j * per - 1, 0), 0))
    nxt = pl.BlockSpec((None, HALO, c), lambda b, j: (b, jnp.minimum((j + 1) * per, nt * per - 1), 0))
    return prev, nxt


def _halo_valid(nt):
    j = pl.program_id(1)
    prev_ok = jnp.logical_and(j != 0, j != nt - 1)
    next_ok = j < nt - 2
    return prev_ok, next_ok


def _adaln_kernel(c_ref, w_ref, b_ref, o_ref):
    o_ref[...] = _dot_hi(_silu(c_ref[...]), w_ref[...]) + b_ref[...]


def _adaln(cvec, w_mod, b_mod):
    depth, d, n = w_mod.shape
    tn = 1536
    return pl.pallas_call(
        _adaln_kernel,
        grid=(depth, n // tn),
        in_specs=[pl.BlockSpec((SUBLANES, d), lambda i, k: (0, 0)),
                  pl.BlockSpec((None, d, tn), lambda i, k: (i, 0, k)),
                  pl.BlockSpec((None, 1, tn), lambda i, k: (i, 0, k))],
        out_specs=pl.BlockSpec((None, SUBLANES, tn), lambda i, k: (i, 0, k)),
        out_shape=jax.ShapeDtypeStruct((depth, SUBLANES, n), F32),
        compiler_params=_cparams(("parallel", "parallel")),
        name="adaln",
    )(cvec, w_mod, b_mod.reshape(depth, 1, n))


def _gdn_in_kernel(nt, h_ref, hp_ref, hn_ref, mod_ref, g_ref, w_ref, wab_ref, cw_ref,
                   q_ref, k_ref, v_ref, gate_ref, ab_ref, ext_ref):
    prev_ok, next_ok = _halo_valid(nt)
    g, shift, scale = g_ref[...], mod_ref[0:1, :], mod_ref[1:2, :]
    a = _norm_mod(h_ref[...], g, shift, scale)
    a16 = a.astype(BF16)
    ax16 = jnp.concatenate([_norm_mod(hp_ref[...], g, shift, scale), a,
                            _norm_mod(hn_ref[...], g, shift, scale)], axis=0).astype(BF16)
    cw = 512
    pad = GDN_CONV // 2
    nqk = GDN_HEADS * GDN_DK
    nq = 2 * nqk + GDN_HEADS * GDN_DV
    for cc in range(nq // cw):
        cs = slice(cc * cw, (cc + 1) * cw)
        p = _dot(ax16, w_ref[:, cs])
        ext = ext_ref.at[cc]
        ext[0:SUBLANES, :] = jnp.where(prev_ok, p[0:SUBLANES], 0.0)
        ext[SUBLANES:SUBLANES + TM, :] = p[SUBLANES:SUBLANES + TM]
        ext[SUBLANES + TM:2 * SUBLANES + TM, :] = jnp.where(next_ok, p[SUBLANES + TM:], 0.0)
        acc = cw_ref[0:1, cs] * ext[pl.ds(SUBLANES - pad, TM), :]
        for tap in range(1, GDN_CONV):
            acc = acc + cw_ref[tap:tap + 1, cs] * ext[pl.ds(SUBLANES - pad + tap, TM), :]
        y = _silu(acc)
        for hh in range(cw // LANES):
            col = cc * cw + hh * LANES
            yh = y[:, hh * LANES:(hh + 1) * LANES]
            if col < 2 * nqk:
                yh = yh * lax.rsqrt(jnp.sum(yh * yh, axis=-1, keepdims=True) + RMS_EPS)
            if col < nqk:
                q_ref[:, col:col + LANES] = yh.astype(BF16)
            elif col < 2 * nqk:
                k_ref[:, col - nqk:col - nqk + LANES] = yh.astype(BF16)
            else:
                v_ref[:, col - 2 * nqk:col - 2 * nqk + LANES] = yh.astype(BF16)
    for n in range(gate_ref.shape[-1] // cw):
        gate_ref[:, n * cw:(n + 1) * cw] = _dot(a16, w_ref[:, nq + n * cw:nq + (n + 1) * cw]).astype(BF16)
    ab_ref[...] = _dot_hi(a, wab_ref[...])


def _gdn_in(h, mod, g, w_main, w_ab, conv_w):
    b, t, d = h.shape
    nt = t // TM
    nqk = GDN_HEADS * GDN_DK
    nv = GDN_HEADS * GDN_DV
    per = TM // SUBLANES
    prev = pl.BlockSpec((None, SUBLANES, d), lambda bb, j: (bb, jnp.maximum(j * per - 1, 0), 0))
    nxt = pl.BlockSpec((None, SUBLANES, d), lambda bb, j: (bb, jnp.minimum((j + 1) * per, nt * per - 1), 0))
    return pl.pallas_call(
        functools.partial(_gdn_in_kernel, nt),
        grid=(b, nt),
        in_specs=[_tok_spec(d), prev, nxt, _mod_spec(nt, d), _full_spec((1, d)),
                  _full_spec(w_main.shape), _full_spec(w_ab.shape), _full_spec(conv_w.shape)],
        out_specs=[_tok_spec(nqk), _tok_spec(nqk), _tok_spec(nv), _tok_spec(nv), _tok_spec(LANES)],
        out_shape=[jax.ShapeDtypeStruct((b, t, nqk), BF16),
                   jax.ShapeDtypeStruct((b, t, nqk), BF16),
                   jax.ShapeDtypeStruct((b, t, nv), BF16),
                   jax.ShapeDtypeStruct((b, t, nv), BF16),
                   jax.ShapeDtypeStruct((b, t, LANES), F32)],
        scratch_shapes=[pltpu.VMEM(((2 * nqk + nv) // 512, TM + 2 * SUBLANES, 512), F32)],
        compiler_params=_cparams(("parallel", "parallel")),
        name="gdn_in",
    )(h, h, h, mod, g, w_main, w_ab, conv_w)


def _split(x):
    hi = x.astype(BF16)
    return hi, (x - hi.astype(F32)).astype(BF16)


def _dot16(a, b):
    return _dot(a.astype(BF16), b.astype(BF16))


def _dot_hi(a, b):
    ah, al = _split(a)
    bh, bl = _split(b)
    return _dot(ah, bh) + (_dot(ah, bl) + _dot(al, bh))


def _gdn_scan_kernel(reverse, nb, q_ref, k_ref, v_ref, ab_ref, cf_ref, o_ref,
                     s_ref, gc_ref, gct_ref, beta_ref, u_ref, wq_ref, kd_ref, at_ref):
    nchunk = TM // GDN_CHUNK
    c64 = GDN_CHUNK
    scale = GDN_DK ** -0.5
    lane0 = 2 * GDN_HEADS if reverse else 0
    chains = [(b, h) for b in range(nb) for h in range(GDN_HEADS)]

    @pl.when(pl.program_id(0) == 0)
    def _():
        s_ref[...] = jnp.zeros_like(s_ref)

    row = lax.broadcasted_iota(jnp.int32, (TM, TM), 0)
    col = lax.broadcasted_iota(jnp.int32, (TM, TM), 1)
    same = (row // c64) == (col // c64)
    tri = (col >= row) if reverse else (col <= row)
    cum = jnp.where(jnp.logical_and(same, tri), 1.0, 0.0)
    for b in range(nb):
        ab = ab_ref[b]
        beta_ref[b] = jax.nn.sigmoid(ab)
        x = ab + cf_ref[1:2, :]
        softplus = jnp.maximum(x, 0.0) + jnp.log1p(jnp.exp(-jnp.abs(x)))
        g = -cf_ref[0:1, :] * softplus
        gc = _dot_hi(cum, g)
        gc_ref[b] = gc
        for c in range(nchunk):
            blk = gc[c * c64:(c + 1) * c64, :]
            gct_ref[b, c] = jnp.concatenate([blk, blk], axis=0).T

    ri = lax.broadcasted_iota(jnp.int32, (c64, LANES), 0)
    ln = lax.broadcasted_iota(jnp.int32, (c64, LANES), 1)
    ci = ln % c64
    left = ln < c64
    causal = (ri <= ci) if reverse else (ri >= ci)
    strict = (ri < ci) if reverse else (ri > ci)
    blk16 = (ri // 16) == (ci // 16)
    blk32 = (ri // 32) == (ci // 32)
    m_c1 = jnp.logical_and(strict, jnp.logical_and(blk32, jnp.logical_not(blk16)))[:, :c64]
    m_c2 = jnp.logical_and(strict, jnp.logical_not(blk32))[:, :c64]
    m_l0x = jnp.logical_and(jnp.logical_and(strict, blk16), jnp.logical_not(left))
    eye_l = jnp.where(jnp.logical_and(ri == ci, left), 1.0, 0.0)
    eye64 = eye_l[:, :c64]
    strict64 = strict[:, :c64]
    zpad = jnp.zeros((c64, LANES), F32)

    def prep(it, carry):
        st = []
        for b in range(nb):
            for j in range(PREP_CHUNKS):
                c = it * PREP_CHUNKS + j
                r0 = pl.multiple_of(c * c64, c64)
                rows = pl.ds(r0, c64)
                r_last = r0 if reverse else r0 + (c64 - 1)
                gcb = gc_ref[b, rows, :]
                gctb = gct_ref[b, c]
                glb = gc_ref[b, pl.ds(r_last, 1), :]
                betab = beta_ref[b, rows, :]
                for h in range(GDN_HEADS):
                    hs = slice(h * GDN_DK, (h + 1) * GDN_DK)
                    gl = lane0 + SUBLANES + h
                    k16 = k_ref[b, rows, hs]
                    k32 = k16.astype(F32)
                    beta = betab[:, lane0 + h:lane0 + h + 1]
                    gcol = gcb[:, gl:gl + 1]
                    dec2 = jnp.exp(jnp.minimum(gcol - gctb[gl:gl + 1, :], 0.0))
                    kb = k32 * beta
                    kk2 = _dot_nt(kb.astype(BF16), jnp.concatenate([k16, k16], axis=0))
                    ll2 = kk2 * dec2
                    st.append(dict(hs=hs, b=b, h=h, c=c, rows=rows, k32=k32, k16=k16, beta=beta, gcol=gcol,
                                   kb=kb, glast=glb[:, gl:gl + 1], dec=dec2[:, :c64],
                                   ll=ll2[:, :c64], x=jnp.where(m_l0x, -ll2, eye_l)))
        for _ in range(4):
            for s in st:
                x = s["x"]
                s["x"] = jnp.where(left, x, 0.0) + _dot16(x, jnp.concatenate([zpad, x], axis=0))
        for s in st:
            s["d"] = s["x"][:, :c64]
        for mask in (m_c1, m_c2):
            for s in st:
                s["t"] = _dot16(jnp.where(mask, s["ll"], 0.0), s["d"])
            for s in st:
                s["d"] = s["d"] - _dot16(s["d"], s["t"])
        for s in st:
            s["r"] = eye64 - s["d"] - _dot_hi(jnp.where(strict64, s["ll"], 0.0), s["d"])
        for s in st:
            s["d"] = s["d"] + _dot16(s["d"], s["r"])
        for s in st:
            eg = jnp.exp(s["gcol"])
            s["eg"] = eg
            v32 = v_ref[s["b"], s["rows"], s["hs"]].astype(F32)
            s["u"] = _dot_hi(s["d"], v32 * s["beta"])
            s["w"] = _dot16(s["d"], s["kb"] * eg)
        for s in st:
            b, h, hs, c, rows = s["b"], s["h"], s["hs"], s["c"], s["rows"]
            qs = q_ref[b, rows, hs].astype(F32) * scale
            attn = _dot_nt(qs.astype(BF16), s["k16"]) * jnp.where(causal[:, :c64], s["dec"], 0.0)
            u_ref[b, rows, hs] = s["u"]
            wq_ref[b, c, h, 0:c64, :] = s["w"].astype(BF16)
            wq_ref[b, c, h, c64:2 * c64, :] = (qs * s["eg"]).astype(BF16)
            kd_ref[b, rows, hs] = (s["k32"] * jnp.exp(s["glast"] - s["gcol"])).astype(BF16)
            at_ref[b, c, h] = attn.astype(BF16)
        return carry

    lax.fori_loop(0, nchunk // PREP_CHUNKS, prep, 0)

    def scan(step, carry):
        c = (nchunk - 1 - step) if reverse else step
        r0 = pl.multiple_of(c * c64, c64)
        rows = pl.ds(r0, c64)
        r_last = r0 if reverse else r0 + (c64 - 1)
        glb = [gc_ref[b, pl.ds(r_last, 1), :] for b in range(nb)]
        st = []
        for (b, h) in chains:
            s = s_ref[b, h]
            st.append(dict(b=b, h=h, hs=slice(h * GDN_DK, (h + 1) * GDN_DK), s=s,
                           ws=_dot(wq_ref[b, c, h], s.astype(BF16))))
        for d in st:
            v_new = u_ref[d["b"], rows, d["hs"]] - d["ws"][:c64]
            d["vn"] = v_new.astype(BF16)
        for d in st:
            b, h, hs = d["b"], d["h"], d["hs"]
            o = d["ws"][c64:] + _dot(at_ref[b, c, h], d["vn"])
            o_ref[b, rows, hs] = o.astype(BF16)
            egl = jnp.exp(glb[b][:, lane0 + SUBLANES + h:lane0 + SUBLANES + h + 1])
            s_ref[b, h] = d["s"] * egl + _dot_tn(kd_ref[b, rows, hs], d["vn"])
        return carry

    lax.fori_loop(0, nchunk, scan, 0)


def _gdn_scan(q, k, v, ab, cf, reverse):
    b, t, c = q.shape
    nt = t // TM
    nchunk = TM // GDN_CHUNK

    def tile(s):
        lat = (nt - 1 - s) if reverse else (s - 1)
        return jnp.where(s == 0, nt - 1, lat)

    spec = pl.BlockSpec((b, TM, c), lambda s: (0, tile(s), 0))
    return pl.pallas_call(
        functools.partial(_gdn_scan_kernel, reverse, b),
        grid=(nt,),
        in_specs=[spec, spec, spec,
                  pl.BlockSpec((b, TM, LANES), lambda s: (0, tile(s), 0)),
                  _full_spec((SUBLANES, LANES))],
        out_specs=spec,
        out_shape=jax.ShapeDtypeStruct((b, t, c), BF16),
        scratch_shapes=[pltpu.VMEM((b, GDN_HEADS, GDN_DK, GDN_DV), F32),
                        pltpu.VMEM((b, TM, LANES), F32),
                        pltpu.VMEM((b, nchunk, LANES, LANES), F32),
                        pltpu.VMEM((b, TM, LANES), F32),
                        pltpu.VMEM((b, TM, c), F32),
                        pltpu.VMEM((b, nchunk, GDN_HEADS, 2 * GDN_CHUNK, GDN_DK), BF16),
                        pltpu.VMEM((b, TM, c), BF16),
                        pltpu.VMEM((b, nchunk, GDN_HEADS, GDN_CHUNK, GDN_CHUNK), BF16)],
        compiler_params=_cparams(("arbitrary",)),
        name="gdn_scan_bwd" if reverse else "gdn_scan_fwd",
    )(q, k, v, ab, cf)


def _gdn_out_kernel(of_ref, ob_ref, gate_ref, h_ref, mod_ref, on_ref, w_ref, g_ref, out_ref, y_ref):
    for hh in range(GDN_HEADS):
        hs = slice(hh * GDN_DV, (hh + 1) * GDN_DV)
        o = of_ref[:, hs].astype(F32) + ob_ref[:, hs].astype(F32)
        gt = gate_ref[:, hs].astype(F32)
        y_ref[:, hs] = (_rms(o) * on_ref[...] * _silu(gt)).astype(BF16)
    out = _dot(y_ref[...], w_ref[...])
    out_ref[...] = _residual(h_ref[...], out, g_ref[...], mod_ref[2:3, :])


def _gdn_out(o_f, o_b, gate, h, mod, o_norm, w_out, g):
    b, t, d = h.shape
    nt = t // TM
    c = o_f.shape[-1]
    return pl.pallas_call(
        _gdn_out_kernel,
        grid=(b, nt),
        in_specs=[_tok_spec(c), _tok_spec(c), _tok_spec(c), _tok_spec(d), _mod_spec(nt, d),
                  _full_spec((1, GDN_DV)), _full_spec(w_out.shape), _full_spec((1, d))],
        out_specs=_tok_spec(d),
        out_shape=jax.ShapeDtypeStruct((b, t, d), F32),
        scratch_shapes=[pltpu.VMEM((TM, c), BF16)],
        compiler_params=_cparams(("parallel", "parallel")),
        name="gdn_out",
    )(o_f, o_b, gate, h, mod, o_norm, w_out, g)


def _conf_kernel(nt, h_ref, hp_ref, hn_ref, mod_ref, g0_ref, win_ref, bin_ref, dw_ref, dwb_ref,
                 lng_ref, lnb_ref, w_ref, b_ref, g1_ref, out_ref, ext_ref, sh_ref, y_ref):
    prev_ok, next_ok = _halo_valid(nt)
    g0, shift, scale = g0_ref[...], mod_ref[0:1, :], mod_ref[1:2, :]
    h = h_ref[...]
    ax16 = jnp.concatenate([_norm_mod(hp_ref[...], g0, shift, scale), _norm_mod(h, g0, shift, scale),
                            _norm_mod(hn_ref[...], g0, shift, scale)], axis=0).astype(BF16)
    d = h.shape[-1]
    cw = 512
    for n in range(d // cw):
        cs = slice(n * cw, (n + 1) * cw)
        gs = slice(d + n * cw, d + (n + 1) * cw)
        u = (_dot(ax16, win_ref[:, cs]) + bin_ref[:, cs]) * jax.nn.sigmoid(_dot(ax16, win_ref[:, gs]) + bin_ref[:, gs])
        ext_ref[0:HALO, cs] = jnp.where(prev_ok, u[0:HALO], 0.0)
        ext_ref[HALO:HALO + TM, cs] = u[HALO:HALO + TM]
        ext_ref[HALO + TM:2 * HALO + TM, cs] = jnp.where(next_ok, u[HALO + TM:], 0.0)
    pad = CONF_CONV // 2
    for r in range(1, SUBLANES):
        sh_ref[r - 1] = ext_ref[pl.ds(r, sh_ref.shape[1]), :]
    rw, cw = 64, 256
    for cc in range(d // cw):
        cs = slice(cc * cw, (cc + 1) * cw)
        for rr in range(TM // rw):
            acc = None
            for tap in range(CONF_CONV):
                off = HALO - pad + tap
                r = off % SUBLANES
                src = ext_ref if r == 0 else sh_ref.at[r - 1]
                term = dw_ref[tap:tap + 1, cs] * src[pl.ds(off - r + rr * rw, rw), cs]
                acc = term if acc is None else acc + term
            y_ref[rr * rw:(rr + 1) * rw, cs] = acc + dwb_ref[:, cs]
    y = y_ref[...]
    yc = y - jnp.mean(y, axis=-1, keepdims=True)
    yn = yc * lax.rsqrt(jnp.mean(yc * yc, axis=-1, keepdims=True) + LN_EPS) * lng_ref[...] + lnb_ref[...]
    out = _dot(_silu(yn).astype(BF16), w_ref[...]) + b_ref[...]
    out_ref[...] = _residual(h, out, g1_ref[...], mod_ref[2:3, :])


def _conformer(h, mod, g0, w_in, b_in, dw, dw_b, ln_g, ln_b, w_out, b_out, g1):
    b, t, d = h.shape
    nt = t // TM
    prev, nxt = _halo_specs(nt, d)
    vec = _full_spec((1, d))
    return pl.pallas_call(
        functools.partial(_conf_kernel, nt),
        grid=(b, nt),
        in_specs=[_tok_spec(d), prev, nxt, _mod_spec(nt, d), vec, _full_spec(w_in.shape), _full_spec(b_in.shape),
                  _full_spec(dw.shape), vec, vec, vec, _full_spec(w_out.shape), vec, vec],
        out_specs=_tok_spec(d),
        out_shape=jax.ShapeDtypeStruct((b, t, d), F32),
        scratch_shapes=[pltpu.VMEM((TM + 2 * HALO, d), F32),
                        pltpu.VMEM((SUBLANES - 1, TM + 2 * HALO - SUBLANES, d), F32),
                        pltpu.VMEM((TM, d), F32)],
        compiler_params=_cparams(("parallel", "parallel")),
        name="conformer",
    )(h, h, h, mod, g0, w_in, b_in, dw, dw_b, ln_g, ln_b, w_out, b_out, g1)


def _mla_proj_kernel(h_ref, mod_ref, g_ref, win_ref, qn_ref, kvn_ref, wuq_ref, wukv_ref, cs_ref,
                     q_ref, k_ref, v_ref):
    a16 = _norm_mod(h_ref[...], g_ref[...], mod_ref[0:1, :], mod_ref[1:2, :]).astype(BF16)
    p = _dot(a16, win_ref[...])
    cq = _rms(p[:, :MLA_Q_RANK]) * qn_ref[...]
    ckv = _rms(p[:, MLA_Q_RANK:MLA_Q_RANK + MLA_KV_RANK]) * kvn_ref[...]
    kr = p[:, MLA_Q_RANK + MLA_KV_RANK:]
    cs = cs_ref[...]
    kx = kr * cs
    krot = (kx + pltpu.roll(kx, MLA_ROPE, 1)).astype(BF16)
    scale = (MLA_NOPE + MLA_ROPE) ** -0.5 * LOG2E
    cq16 = cq.astype(BF16)
    ckv16 = ckv.astype(BF16)
    hw = MLA_NOPE + 2 * MLA_ROPE
    for hh in range(MLA_HEADS):
        qh = _dot(cq16, wuq_ref[:, hh * hw:(hh + 1) * hw]) * scale
        q_ref[:, hh * hw:hh * hw + MLA_NOPE] = qh[:, :MLA_NOPE].astype(BF16)
        q_ref[:, hh * hw + MLA_NOPE:(hh + 1) * hw] = (qh[:, MLA_NOPE:] * cs).astype(BF16)
        k_ref[:, hh * hw:hh * hw + MLA_NOPE] = _dot(
            ckv16, wukv_ref[:, hh * MLA_NOPE:(hh + 1) * MLA_NOPE]).astype(BF16)
        k_ref[:, hh * hw + MLA_NOPE:(hh + 1) * hw] = krot
        nk = MLA_HEADS * MLA_NOPE
        v_ref[:, 2 * hh * MLA_V:(2 * hh + 1) * MLA_V] = _dot(
            ckv16, wukv_ref[:, nk + hh * MLA_V:nk + (hh + 1) * MLA_V]).astype(BF16)
        v_ref[:, (2 * hh + 1) * MLA_V:(2 * hh + 2) * MLA_V] = jnp.ones((TM, MLA_V), BF16)


def _mla_proj(h, mod, g, w_in, q_norm, kv_norm, w_uq, w_ukv, cs_tab):
    b, t, d = h.shape
    nt = t // TM
    hw = MLA_NOPE + 2 * MLA_ROPE
    return pl.pallas_call(
        _mla_proj_kernel,
        grid=(b, nt),
        in_specs=[_tok_spec(d), _mod_spec(nt, d), _full_spec((1, d)), _full_spec(w_in.shape),
                  _full_spec(q_norm.shape), _full_spec(kv_norm.shape),
                  _full_spec(w_uq.shape), _full_spec(w_ukv.shape),
                  pl.BlockSpec((TM, LANES), lambda bb, j: (j, 0))],
        out_specs=[_tok_spec(MLA_HEADS * hw), _tok_spec(MLA_HEADS * hw), _tok_spec(MLA_HEADS * 2 * MLA_V)],
        out_shape=[jax.ShapeDtypeStruct((b, t, MLA_HEADS * hw), BF16),
                   jax.ShapeDtypeStruct((b, t, MLA_HEADS * hw), BF16),
                   jax.ShapeDtypeStruct((b, t, MLA_HEADS * 2 * MLA_V), BF16)],
        compiler_params=_cparams(("parallel", "parallel")),
        name="mla_proj",
    )(h, mod, g, w_in, q_norm, kv_norm, w_uq, w_ukv, cs_tab)


def _attn_kernel(tk, q_ref, k_ref, v_ref, o_ref, m_ref, acc_ref):
    q = q_ref[...]
    for i in range(k_ref.shape[0] // tk):
        rows = slice(i * tk, (i + 1) * tk)
        s = _dot_nt(q, k_ref[rows, :])
        m_blk = jnp.max(s, axis=-1, keepdims=True)
        if i == 0:
            m_new = m_blk
            acc_ref[...] = _dot(jnp.exp2(s - m_new).astype(BF16), v_ref[rows, :])
        else:
            m_old = m_ref[...]
            m_new = jnp.maximum(m_old, m_blk)
            acc_ref[...] = jnp.exp2(m_old - m_new) * acc_ref[...] + _dot(
                jnp.exp2(s - m_new).astype(BF16), v_ref[rows, :])
        m_ref[...] = m_new
    acc = acc_ref[...]
    o_ref[...] = (acc[:, :MLA_V] / acc[:, MLA_V:]).astype(BF16)


def _attention(q, k, v, tq, q_rows, q_row0, k_rows, k_row0, name):
    b = q.shape[0]
    hw = MLA_NOPE + 2 * MLA_ROPE
    tk = next(c for c in (768, 512, 256) if k_rows % c == 0)
    q0 = q_row0 // tq
    k0 = k_row0 // k_rows
    return pl.pallas_call(
        functools.partial(_attn_kernel, tk),
        grid=(b, MLA_HEADS, q_rows // tq),
        in_specs=[pl.BlockSpec((None, tq, hw), lambda bb, hh, i: (bb, q0 + i, hh)),
                  pl.BlockSpec((None, k_rows, hw), lambda bb, hh, i: (bb, k0, hh)),
                  pl.BlockSpec((None, k_rows, 2 * MLA_V), lambda bb, hh, i: (bb, k0, hh))],
        out_specs=pl.BlockSpec((None, tq, MLA_V), lambda bb, hh, i: (bb, i, hh)),
        out_shape=jax.ShapeDtypeStruct((b, q_rows, MLA_HEADS * MLA_V), BF16),
        scratch_shapes=[pltpu.VMEM((tq, 1), F32), pltpu.VMEM((tq, 2 * MLA_V), F32)],
        compiler_params=_cparams(("parallel", "parallel", "arbitrary")),
        name=name,
    )(q, k, v)


def _proj_out_kernel(o_ref, w_ref, h_ref, mod_ref, g_ref, out_ref):
    out = _dot(o_ref[...], w_ref[...])
    out_ref[...] = _residual(h_ref[...], out, g_ref[...], mod_ref[2:3, :])


def _proj_out(o, w, h, mod, g):
    b, t, d = h.shape
    nt = t // TM
    return pl.pallas_call(
        _proj_out_kernel,
        grid=(b, nt),
        in_specs=[_tok_spec(o.shape[-1]), _full_spec(w.shape), _tok_spec(d), _mod_spec(nt, d),
                  _full_spec((1, d))],
        out_specs=_tok_spec(d),
        out_shape=jax.ShapeDtypeStruct((b, t, d), F32),
        compiler_params=_cparams(("parallel", "parallel")),
        name="mla_out",
    )(o, w, h, mod, g)


def _ffn_kernel(fc, tps, nt, h_ref, mod_ref, g2_ref, wg_ref, wu_ref, wd_ref, g3_ref, out_ref, acc_ref):
    b = pl.program_id(0)
    j0 = pl.program_id(1) * tps
    hs = [h_ref[s * TM:(s + 1) * TM, :] for s in range(tps)]
    mods = [mod_ref[jnp.where(j0 + s == nt - 1, 0, b + 1)] for s in range(tps)]
    a16 = jnp.concatenate([_norm_mod(hs[s], g2_ref[...], mods[s][3:4, :], mods[s][4:5, :])
                           for s in range(tps)], axis=0).astype(BF16)
    for c in range(wg_ref.shape[-1] // fc):
        fs = slice(c * fc, (c + 1) * fc)
        act = (_silu(_dot(a16, wg_ref[:, fs])) * _dot(a16, wu_ref[:, fs])).astype(BF16)
        part = _dot(act, wd_ref[fs, :])
        if c == 0:
            acc_ref[...] = part
        else:
            acc_ref[...] += part
    for s in range(tps):
        rows = slice(s * TM, (s + 1) * TM)
        out_ref[rows, :] = _residual(hs[s], acc_ref[rows, :], g3_ref[...], mods[s][5:6, :])


def _ffn(h, mod, g2, w_gate, w_up, w_down, g3, n_tiles):
    b, t, d = h.shape
    nt = t // TM
    tps = 3 if n_tiles % 3 == 0 else 1
    tok = pl.BlockSpec((None, tps * TM, d), lambda bb, j: (bb, j, 0))

    def resident(shape):
        return pl.BlockSpec(shape, lambda *_: (0,) * len(shape), pipeline_mode=pl.Buffered(1))

    return pl.pallas_call(
        functools.partial(_ffn_kernel, 1408, tps, nt),
        grid=(b, n_tiles // tps),
        in_specs=[tok, _full_spec(mod.shape), _full_spec((1, d)), resident(w_gate.shape),
                  resident(w_up.shape), resident(w_down.shape), _full_spec((1, d))],
        out_specs=tok,
        out_shape=jax.ShapeDtypeStruct((b, n_tiles * TM, d), F32),
        scratch_shapes=[pltpu.VMEM((tps * TM, d), F32)],
        compiler_params=_cparams(("parallel", "parallel")),
        name="ffn",
    )(h, mod, g2, w_gate, w_up, w_down, g3)


def _moe_route_kernel(h_ref, mod_ref, g2_ref, wr_ref, a_ref, rt_ref):
    lane = lax.broadcasted_iota(jnp.int32, (TM, LANES), 1).astype(F32)
    a = _norm_mod(h_ref[...], g2_ref[...], mod_ref[3:4, :], mod_ref[4:5, :])
    for s in range(SUBLANES):
        a_ref[pl.ds(s, TM, stride=SUBLANES), :] = a[:, s * LANES:(s + 1) * LANES]
    logits = jnp.where(lane < N_EXPERTS, _dot_hi(a, wr_ref[...]), -jnp.inf)
    m1 = jnp.max(logits, axis=-1, keepdims=True)
    i1 = jnp.min(jnp.where(logits == m1, lane, float(LANES)), axis=-1, keepdims=True)
    rest = jnp.where(lane == i1, -jnp.inf, logits)
    m2 = jnp.max(rest, axis=-1, keepdims=True)
    i2 = jnp.min(jnp.where(rest == m2, lane, float(LANES)), axis=-1, keepdims=True)
    e2 = jnp.exp(m2 - m1)
    w1 = 1.0 / (1.0 + e2)
    rt_ref[...] = (jnp.where(lane == 0.0, i1, 0.0) + jnp.where(lane == 1.0, i2, 0.0)
                   + jnp.where(lane == 2.0, w1, 0.0) + jnp.where(lane == 3.0, e2 * w1, 0.0))


def _moe_route(h, mod, g2, w_router, n_tiles):
    b, t, d = h.shape
    nt = t // TM
    return pl.pallas_call(
        _moe_route_kernel,
        grid=(b, n_tiles),
        in_specs=[_tok_spec(d), _mod_spec(nt, d), _full_spec((1, d)), _full_spec(w_router.shape)],
        out_specs=[pl.BlockSpec((TM * SUBLANES, LANES), lambda bb, j: (bb * n_tiles + j, 0)),
                   _tok_spec(LANES)],
        out_shape=[jax.ShapeDtypeStruct((b * n_tiles * TM * SUBLANES, LANES), F32),
                   jax.ShapeDtypeStruct((b, n_tiles * TM, LANES), F32)],
        compiler_params=_cparams(("parallel", "parallel")),
        name="moe_route",
    )(h, mod, g2, w_router)


def _gather_tiles(idx_ref, base, n, src_hbm, dst, sem):
    def body(r, carry):
        row = pl.multiple_of(idx_ref[base + r], SUBLANES)
        pltpu.make_async_copy(src_hbm.at[pl.ds(row, SUBLANES), :],
                              dst.at[pl.ds(pl.multiple_of(r * SUBLANES, SUBLANES), SUBLANES), :], sem).start()
        return carry
    lax.fori_loop(0, n, body, 0, unroll=8)


def _wait_tiles(n, src_hbm, dst, sem):
    pltpu.make_async_copy(src_hbm.at[pl.ds(0, n * SUBLANES), :], dst, sem).wait()


def _untile(buf, n):
    return jnp.concatenate([buf[pl.ds(s, n, stride=SUBLANES), :] for s in range(SUBLANES)], axis=1)


def _moe_dispatch_kernel(dst_ref, last_ref, nact_ref, a_ref, x_hbm, stage_ref, sem, zsem):
    i = pl.program_id(0)
    n = pl.num_programs(0)
    rows = TM * SUBLANES
    slot = i % 2
    n_tiles_max = x_hbm.shape[0] // rows

    @pl.when(i == 0)
    def _():
        zero = stage_ref.at[1]
        zero[...] = jnp.zeros_like(zero)

        def zero_tile(t):
            return pltpu.make_async_copy(zero, x_hbm.at[pl.ds(pl.multiple_of(t * rows, rows), rows), :], zsem)

        for e in range(N_EXPERTS):
            zero_tile(last_ref[e]).start()
            pl.when(nact_ref[0] + e < n_tiles_max)(lambda e=e: zero_tile(nact_ref[0] + e).start())
        for e in range(N_EXPERTS):
            zero_tile(0).wait()
            pl.when(nact_ref[0] + e < n_tiles_max)(lambda: zero_tile(0).wait())

    stage = stage_ref.at[slot]
    stage[...] = a_ref[...]

    def body(r, carry):
        src = pl.multiple_of(r * SUBLANES, SUBLANES)
        for k in range(2):
            dst = pl.multiple_of(dst_ref[(i * TM + r) * 2 + k], SUBLANES)
            pltpu.make_async_copy(stage.at[pl.ds(src, SUBLANES), :], x_hbm.at[pl.ds(dst, SUBLANES), :],
                                  sem.at[slot]).start()
        return carry
    lax.fori_loop(0, TM, body, 0, unroll=4)

    def wait(sl):
        for k in range(2):
            pltpu.make_async_copy(stage_ref.at[sl], x_hbm.at[pl.ds(0, rows), :], sem.at[sl]).wait()

    @pl.when(i > 0)
    def _():
        wait(1 - slot)

    @pl.when(i == n - 1)
    def _():
        wait(slot)


def _moe_dispatch(dst_rows, last_tile, n_active, a2d, n_rows):
    n_steps = a2d.shape[0] // (TM * SUBLANES)
    return pl.pallas_call(
        _moe_dispatch_kernel,
        grid_spec=pltpu.PrefetchScalarGridSpec(
            num_scalar_prefetch=3,
            grid=(n_steps,),
            in_specs=[pl.BlockSpec((TM * SUBLANES, LANES), lambda i, ds, lt, na: (i, 0))],
            out_specs=pl.BlockSpec(memory_space=pl.ANY),
            scratch_shapes=[pltpu.VMEM((2, TM * SUBLANES, LANES), F32), pltpu.SemaphoreType.DMA((2,)),
                            pltpu.SemaphoreType.DMA(())]),
        out_shape=jax.ShapeDtypeStruct((n_rows * SUBLANES, LANES), F32),
        compiler_params=_cparams(("arbitrary",)),
        name="moe_dispatch",
    )(dst_rows, last_tile, n_active, a2d)


def _moe_experts_kernel(te_ref, nact_ref, x_ref, wg_ref, wu_ref, wd_ref, y_ref, wg16, wu16, wd16):
    i = pl.program_id(0)

    @pl.when(jnp.logical_or(i == 0, te_ref[i] != te_ref[jnp.maximum(i - 1, 0)]))
    def _():
        wg16[...] = wg_ref[...].astype(BF16)
        wu16[...] = wu_ref[...].astype(BF16)
        wd16[...] = wd_ref[...].astype(BF16)

    @pl.when(i < nact_ref[0])
    def _():
        x16 = _untile(x_ref, TM).astype(BF16)
        f = wg_ref.shape[-1]
        acc = None
        for c0 in range(0, f, 1408):
            fs = slice(c0, min(c0 + 1408, f))
            act = (_silu(_dot(x16, wg16[:, fs])) * _dot(x16, wu16[:, fs])).astype(BF16)
            part = _dot(act, wd16[fs, :])
            acc = part if acc is None else acc + part
        for s in range(SUBLANES):
            y_ref[pl.ds(s, TM, stride=SUBLANES), :] = acc[:, s * LANES:(s + 1) * LANES]

    @pl.when(i >= nact_ref[0])
    def _():
        y_ref[...] = jnp.zeros_like(y_ref)


def _moe_experts(tile_expert, n_active, x2d, w_gate, w_up, w_down, layer):
    n_tiles = tile_expert.shape[0]
    _, _, d, f = w_gate.shape
    wspec_in = pl.BlockSpec((None, None, d, f), lambda i, te, na: (layer, te[i], 0, 0))
    wspec_out = pl.BlockSpec((None, None, f, d), lambda i, te, na: (layer, te[i], 0, 0))
    return pl.pallas_call(
        _moe_experts_kernel,
        grid_spec=pltpu.PrefetchScalarGridSpec(
            num_scalar_prefetch=2,
            grid=(n_tiles,),
            in_specs=[pl.BlockSpec((TM * SUBLANES, LANES), lambda i, te, na: (jnp.minimum(i, na[0] - 1), 0)),
                      wspec_in, wspec_in, wspec_out],
            out_specs=pl.BlockSpec((TM * SUBLANES, LANES), lambda i, te, na: (i, 0)),
            scratch_shapes=[pltpu.VMEM((d, f), BF16), pltpu.VMEM((d, f), BF16), pltpu.VMEM((f, d), BF16)]),
        out_shape=jax.ShapeDtypeStruct((n_tiles * TM * SUBLANES, LANES), F32),
        compiler_params=_cparams(("arbitrary",)),
        name="moe_experts",
    )(tile_expert, n_active, x2d, w_gate, w_up, w_down)


def _moe_combine_kernel(n_tiles, dst_ref, y_hbm, rt_ref, h_ref, mod_ref, g3_ref, out_ref, ybuf, sem):
    step = pl.program_id(0) * n_tiles + pl.program_id(1)
    nsteps = pl.num_programs(0) * n_tiles

    @pl.when(step == 0)
    def _():
        _gather_tiles(dst_ref, 0, 2 * TM, y_hbm, ybuf.at[0], sem.at[0])

    @pl.when(step + 1 < nsteps)
    def _():
        nxt = (step + 1) % 2
        _gather_tiles(dst_ref, (step + 1) * 2 * TM, 2 * TM, y_hbm, ybuf.at[nxt], sem.at[nxt])

    cur = step % 2
    _wait_tiles(2 * TM, y_hbm, ybuf.at[cur], sem.at[cur])
    yb = ybuf.at[cur]
    w1 = rt_ref[:, 2:3]
    w2 = rt_ref[:, 3:4]
    f = jnp.concatenate(
        [w1 * yb[pl.ds(s, TM, stride=2 * SUBLANES), :] + w2 * yb[pl.ds(SUBLANES + s, TM, stride=2 * SUBLANES), :]
         for s in range(SUBLANES)], axis=1)
    out_ref[...] = _residual(h_ref[...], f, g3_ref[...], mod_ref[5:6, :])


def _moe_combine(dst_rows, y2d, rt, h, mod, g3, n_tiles):
    b, t, d = h.shape
    nt = t // TM
    tok = lambda c: pl.BlockSpec((None, TM, c), lambda bb, j, ds: (bb, j, 0))
    return pl.pallas_call(
        functools.partial(_moe_combine_kernel, n_tiles),
        grid_spec=pltpu.PrefetchScalarGridSpec(
            num_scalar_prefetch=1,
            grid=(b, n_tiles),
            in_specs=[pl.BlockSpec(memory_space=pl.ANY), tok(LANES), tok(d),
                      pl.BlockSpec((None, SUBLANES, d),
                                   lambda bb, j, ds: (jnp.where(j == nt - 1, 0, bb + 1), 0, 0)),
                      pl.BlockSpec((1, d), lambda bb, j, ds: (0, 0))],
            out_specs=tok(d),
            scratch_shapes=[pltpu.VMEM((2, 2 * TM * SUBLANES, LANES), F32), pltpu.SemaphoreType.DMA((2,))]),
        out_shape=jax.ShapeDtypeStruct((b, n_tiles * TM, d), F32),
        compiler_params=_cparams(("arbitrary", "arbitrary")),
        name="moe_combine",
    )(dst_rows, y2d, rt, h, mod, g3)


def _moe(h, mod, g2, w_router, w_gate, w_up, w_down, layer, g3, n_tiles):
    b = h.shape[0]
    n_tok = b * n_tiles * TM
    a2d, rt = _moe_route(h, mod, g2, w_router, n_tiles)
    expert = rt.reshape(n_tok, LANES)[:, :2].astype(jnp.int32).reshape(-1)
    onehot = (expert[:, None] == jnp.arange(N_EXPERTS)[None, :]).astype(jnp.int32)
    rank = jnp.sum((jnp.cumsum(onehot, axis=0) - onehot) * onehot, axis=1)
    count = jnp.sum(onehot, axis=0)
    padded = (count + TM - 1) // TM * TM
    ends = jnp.cumsum(padded)
    dest = (ends - padded)[expert] + rank
    n_tiles_max = 2 * n_tok // TM + N_EXPERTS
    tile_start = jnp.arange(n_tiles_max, dtype=jnp.int32) * TM
    tile_expert = jnp.minimum(jnp.sum(ends[None, :] <= tile_start[:, None], axis=1), N_EXPERTS - 1)
    n_active = (ends[-1] // TM).reshape(1).astype(jnp.int32)
    last_tile = jnp.maximum(ends // TM - 1, 0).astype(jnp.int32)
    dst_rows = (dest * SUBLANES).astype(jnp.int32)
    x2d = _moe_dispatch(dst_rows, last_tile, n_active, a2d, n_tiles_max * TM)
    y2d = _moe_experts(tile_expert.astype(jnp.int32), n_active, x2d, w_gate, w_up, w_down, layer)
    return _moe_combine(dst_rows, y2d, rt, h, mod, g3, n_tiles)


def _pad_rows(a, rows):
    return jnp.pad(a, ((0, rows - a.shape[0]),) + ((0, 0),) * (a.ndim - 1))


def _rope_table(seq, ctx_len):
    n_freq = MLA_ROPE // 4
    t = jnp.arange(seq)
    pos = jnp.stack([(t // GRID_W).astype(F32), (t % GRID_W).astype(F32)], axis=1)
    inv = ROPE_THETA ** (-jnp.arange(n_freq, dtype=F32) / n_freq)
    ang = pos[:, :, None] * inv
    cos = jnp.repeat(jnp.cos(ang)[:, :, None, :], 2, axis=2).reshape(seq, MLA_ROPE)
    sin = jnp.stack([-jnp.sin(ang), jnp.sin(ang)], axis=2).reshape(seq, MLA_ROPE)
    lat = jnp.concatenate([cos, sin], axis=1)
    ctx = jnp.concatenate([jnp.ones((ctx_len, MLA_ROPE), F32), jnp.zeros((ctx_len, MLA_ROPE), F32)], axis=1)
    return jnp.concatenate([lat, ctx], axis=0)


def _swap_halves(w):
    shp = w.shape
    return jnp.flip(w.reshape(shp[:-1] + (2, 2, MLA_ROPE // 4)), axis=-2).reshape(shp)


def kernel(x, c, ctx, c_ctx, w_mod, b_mod, norm_g, gdn_w_in, gdn_conv, gdn_a_log, gdn_dt_bias, gdn_o_norm, gdn_w_out, conf_w_in, conf_b_in, conf_dw, conf_dw_b, conf_ln_g, conf_ln_b, conf_w_out, conf_b_out, mla_w_in, mla_q_norm, mla_kv_norm, mla_w_uq, mla_w_ukv, mla_w_o, ffn_w_gate, ffn_w_up, ffn_w_down, moe_router, moe_w_gate, moe_w_up, moe_w_down):
    bsz, seq, d = x.shape
    ctx_len = ctx.shape[1]
    depth = w_mod.shape[0]
    assert ctx_len == TM and seq % (4 * TM) == 0 and bsz + 1 <= SUBLANES
    nt = (seq + ctx_len) // TM
    n_lat = seq // TM

    cvec = _pad_rows(jnp.concatenate([c_ctx[None, :], c], axis=0), SUBLANES)
    mods = _adaln(cvec, w_mod, b_mod)
    mods = mods.reshape(depth, SUBLANES, N_MOD, d)[:, :bsz + 1]
    mods = jnp.pad(mods, ((0, 0), (0, 0), (0, SUBLANES - N_MOD), (0, 0)))

    h = jnp.concatenate([x, ctx], axis=1)
    rope_cs = _rope_table(seq, ctx_len)

    n_mix = 3
    for i in range(depth):
        last = i == depth - 1
        mod = mods[i]
        g = norm_g[i].reshape(4, 1, d)
        kind, j = i % n_mix, i // n_mix
        if kind == 0:
            nqk = GDN_HEADS * GDN_DK
            nmain = 2 * nqk + 2 * GDN_HEADS * GDN_DV
            w_in = gdn_w_in[j]
            w_main = w_in[:, :nmain].astype(BF16)
            w_tail = w_in[:, nmain:]
            nh = GDN_HEADS
            w_ab = jnp.concatenate([w_tail[:, (2 * part + dr) * nh:(2 * part + dr + 1) * nh]
                                    for dr in range(2) for part in range(2)], axis=1)
            w_ab = jnp.pad(w_ab, ((0, 0), (0, LANES - 4 * nh)))

            def decay_lanes(p):
                z = jnp.zeros((nh,), F32)
                return jnp.pad(jnp.concatenate([z, p[0], z, p[1]]), (0, LANES - 4 * nh))

            cf = _pad_rows(jnp.stack([decay_lanes(jnp.exp(gdn_a_log[j])), decay_lanes(gdn_dt_bias[j])]),
                           SUBLANES)
            q, k, v, gate, ab = _gdn_in(h, mod, g[0], w_main, w_ab, _pad_rows(gdn_conv[j], SUBLANES))
            o_f = _gdn_scan(q, k, v, ab, cf, reverse=False)
            o_b = _gdn_scan(q, k, v, ab, cf, reverse=True)
            h = _gdn_out(o_f, o_b, gate, h, mod, gdn_o_norm[j].reshape(1, -1),
                         gdn_w_out[j].astype(BF16), g[1])
        elif kind == 1:
            h = _conformer(h, mod, g[0], conf_w_in[j].astype(BF16), conf_b_in[j].reshape(1, -1),
                           _pad_rows(conf_dw[j], 32), conf_dw_b[j].reshape(1, -1),
                           conf_ln_g[j].reshape(1, -1), conf_ln_b[j].reshape(1, -1),
                           conf_w_out[j].astype(BF16), conf_b_out[j].reshape(1, -1), g[1])
        else:
            w_in = mla_w_in[j]
            w_kr = w_in[:, MLA_Q_RANK + MLA_KV_RANK:]
            w_in2 = jnp.concatenate([w_in, _swap_halves(w_kr)], axis=1).astype(BF16)
            hq = MLA_NOPE + MLA_ROPE
            w_uq = mla_w_uq[j].reshape(MLA_Q_RANK, MLA_HEADS, hq)
            w_uq2 = jnp.concatenate([w_uq, _swap_halves(w_uq[..., MLA_NOPE:])], axis=-1)
            w_uq2 = w_uq2.reshape(MLA_Q_RANK, -1).astype(BF16)
            w_ukv = mla_w_ukv[j].reshape(MLA_KV_RANK, MLA_HEADS, MLA_NOPE + MLA_V)
            w_ukv2 = jnp.concatenate([w_ukv[..., :MLA_NOPE].reshape(MLA_KV_RANK, -1),
                                      w_ukv[..., MLA_NOPE:].reshape(MLA_KV_RANK, -1)], axis=1).astype(BF16)
            qa, ka, va = _mla_proj(h, mod, g[0], w_in2, mla_q_norm[j].reshape(1, -1),
                                   mla_kv_norm[j].reshape(1, -1), w_uq2, w_ukv2, rope_cs)
            o_l = _attention(qa, ka, va, 4 * TM, seq, 0, seq + ctx_len, 0, "attn_latent")
            o_c = _attention(qa, ka, va, TM, ctx_len, seq, ctx_len, seq, "attn_context")
            o = jnp.concatenate([o_l, o_c], axis=1)
            h = _proj_out(o, mla_w_o[j].astype(BF16), h, mod, g[1])

        n_tiles = n_lat if last else nt
        f = i // 2
        if i % 2 == 0:
            h = _ffn(h, mod, g[2], ffn_w_gate[f].astype(BF16), ffn_w_up[f].astype(BF16),
                     ffn_w_down[f].astype(BF16), g[3], n_tiles)
        else:
            w_r = jnp.pad(moe_router[f], ((0, 0), (0, LANES - N_EXPERTS)))
            h = _moe(h, mod, g[2], w_r, moe_w_gate, moe_w_up, moe_w_down, f, g[3], n_tiles)
    return h
```

```python
import functools

import jax
import jax.numpy as jnp
from jax import lax
from jax.experimental import pallas as pl
from jax.experimental.pallas import tpu as pltpu

F32 = jnp.float32
BF16 = jnp.bfloat16

N_MOD = 6
RMS_EPS = 1e-6
LN_EPS = 1e-5
GRID_W = 64
GDN_HEADS = 8
GDN_DK = 128
GDN_DV = 128
GDN_CONV = 5
GDN_CHUNK = 64
CONF_CONV = 31
MLA_HEADS = 8
MLA_NOPE = 128
MLA_ROPE = 64
MLA_V = 128
MLA_Q_RANK = 512
MLA_KV_RANK = 256
ROPE_THETA = 10000.0
LOG2E = 1.4426950408889634
N_EXPERTS = 8

LANES = 128
SUBLANES = 8
TM = 256
HALO = 16
PREP_CHUNKS = 2
VMEM_LIMIT = 56 * 1024 * 1024


def _cparams(sem):
    return pltpu.CompilerParams(dimension_semantics=sem, vmem_limit_bytes=VMEM_LIMIT)


def _dot(a, b, **kw):
    return jnp.dot(a, b, preferred_element_type=F32, **kw)


def _dot_nt(a, b):
    return lax.dot_general(a, b, (((1,), (1,)), ((), ())), preferred_element_type=F32)


def _dot_tn(a, b):
    return lax.dot_general(a, b, (((0,), (0,)), ((), ())), preferred_element_type=F32)


def _silu(x):
    return x * jax.nn.sigmoid(x)


def _rms(x, eps=RMS_EPS):
    return x * lax.rsqrt(jnp.mean(x * x, axis=-1, keepdims=True) + eps)


def _norm_mod(h, g, shift, scale):
    return _rms(h) * g * (1.0 + scale) + shift


def _residual(h, o, g, gate):
    return h + gate * (_rms(o) * g)


def _tok_spec(c):
    return pl.BlockSpec((None, TM, c), lambda b, j: (b, j, 0))


def _full_spec(shape):
    nd = len(shape)
    return pl.BlockSpec(shape, lambda *_: (0,) * nd)


def _mod_spec(nt, d):
    return pl.BlockSpec((None, SUBLANES, d), lambda b, j: (jnp.where(j == nt - 1, 0, b + 1), 0, 0))


def _halo_specs(nt, c):
    per = TM // HALO
    prev = pl.BlockSpec((None, HALO, c), lambda b, j: (b, jnp.maximum(j * per - 1, 0), 0))
    nxt = pl.BlockSpec((None, HALO, c), lambda b, j: (b, jnp.minimum((j + 1) * per, nt * per - 1), 0))
    return prev, nxt


def _halo_valid(nt):
    j = pl.program_id(1)
    prev_ok = jnp.logical_and(j != 0, j != nt - 1)
    next_ok = j < nt - 2
    return prev_ok, next_ok


def _adaln_kernel(c_ref, w_ref, b_ref, o_ref):
    o_ref[...] = _dot_hi(_silu(c_ref[...]), w_ref[...]) + b_ref[...]


def _adaln(cvec, w_mod, b_mod):
    depth, d, n = w_mod.shape
    tn = 1536
    return pl.pallas_call(
        _adaln_kernel,
        grid=(depth, n // tn),
        in_specs=[pl.BlockSpec((SUBLANES, d), lambda i, k: (0, 0)),
                  pl.BlockSpec((None, d, tn), lambda i, k: (i, 0, k)),
                  pl.BlockSpec((None, 1, tn), lambda i, k: (i, 0, k))],
        out_specs=pl.BlockSpec((None, SUBLANES, tn), lambda i, k: (i, 0, k)),
        out_shape=jax.ShapeDtypeStruct((depth, SUBLANES, n), F32),
        compiler_params=_cparams(("parallel", "parallel")),
        name="adaln",
    )(cvec, w_mod, b_mod.reshape(depth, 1, n))


def _gdn_in_kernel(nt, h_ref, hp_ref, hn_ref, mod_ref, g_ref, w_ref, wab_ref, cw_ref,
                   q_ref, k_ref, v_ref, gate_ref, ab_ref, ext_ref):
    prev_ok, next_ok = _halo_valid(nt)
    g, shift, scale = g_ref[...], mod_ref[0:1, :], mod_ref[1:2, :]
    a = _norm_mod(h_ref[...], g, shift, scale)
    a16 = a.astype(BF16)
    ax16 = jnp.concatenate([_norm_mod(hp_ref[...], g, shift, scale), a,
                            _norm_mod(hn_ref[...], g, shift, scale)], axis=0).astype(BF16)
    cw = 512
    pad = GDN_CONV // 2
    nqk = GDN_HEADS * GDN_DK
    nq = 2 * nqk + GDN_HEADS * GDN_DV
    for cc in range(nq // cw):
        cs = slice(cc * cw, (cc + 1) * cw)
        p = _dot(ax16, w_ref[:, cs])
        ext = ext_ref.at[cc]
        ext[0:SUBLANES, :] = jnp.where(prev_ok, p[0:SUBLANES], 0.0)
        ext[SUBLANES:SUBLANES + TM, :] = p[SUBLANES:SUBLANES + TM]
        ext[SUBLANES + TM:2 * SUBLANES + TM, :] = jnp.where(next_ok, p[SUBLANES + TM:], 0.0)
        acc = cw_ref[0:1, cs] * ext[pl.ds(SUBLANES - pad, TM), :]
        for tap in range(1, GDN_CONV):
            acc = acc + cw_ref[tap:tap + 1, cs] * ext[pl.ds(SUBLANES - pad + tap, TM), :]
        y = _silu(acc)
        for hh in range(cw // LANES):
            col = cc * cw + hh * LANES
            yh = y[:, hh * LANES:(hh + 1) * LANES]
            if col < 2 * nqk:
                yh = yh * lax.rsqrt(jnp.sum(yh * yh, axis=-1, keepdims=True) + RMS_EPS)
            if col < nqk:
                q_ref[:, col:col + LANES] = yh.astype(BF16)
            elif col < 2 * nqk:
                k_ref[:, col - nqk:col - nqk + LANES] = yh.astype(BF16)
            else:
                v_ref[:, col - 2 * nqk:col - 2 * nqk + LANES] = yh.astype(BF16)
    for n in range(gate_ref.shape[-1] // cw):
        gate_ref[:, n * cw:(n + 1) * cw] = _dot(a16, w_ref[:, nq + n * cw:nq + (n + 1) * cw]).astype(BF16)
    ab_ref[...] = _dot_hi(a, wab_ref[...])


def _gdn_in(h, mod, g, w_main, w_ab, conv_w):
    b, t, d = h.shape
    nt = t // TM
    nqk = GDN_HEADS * GDN_DK
    nv = GDN_HEADS * GDN_DV
    per = TM // SUBLANES
    prev = pl.BlockSpec((None, SUBLANES, d), lambda bb, j: (bb, jnp.maximum(j * per - 1, 0), 0))
    nxt = pl.BlockSpec((None, SUBLANES, d), lambda bb, j: (bb, jnp.minimum((j + 1) * per, nt * per - 1), 0))
    return pl.pallas_call(
        functools.partial(_gdn_in_kernel, nt),
        grid=(b, nt),
        in_specs=[_tok_spec(d), prev, nxt, _mod_spec(nt, d), _full_spec((1, d)),
                  _full_spec(w_main.shape), _full_spec(w_ab.shape), _full_spec(conv_w.shape)],
        out_specs=[_tok_spec(nqk), _tok_spec(nqk), _tok_spec(nv), _tok_spec(nv), _tok_spec(LANES)],
        out_shape=[jax.ShapeDtypeStruct((b, t, nqk), BF16),
                   jax.ShapeDtypeStruct((b, t, nqk), BF16),
                   jax.ShapeDtypeStruct((b, t, nv), BF16),
                   jax.ShapeDtypeStruct((b, t, nv), BF16),
                   jax.ShapeDtypeStruct((b, t, LANES), F32)],
        scratch_shapes=[pltpu.VMEM(((2 * nqk + nv) // 512, TM + 2 * SUBLANES, 512), F32)],
        compiler_params=_cparams(("parallel", "parallel")),
        name="gdn_in",
    )(h, h, h, mod, g, w_main, w_ab, conv_w)


def _split(x):
    hi = x.astype(BF16)
    return hi, (x - hi.astype(F32)).astype(BF16)


def _dot16(a, b):
    return _dot(a.astype(BF16), b.astype(BF16))


def _dot_hi(a, b):
    ah, al = _split(a)
    bh, bl = _split(b)
    return _dot(ah, bh) + (_dot(ah, bl) + _dot(al, bh))


def _gdn_scan_kernel(reverse, nb, q_ref, k_ref, v_ref, ab_ref, cf_ref, o_ref,
                     s_ref, gc_ref, gct_ref, beta_ref, u_ref, wq_ref, kd_ref, at_ref):
    nchunk = TM // GDN_CHUNK
    c64 = GDN_CHUNK
    scale = GDN_DK ** -0.5
    lane0 = 2 * GDN_HEADS if reverse else 0
    chains = [(b, h) for b in range(nb) for h in range(GDN_HEADS)]

    @pl.when(pl.program_id(0) == 0)
    def _():
        s_ref[...] = jnp.zeros_like(s_ref)

    row = lax.broadcasted_iota(jnp.int32, (TM, TM), 0)
    col = lax.broadcasted_iota(jnp.int32, (TM, TM), 1)
    same = (row // c64) == (col // c64)
    tri = (col >= row) if reverse else (col <= row)
    cum = jnp.where(jnp.logical_and(same, tri), 1.0, 0.0)
    for b in range(nb):
        ab = ab_ref[b]
        beta_ref[b] = jax.nn.sigmoid(ab)
        x = ab + cf_ref[1:2, :]
        softplus = jnp.maximum(x, 0.0) + jnp.log1p(jnp.exp(-jnp.abs(x)))
        g = -cf_ref[0:1, :] * softplus
        gc = _dot_hi(cum, g)
        gc_ref[b] = gc
        for c in range(nchunk):
            blk = gc[c * c64:(c + 1) * c64, :]
            gct_ref[b, c] = jnp.concatenate([blk, blk], axis=0).T

    ri = lax.broadcasted_iota(jnp.int32, (c64, LANES), 0)
    ln = lax.broadcasted_iota(jnp.int32, (c64, LANES), 1)
    ci = ln % c64
    left = ln < c64
    causal = (ri <= ci) if reverse else (ri >= ci)
    strict = (ri < ci) if reverse else (ri > ci)
    blk16 = (ri // 16) == (ci // 16)
    blk32 = (ri // 32) == (ci // 32)
    m_c1 = jnp.logical_and(strict, jnp.logical_and(blk32, jnp.logical_not(blk16)))[:, :c64]
    m_c2 = jnp.logical_and(strict, jnp.logical_not(blk32))[:, :c64]
    m_l0x = jnp.logical_and(jnp.logical_and(strict, blk16), jnp.logical_not(left))
    eye_l = jnp.where(jnp.logical_and(ri == ci, left), 1.0, 0.0)
    eye64 = eye_l[:, :c64]
    strict64 = strict[:, :c64]
    zpad = jnp.zeros((c64, LANES), F32)

    def prep(it, carry):
        st = []
        for b in range(nb):
            for j in range(PREP_CHUNKS):
                c = it * PREP_CHUNKS + j
                r0 = pl.multiple_of(c * c64, c64)
                rows = pl.ds(r0, c64)
                r_last = r0 if reverse else r0 + (c64 - 1)
                gcb = gc_ref[b, rows, :]
                gctb = gct_ref[b, c]
                glb = gc_ref[b, pl.ds(r_last, 1), :]
                betab = beta_ref[b, rows, :]
                for h in range(GDN_HEADS):
                    hs = slice(h * GDN_DK, (h + 1) * GDN_DK)
                    gl = lane0 + SUBLANES + h
                    k16 = k_ref[b, rows, hs]
                    k32 = k16.astype(F32)
                    beta = betab[:, lane0 + h:lane0 + h + 1]
                    gcol = gcb[:, gl:gl + 1]
                    dec2 = jnp.exp(jnp.minimum(gcol - gctb[gl:gl + 1, :], 0.0))
                    kb = k32 * beta
                    kk2 = _dot_nt(kb.astype(BF16), jnp.concatenate([k16, k16], axis=0))
                    ll2 = kk2 * dec2
                    st.append(dict(hs=hs, b=b, h=h, c=c, rows=rows, k32=k32, k16=k16, beta=beta, gcol=gcol,
                                   kb=kb, glast=glb[:, gl:gl + 1], dec=dec2[:, :c64],
                                   ll=ll2[:, :c64], x=jnp.where(m_l0x, -ll2, eye_l)))
        for _ in range(4):
            for s in st:
                x = s["x"]
                s["x"] = jnp.where(left, x, 0.0) + _dot16(x, jnp.concatenate([zpad, x], axis=0))
        for s in st:
            s["d"] = s["x"][:, :c64]
        for mask in (m_c1, m_c2):
            for s in st:
                s["t"] = _dot16(jnp.where(mask, s["ll"], 0.0), s["d"])
            for s in st:
                s["d"] = s["d"] - _dot16(s["d"], s["t"])
        for s in st:
            s["r"] = eye64 - s["d"] - _dot_hi(jnp.where(strict64, s["ll"], 0.0), s["d"])
        for s in st:
            s["d"] = s["d"] + _dot16(s["d"], s["r"])
        for s in st:
            eg = jnp.exp(s["gcol"])
            s["eg"] = eg
            v32 = v_ref[s["b"], s["rows"], s["hs"]].astype(F32)
            s["u"] = _dot_hi(s["d"], v32 * s["beta"])
            s["w"] = _dot16(s["d"], s["kb"] * eg)
        for s in st:
            b, h, hs, c, rows = s["b"], s["h"], s["hs"], s["c"], s["rows"]
            qs = q_ref[b, rows, hs].astype(F32) * scale
            attn = _dot_nt(qs.astype(BF16), s["k16"]) * jnp.where(causal[:, :c64], s["dec"], 0.0)
            u_ref[b, rows, hs] = s["u"]
            wq_ref[b, c, h, 0:c64, :] = s["w"].astype(BF16)
            wq_ref[b, c, h, c64:2 * c64, :] = (qs * s["eg"]).astype(BF16)
            kd_ref[b, rows, hs] = (s["k32"] * jnp.exp(s["glast"] - s["gcol"])).astype(BF16)
            at_ref[b, c, h] = attn.astype(BF16)
        return carry

    lax.fori_loop(0, nchunk // PREP_CHUNKS, prep, 0)

    def scan(step, carry):
        c = (nchunk - 1 - step) if reverse else step
        r0 = pl.multiple_of(c * c64, c64)
        rows = pl.ds(r0, c64)
        r_last = r0 if reverse else r0 + (c64 - 1)
        glb = [gc_ref[b, pl.ds(r_last, 1), :] for b in range(nb)]
        st = []
        for (b, h) in chains:
            s = s_ref[b, h]
            st.append(dict(b=b, h=h, hs=slice(h * GDN_DK, (h + 1) * GDN_DK), s=s,
                           ws=_dot(wq_ref[b, c, h], s.astype(BF16))))
        for d in st:
            v_new = u_ref[d["b"], rows, d["hs"]] - d["ws"][:c64]
            d["vn"] = v_new.astype(BF16)
        for d in st:
            b, h, hs = d["b"], d["h"], d["hs"]
            o = d["ws"][c64:] + _dot(at_ref[b, c, h], d["vn"])
            o_ref[b, rows, hs] = o.astype(BF16)
            egl = jnp.exp(glb[b][:, lane0 + SUBLANES + h:lane0 + SUBLANES + h + 1])
            s_ref[b, h] = d["s"] * egl + _dot_tn(kd_ref[b, rows, hs], d["vn"])
        return carry

    lax.fori_loop(0, nchunk, scan, 0)


def _gdn_scan(q, k, v, ab, cf, reverse):
    b, t, c = q.shape
    nt = t // TM
    nchunk = TM // GDN_CHUNK

    def tile(s):
        lat = (nt - 1 - s) if reverse else (s - 1)
        return jnp.where(s == 0, nt - 1, lat)

    spec = pl.BlockSpec((b, TM, c), lambda s: (0, tile(s), 0))
    return pl.pallas_call(
        functools.partial(_gdn_scan_kernel, reverse, b),
        grid=(nt,),
        in_specs=[spec, spec, spec,
                  pl.BlockSpec((b, TM, LANES), lambda s: (0, tile(s), 0)),
                  _full_spec((SUBLANES, LANES))],
        out_specs=spec,
        out_shape=jax.ShapeDtypeStruct((b, t, c), BF16),
        scratch_shapes=[pltpu.VMEM((b, GDN_HEADS, GDN_DK, GDN_DV), F32),
                        pltpu.VMEM((b, TM, LANES), F32),
                        pltpu.VMEM((b, nchunk, LANES, LANES), F32),
                        pltpu.VMEM((b, TM, LANES), F32),
                        pltpu.VMEM((b, TM, c), F32),
                        pltpu.VMEM((b, nchunk, GDN_HEADS, 2 * GDN_CHUNK, GDN_DK), BF16),
                        pltpu.VMEM((b, TM, c), BF16),
                        pltpu.VMEM((b, nchunk, GDN_HEADS, GDN_CHUNK, GDN_CHUNK), BF16)],
        compiler_params=_cparams(("arbitrary",)),
        name="gdn_scan_bwd" if reverse else "gdn_scan_fwd",
    )(q, k, v, ab, cf)


def _gdn_out_kernel(of_ref, ob_ref, gate_ref, h_ref, mod_ref, on_ref, w_ref, g_ref, out_ref, y_ref):
    for hh in range(GDN_HEADS):
        hs = slice(hh * GDN_DV, (hh + 1) * GDN_DV)
        o = of_ref[:, hs].astype(F32) + ob_ref[:, hs].astype(F32)
        gt = gate_ref[:, hs].astype(F32)
        y_ref[:, hs] = (_rms(o) * on_ref[...] * _silu(gt)).astype(BF16)
    out = _dot(y_ref[...], w_ref[...])
    out_ref[...] = _residual(h_ref[...], out, g_ref[...], mod_ref[2:3, :])


def _gdn_out(o_f, o_b, gate, h, mod, o_norm, w_out, g):
    b, t, d = h.shape
    nt = t // TM
    c = o_f.shape[-1]
    return pl.pallas_call(
        _gdn_out_kernel,
        grid=(b, nt),
        in_specs=[_tok_spec(c), _tok_spec(c), _tok_spec(c), _tok_spec(d), _mod_spec(nt, d),
                  _full_spec((1, GDN_DV)), _full_spec(w_out.shape), _full_spec((1, d))],
        out_specs=_tok_spec(d),
        out_shape=jax.ShapeDtypeStruct((b, t, d), F32),
        scratch_shapes=[pltpu.VMEM((TM, c), BF16)],
        compiler_params=_cparams(("parallel", "parallel")),
        name="gdn_out",
    )(o_f, o_b, gate, h, mod, o_norm, w_out, g)


def _conf_kernel(nt, h_ref, hp_ref, hn_ref, mod_ref, g0_ref, win_ref, bin_ref, dw_ref, dwb_ref,
                 lng_ref, lnb_ref, w_ref, b_ref, g1_ref, out_ref, ext_ref, sh_ref, y_ref):
    prev_ok, next_ok = _halo_valid(nt)
    g0, shift, scale = g0_ref[...], mod_ref[0:1, :], mod_ref[1:2, :]
    h = h_ref[...]
    ax16 = jnp.concatenate([_norm_mod(hp_ref[...], g0, shift, scale), _norm_mod(h, g0, shift, scale),
                            _norm_mod(hn_ref[...], g0, shift, scale)], axis=0).astype(BF16)
    d = h.shape[-1]
    cw = 512
    for n in range(d // cw):
        cs = slice(n * cw, (n + 1) * cw)
        gs = slice(d + n * cw, d + (n + 1) * cw)
        u = (_dot(ax16, win_ref[:, cs]) + bin_ref[:, cs]) * jax.nn.sigmoid(_dot(ax16, win_ref[:, gs]) + bin_ref[:, gs])
        ext_ref[0:HALO, cs] = jnp.where(prev_ok, u[0:HALO], 0.0)
        ext_ref[HALO:HALO + TM, cs] = u[HALO:HALO + TM]
        ext_ref[HALO + TM:2 * HALO + TM, cs] = jnp.where(next_ok, u[HALO + TM:], 0.0)
    pad = CONF_CONV // 2
    for r in range(1, SUBLANES):
        sh_ref[r - 1] = ext_ref[pl.ds(r, sh_ref.shape[1]), :]
    rw, cw = 64, 256
    for cc in range(d // cw):
        cs = slice(cc * cw, (cc + 1) * cw)
        for rr in range(TM // rw):
            acc = None
            for tap in range(CONF_CONV):
                off = HALO - pad + tap
                r = off % SUBLANES
                src = ext_ref if r == 0 else sh_ref.at[r - 1]
                term = dw_ref[tap:tap + 1, cs] * src[pl.ds(off - r + rr * rw, rw), cs]
                acc = term if acc is None else acc + term
            y_ref[rr * rw:(rr + 1) * rw, cs] = acc + dwb_ref[:, cs]
    y = y_ref[...]
    yc = y - jnp.mean(y, axis=-1, keepdims=True)
    yn = yc * lax.rsqrt(jnp.mean(yc * yc, axis=-1, keepdims=True) + LN_EPS) * lng_ref[...] + lnb_ref[...]
    out = _dot(_silu(yn).astype(BF16), w_ref[...]) + b_ref[...]
    out_ref[...] = _residual(h, out, g1_ref[...], mod_ref[2:3, :])


def _conformer(h, mod, g0, w_in, b_in, dw, dw_b, ln_g, ln_b, w_out, b_out, g1):
    b, t, d = h.shape
    nt = t // TM
    prev, nxt = _halo_specs(nt, d)
    vec = _full_spec((1, d))
    return pl.pallas_call(
        functools.partial(_conf_kernel, nt),
        grid=(b, nt),
        in_specs=[_tok_spec(d), prev, nxt, _mod_spec(nt, d), vec, _full_spec(w_in.shape), _full_spec(b_in.shape),
                  _full_spec(dw.shape), vec, vec, vec, _full_spec(w_out.shape), vec, vec],
        out_specs=_tok_spec(d),
        out_shape=jax.ShapeDtypeStruct((b, t, d), F32),
        scratch_shapes=[pltpu.VMEM((TM + 2 * HALO, d), F32),
                        pltpu.VMEM((SUBLANES - 1, TM + 2 * HALO - SUBLANES, d), F32),
                        pltpu.VMEM((TM, d), F32)],
        compiler_params=_cparams(("parallel", "parallel")),
        name="conformer",
    )(h, h, h, mod, g0, w_in, b_in, dw, dw_b, ln_g, ln_b, w_out, b_out, g1)


def _mla_proj_kernel(h_ref, mod_ref, g_ref, win_ref, qn_ref, kvn_ref, wuq_ref, wukv_ref, cs_ref,
                     q_ref, k_ref, v_ref):
    a16 = _norm_mod(h_ref[...], g_ref[...], mod_ref[0:1, :], mod_ref[1:2, :]).astype(BF16)
    p = _dot(a16, win_ref[...])
    cq = _rms(p[:, :MLA_Q_RANK]) * qn_ref[...]
    ckv = _rms(p[:, MLA_Q_RANK:MLA_Q_RANK + MLA_KV_RANK]) * kvn_ref[...]
    kr = p[:, MLA_Q_RANK + MLA_KV_RANK:]
    cs = cs_ref[...]
    kx = kr * cs
    krot = (kx + pltpu.roll(kx, MLA_ROPE, 1)).astype(BF16)
    scale = (MLA_NOPE + MLA_ROPE) ** -0.5 * LOG2E
    cq16 = cq.astype(BF16)
    ckv16 = ckv.astype(BF16)
    hw = MLA_NOPE + 2 * MLA_ROPE
    for hh in range(MLA_HEADS):
        qh = _dot(cq16, wuq_ref[:, hh * hw:(hh + 1) * hw]) * scale
        q_ref[:, hh * hw:hh * hw + MLA_NOPE] = qh[:, :MLA_NOPE].astype(BF16)
        q_ref[:, hh * hw + MLA_NOPE:(hh + 1) * hw] = (qh[:, MLA_NOPE:] * cs).astype(BF16)
        k_ref[:, hh * hw:hh * hw + MLA_NOPE] = _dot(
            ckv16, wukv_ref[:, hh * MLA_NOPE:(hh + 1) * MLA_NOPE]).astype(BF16)
        k_ref[:, hh * hw + MLA_NOPE:(hh + 1) * hw] = krot
        nk = MLA_HEADS * MLA_NOPE
        v_ref[:, 2 * hh * MLA_V:(2 * hh + 1) * MLA_V] = _dot(
            ckv16, wukv_ref[:, nk + hh * MLA_V:nk + (hh + 1) * MLA_V]).astype(BF16)
        v_ref[:, (2 * hh + 1) * MLA_V:(2 * hh + 2) * MLA_V] = jnp.ones((TM, MLA_V), BF16)


def _mla_proj(h, mod, g, w_in, q_norm, kv_norm, w_uq, w_ukv, cs_tab):
    b, t, d = h.shape
    nt = t // TM
    hw = MLA_NOPE + 2 * MLA_ROPE
    return pl.pallas_call(
        _mla_proj_kernel,
        grid=(b, nt),
        in_specs=[_tok_spec(d), _mod_spec(nt, d), _full_spec((1, d)), _full_spec(w_in.shape),
                  _full_spec(q_norm.shape), _full_spec(kv_norm.shape),
                  _full_spec(w_uq.shape), _full_spec(w_ukv.shape),
                  pl.BlockSpec((TM, LANES), lambda bb, j: (j, 0))],
        out_specs=[_tok_spec(MLA_HEADS * hw), _tok_spec(MLA_HEADS * hw), _tok_spec(MLA_HEADS * 2 * MLA_V)],
        out_shape=[jax.ShapeDtypeStruct((b, t, MLA_HEADS * hw), BF16),
                   jax.ShapeDtypeStruct((b, t, MLA_HEADS * hw), BF16),
                   jax.ShapeDtypeStruct((b, t, MLA_HEADS * 2 * MLA_V), BF16)],
        compiler_params=_cparams(("parallel", "parallel")),
        name="mla_proj",
    )(h, mod, g, w_in, q_norm, kv_norm, w_uq, w_ukv, cs_tab)


def _attn_kernel(tk, q_ref, k_ref, v_ref, o_ref, m_ref, acc_ref):
    q = q_ref[...]
    for i in range(k_ref.shape[0] // tk):
        rows = slice(i * tk, (i + 1) * tk)
        s = _dot_nt(q, k_ref[rows, :])
        m_blk = jnp.max(s, axis=-1, keepdims=True)
        if i == 0:
            m_new = m_blk
            acc_ref[...] = _dot(jnp.exp2(s - m_new).astype(BF16), v_ref[rows, :])
        else:
            m_old = m_ref[...]
            m_new = jnp.maximum(m_old, m_blk)
            acc_ref[...] = jnp.exp2(m_old - m_new) * acc_ref[...] + _dot(
                jnp.exp2(s - m_new).astype(BF16), v_ref[rows, :])
        m_ref[...] = m_new
    acc = acc_ref[...]
    o_ref[...] = (acc[:, :MLA_V] / acc[:, MLA_V:]).astype(BF16)


def _attention(q, k, v, tq, q_rows, q_row0, k_rows, k_row0, name):
    b = q.shape[0]
    hw = MLA_NOPE + 2 * MLA_ROPE
    tk = next(c for c in (768, 512, 256) if k_rows % c == 0)
    q0 = q_row0 // tq
    k0 = k_row0 // k_rows
    return pl.pallas_call(
        functools.partial(_attn_kernel, tk),
        grid=(b, MLA_HEADS, q_rows // tq),
        in_specs=[pl.BlockSpec((None, tq, hw), lambda bb, hh, i: (bb, q0 + i, hh)),
                  pl.BlockSpec((None, k_rows, hw), lambda bb, hh, i: (bb, k0, hh)),
                  pl.BlockSpec((None, k_rows, 2 * MLA_V), lambda bb, hh, i: (bb, k0, hh))],
        out_specs=pl.BlockSpec((None, tq, MLA_V), lambda bb, hh, i: (bb, i, hh)),
        out_shape=jax.ShapeDtypeStruct((b, q_rows, MLA_HEADS * MLA_V), BF16),
        scratch_shapes=[pltpu.VMEM((tq, 1), F32), pltpu.VMEM((tq, 2 * MLA_V), F32)],
        compiler_params=_cparams(("parallel", "parallel", "arbitrary")),
        name=name,
    )(q, k, v)


def _proj_out_kernel(o_ref, w_ref, h_ref, mod_ref, g_ref, out_ref):
    out = _dot(o_ref[...], w_ref[...])
    out_ref[...] = _residual(h_ref[...], out, g_ref[...], mod_ref[2:3, :])


def _proj_out(o, w, h, mod, g):
    b, t, d = h.shape
    nt = t // TM
    return pl.pallas_call(
        _proj_out_kernel,
        grid=(b, nt),
        in_specs=[_tok_spec(o.shape[-1]), _full_spec(w.shape), _tok_spec(d), _mod_spec(nt, d),
                  _full_spec((1, d))],
        out_specs=_tok_spec(d),
        out_shape=jax.ShapeDtypeStruct((b, t, d), F32),
        compiler_params=_cparams(("parallel", "parallel")),
        name="mla_out",
    )(o, w, h, mod, g)


def _ffn_kernel(fc, tps, nt, h_ref, mod_ref, g2_ref, wg_ref, wu_ref, wd_ref, g3_ref, out_ref, acc_ref):
    b = pl.program_id(0)
    j0 = pl.program_id(1) * tps
    hs = [h_ref[s * TM:(s + 1) * TM, :] for s in range(tps)]
    mods = [mod_ref[jnp.where(j0 + s == nt - 1, 0, b + 1)] for s in range(tps)]
    a16 = jnp.concatenate([_norm_mod(hs[s], g2_ref[...], mods[s][3:4, :], mods[s][4:5, :])
                           for s in range(tps)], axis=0).astype(BF16)
    for c in range(wg_ref.shape[-1] // fc):
        fs = slice(c * fc, (c + 1) * fc)
        act = (_silu(_dot(a16, wg_ref[:, fs])) * _dot(a16, wu_ref[:, fs])).astype(BF16)
        part = _dot(act, wd_ref[fs, :])
        if c == 0:
            acc_ref[...] = part
        else:
            acc_ref[...] += part
    for s in range(tps):
        rows = slice(s * TM, (s + 1) * TM)
        out_ref[rows, :] = _residual(hs[s], acc_ref[rows, :], g3_ref[...], mods[s][5:6, :])


def _ffn(h, mod, g2, w_gate, w_up, w_down, g3, n_tiles):
    b, t, d = h.shape
    nt = t // TM
    tps = 3 if n_tiles % 3 == 0 else 1
    tok = pl.BlockSpec((None, tps * TM, d), lambda bb, j: (bb, j, 0))

    def resident(shape):
        return pl.BlockSpec(shape, lambda *_: (0,) * len(shape), pipeline_mode=pl.Buffered(1))

    return pl.pallas_call(
        functools.partial(_ffn_kernel, 1408, tps, nt),
        grid=(b, n_tiles // tps),
        in_specs=[tok, _full_spec(mod.shape), _full_spec((1, d)), resident(w_gate.shape),
                  resident(w_up.shape), resident(w_down.shape), _full_spec((1, d))],
        out_specs=tok,
        out_shape=jax.ShapeDtypeStruct((b, n_tiles * TM, d), F32),
        scratch_shapes=[pltpu.VMEM((tps * TM, d), F32)],
        compiler_params=_cparams(("parallel", "parallel")),
        name="ffn",
    )(h, mod, g2, w_gate, w_up, w_down, g3)


def _moe_route_kernel(h_ref, mod_ref, g2_ref, wr_ref, a_ref, rt_ref):
    lane = lax.broadcasted_iota(jnp.int32, (TM, LANES), 1).astype(F32)
    a = _norm_mod(h_ref[...], g2_ref[...], mod_ref[3:4, :], mod_ref[4:5, :])
    for s in range(SUBLANES):
        a_ref[pl.ds(s, TM, stride=SUBLANES), :] = a[:, s * LANES:(s + 1) * LANES]
    logits = jnp.where(lane < N_EXPERTS, _dot_hi(a, wr_ref[...]), -jnp.inf)
    m1 = jnp.max(logits, axis=-1, keepdims=True)
    i1 = jnp.min(jnp.where(logits == m1, lane, float(LANES)), axis=-1, keepdims=True)
    rest = jnp.where(lane == i1, -jnp.inf, logits)
    m2 = jnp.max(rest, axis=-1, keepdims=True)
    i2 = jnp.min(jnp.where(rest == m2, lane, float(LANES)), axis=-1, keepdims=True)
    e2 = jnp.exp(m2 - m1)
    w1 = 1.0 / (1.0 + e2)
    rt_ref[...] = (jnp.where(lane == 0.0, i1, 0.0) + jnp.where(lane == 1.0, i2, 0.0)
                   + jnp.where(lane == 2.0, w1, 0.0) + jnp.where(lane == 3.0, e2 * w1, 0.0))


def _moe_route(h, mod, g2, w_router, n_tiles):
    b, t, d = h.shape
    nt = t // TM
    return pl.pallas_call(
        _moe_route_kernel,
        grid=(b, n_tiles),
        in_specs=[_tok_spec(d), _mod_spec(nt, d), _full_spec((1, d)), _full_spec(w_router.shape)],
        out_specs=[pl.BlockSpec((TM * SUBLANES, LANES), lambda bb, j: (bb * n_tiles + j, 0)),
                   _tok_spec(LANES)],
        out_shape=[jax.ShapeDtypeStruct((b * n_tiles * TM * SUBLANES, LANES), F32),
                   jax.ShapeDtypeStruct((b, n_tiles * TM, LANES), F32)],
        compiler_params=_cparams(("parallel", "parallel")),
        name="moe_route",
    )(h, mod, g2, w_router)


def _gather_tiles(idx_ref, base, n, src_hbm, dst, sem):
    def body(r2, carry):
        for j in range(2):
            r = r2 * 2 + j
            row = pl.multiple_of(idx_ref[base + r], SUBLANES)
            pltpu.make_async_copy(src_hbm.at[pl.ds(row, SUBLANES), :],
                                  dst.at[pl.ds(pl.multiple_of(r * SUBLANES, SUBLANES), SUBLANES), :],
                                  sem).start(priority=j)
        return carry
    lax.fori_loop(0, n // 2, body, 0, unroll=4)


def _wait_tiles(n, src_hbm, dst, sem):
    pltpu.make_async_copy(src_hbm.at[pl.ds(0, n * SUBLANES), :], dst, sem).wait()


def _untile(buf, n):
    return jnp.concatenate([buf[pl.ds(s, n, stride=SUBLANES), :] for s in range(SUBLANES)], axis=1)


def _moe_dispatch_kernel(dst_ref, last_ref, nact_ref, a_ref, x_hbm, stage_ref, sem, zsem):
    i = pl.program_id(0)
    n = pl.num_programs(0)
    rows = TM * SUBLANES
    slot = i % 2
    n_tiles_max = x_hbm.shape[0] // rows

    @pl.when(i == 0)
    def _():
        zero = stage_ref.at[1]
        zero[...] = jnp.zeros_like(zero)

        def zero_tile(t):
            return pltpu.make_async_copy(zero, x_hbm.at[pl.ds(pl.multiple_of(t * rows, rows), rows), :], zsem)

        for e in range(N_EXPERTS):
            zero_tile(last_ref[e]).start()
            pl.when(nact_ref[0] + e < n_tiles_max)(lambda e=e: zero_tile(nact_ref[0] + e).start())
        for e in range(N_EXPERTS):
            zero_tile(0).wait()
            pl.when(nact_ref[0] + e < n_tiles_max)(lambda: zero_tile(0).wait())

    stage = stage_ref.at[slot]
    stage[...] = a_ref[...]

    def body(r, carry):
        src = pl.multiple_of(r * SUBLANES, SUBLANES)
        for k in range(2):
            dst = pl.multiple_of(dst_ref[(i * TM + r) * 2 + k], SUBLANES)
            pltpu.make_async_copy(stage.at[pl.ds(src, SUBLANES), :], x_hbm.at[pl.ds(dst, SUBLANES), :],
                                  sem.at[slot]).start(priority=k)
        return carry
    lax.fori_loop(0, TM, body, 0, unroll=4)

    def wait(sl):
        for k in range(2):
            pltpu.make_async_copy(stage_ref.at[sl], x_hbm.at[pl.ds(0, rows), :], sem.at[sl]).wait()

    @pl.when(i > 0)
    def _():
        wait(1 - slot)

    @pl.when(i == n - 1)
    def _():
        wait(slot)


def _moe_dispatch(dst_rows, last_tile, n_active, a2d, n_rows):
    n_steps = a2d.shape[0] // (TM * SUBLANES)
    return pl.pallas_call(
        _moe_dispatch_kernel,
        grid_spec=pltpu.PrefetchScalarGridSpec(
            num_scalar_prefetch=3,
            grid=(n_steps,),
            in_specs=[pl.BlockSpec((TM * SUBLANES, LANES), lambda i, ds, lt, na: (i, 0))],
            out_specs=pl.BlockSpec(memory_space=pl.ANY),
            scratch_shapes=[pltpu.VMEM((2, TM * SUBLANES, LANES), F32), pltpu.SemaphoreType.DMA((2,)),
                            pltpu.SemaphoreType.DMA(())]),
        out_shape=jax.ShapeDtypeStruct((n_rows * SUBLANES, LANES), F32),
        compiler_params=_cparams(("arbitrary",)),
        name="moe_dispatch",
    )(dst_rows, last_tile, n_active, a2d)


def _moe_experts_kernel(te_ref, nact_ref, x_ref, wg_ref, wu_ref, wd_ref, y_ref, wg16, wu16, wd16):
    i = pl.program_id(0)

    @pl.when(jnp.logical_or(i == 0, te_ref[i] != te_ref[jnp.maximum(i - 1, 0)]))
    def _():
        wg16[...] = wg_ref[...].astype(BF16)
        wu16[...] = wu_ref[...].astype(BF16)
        wd16[...] = wd_ref[...].astype(BF16)

    @pl.when(i < nact_ref[0])
    def _():
        x16 = _untile(x_ref, TM).astype(BF16)
        f = wg_ref.shape[-1]
        acc = None
        for c0 in range(0, f, 1408):
            fs = slice(c0, min(c0 + 1408, f))
            act = (_silu(_dot(x16, wg16[:, fs])) * _dot(x16, wu16[:, fs])).astype(BF16)
            part = _dot(act, wd16[fs, :])
            acc = part if acc is None else acc + part
        for s in range(SUBLANES):
            y_ref[pl.ds(s, TM, stride=SUBLANES), :] = acc[:, s * LANES:(s + 1) * LANES]

    @pl.when(i >= nact_ref[0])
    def _():
        y_ref[...] = jnp.zeros_like(y_ref)


def _moe_experts(tile_expert, n_active, x2d, w_gate, w_up, w_down, layer):
    n_tiles = tile_expert.shape[0]
    _, _, d, f = w_gate.shape
    wspec_in = pl.BlockSpec((None, None, d, f), lambda i, te, na: (layer, te[i], 0, 0))
    wspec_out = pl.BlockSpec((None, None, f, d), lambda i, te, na: (layer, te[i], 0, 0))
    return pl.pallas_call(
        _moe_experts_kernel,
        grid_spec=pltpu.PrefetchScalarGridSpec(
            num_scalar_prefetch=2,
            grid=(n_tiles,),
            in_specs=[pl.BlockSpec((TM * SUBLANES, LANES), lambda i, te, na: (jnp.minimum(i, na[0] - 1), 0)),
                      wspec_in, wspec_in, wspec_out],
            out_specs=pl.BlockSpec((TM * SUBLANES, LANES), lambda i, te, na: (i, 0)),
            scratch_shapes=[pltpu.VMEM((d, f), BF16), pltpu.VMEM((d, f), BF16), pltpu.VMEM((f, d), BF16)]),
        out_shape=jax.ShapeDtypeStruct((n_tiles * TM * SUBLANES, LANES), F32),
        compiler_params=_cparams(("arbitrary",)),
        name="moe_experts",
    )(tile_expert, n_active, x2d, w_gate, w_up, w_down)


def _moe_combine_kernel(n_tiles, dst_ref, y_hbm, rt_ref, h_ref, mod_ref, g3_ref, out_ref, ybuf, sem):
    step = pl.program_id(0) * n_tiles + pl.program_id(1)
    nsteps = pl.num_programs(0) * n_tiles

    @pl.when(step == 0)
    def _():
        _gather_tiles(dst_ref, 0, 2 * TM, y_hbm, ybuf.at[0], sem.at[0])

    @pl.when(step + 1 < nsteps)
    def _():
        nxt = (step + 1) % 2
        _gather_tiles(dst_ref, (step + 1) * 2 * TM, 2 * TM, y_hbm, ybuf.at[nxt], sem.at[nxt])

    cur = step % 2
    _wait_tiles(2 * TM, y_hbm, ybuf.at[cur], sem.at[cur])
    yb = ybuf.at[cur]
    w1 = rt_ref[:, 2:3]
    w2 = rt_ref[:, 3:4]
    f = jnp.concatenate(
        [w1 * yb[pl.ds(s, TM, stride=2 * SUBLANES), :] + w2 * yb[pl.ds(SUBLANES + s, TM, stride=2 * SUBLANES), :]
         for s in range(SUBLANES)], axis=1)
    out_ref[...] = _residual(h_ref[...], f, g3_ref[...], mod_ref[5:6, :])


def _moe_combine(dst_rows, y2d, rt, h, mod, g3, n_tiles):
    b, t, d = h.shape
    nt = t // TM
    tok = lambda c: pl.BlockSpec((None, TM, c), lambda bb, j, ds: (bb, j, 0))
    return pl.pallas_call(
        functools.partial(_moe_combine_kernel, n_tiles),
        grid_spec=pltpu.PrefetchScalarGridSpec(
            num_scalar_prefetch=1,
            grid=(b, n_tiles),
            in_specs=[pl.BlockSpec(memory_space=pl.ANY), tok(LANES), tok(d),
                      pl.BlockSpec((None, SUBLANES, d),
                                   lambda bb, j, ds: (jnp.where(j == nt - 1, 0, bb + 1), 0, 0)),
                      pl.BlockSpec((1, d), lambda bb, j, ds: (0, 0))],
            out_specs=tok(d),
            scratch_shapes=[pltpu.VMEM((2, 2 * TM * SUBLANES, LANES), F32), pltpu.SemaphoreType.DMA((2,))]),
        out_shape=jax.ShapeDtypeStruct((b, n_tiles * TM, d), F32),
        compiler_params=_cparams(("arbitrary", "arbitrary")),
        name="moe_combine",
    )(dst_rows, y2d, rt, h, mod, g3)


def _moe(h, mod, g2, w_router, w_gate, w_up, w_down, layer, g3, n_tiles):
    b = h.shape[0]
    n_tok = b * n_tiles * TM
    a2d, rt = _moe_route(h, mod, g2, w_router, n_tiles)
    expert = rt.reshape(n_tok, LANES)[:, :2].astype(jnp.int32).reshape(-1)
    onehot = (expert[:, None] == jnp.arange(N_EXPERTS)[None, :]).astype(jnp.int32)
    rank = jnp.sum((jnp.cumsum(onehot, axis=0) - onehot) * onehot, axis=1)
    count = jnp.sum(onehot, axis=0)
    padded = (count + TM - 1) // TM * TM
    ends = jnp.cumsum(padded)
    dest = (ends - padded)[expert] + rank
    n_tiles_max = 2 * n_tok // TM + N_EXPERTS
    tile_start = jnp.arange(n_tiles_max, dtype=jnp.int32) * TM
    tile_expert = jnp.minimum(jnp.sum(ends[None, :] <= tile_start[:, None], axis=1), N_EXPERTS - 1)
    n_active = (ends[-1] // TM).reshape(1).astype(jnp.int32)
    last_tile = jnp.maximum(ends // TM - 1, 0).astype(jnp.int32)
    dst_rows = (dest * SUBLANES).astype(jnp.int32)
    x2d = _moe_dispatch(dst_rows, last_tile, n_active, a2d, n_tiles_max * TM)
    y2d = _moe_experts(tile_expert.astype(jnp.int32), n_active, x2d, w_gate, w_up, w_down, layer)
    return _moe_combine(dst_rows, y2d, rt, h, mod, g3, n_tiles)


def _pad_rows(a, rows):
    return jnp.pad(a, ((0, rows - a.shape[0]),) + ((0, 0),) * (a.ndim - 1))


def _rope_table(seq, ctx_len):
    n_freq = MLA_ROPE // 4
    t = jnp.arange(seq)
    pos = jnp.stack([(t // GRID_W).astype(F32), (t % GRID_W).astype(F32)], axis=1)
    inv = ROPE_THETA ** (-jnp.arange(n_freq, dtype=F32) / n_freq)
    ang = pos[:, :, None] * inv
    cos = jnp.repeat(jnp.cos(ang)[:, :, None, :], 2, axis=2).reshape(seq, MLA_ROPE)
    sin = jnp.stack([-jnp.sin(ang), jnp.sin(ang)], axis=2).reshape(seq, MLA_ROPE)
    lat = jnp.concatenate([cos, sin], axis=1)
    ctx = jnp.concatenate([jnp.ones((ctx_len, MLA_ROPE), F32), jnp.zeros((ctx_len, MLA_ROPE), F32)], axis=1)
    return jnp.concatenate([lat, ctx], axis=0)


def _swap_halves(w):
    shp = w.shape
    return jnp.flip(w.reshape(shp[:-1] + (2, 2, MLA_ROPE // 4)), axis=-2).reshape(shp)


def kernel(x, c, ctx, c_ctx, w_mod, b_mod, norm_g, gdn_w_in, gdn_conv, gdn_a_log, gdn_dt_bias, gdn_o_norm, gdn_w_out, conf_w_in, conf_b_in, conf_dw, conf_dw_b, conf_ln_g, conf_ln_b, conf_w_out, conf_b_out, mla_w_in, mla_q_norm, mla_kv_norm, mla_w_uq, mla_w_ukv, mla_w_o, ffn_w_gate, ffn_w_up, ffn_w_down, moe_router, moe_w_gate, moe_w_up, moe_w_down):
    bsz, seq, d = x.shape
    ctx_len = ctx.shape[1]
    depth = w_mod.shape[0]
    assert ctx_len == TM and seq % (4 * TM) == 0 and bsz + 1 <= SUBLANES
    nt = (seq + ctx_len) // TM
    n_lat = seq // TM

    cvec = _pad_rows(jnp.concatenate([c_ctx[None, :], c], axis=0), SUBLANES)
    mods = _adaln(cvec, w_mod, b_mod)
    mods = mods.reshape(depth, SUBLANES, N_MOD, d)[:, :bsz + 1]
    mods = jnp.pad(mods, ((0, 0), (0, 0), (0, SUBLANES - N_MOD), (0, 0)))

    h = jnp.concatenate([x, ctx], axis=1)
    rope_cs = _rope_table(seq, ctx_len)

    n_mix = 3
    for i in range(depth):
        last = i == depth - 1
        mod = mods[i]
        g = norm_g[i].reshape(4, 1, d)
        kind, j = i % n_mix, i // n_mix
        if kind == 0:
            nqk = GDN_HEADS * GDN_DK
            nmain = 2 * nqk + 2 * GDN_HEADS * GDN_DV
            w_in = gdn_w_in[j]
            w_main = w_in[:, :nmain].astype(BF16)
            w_tail = w_in[:, nmain:]
            nh = GDN_HEADS
            w_ab = jnp.concatenate([w_tail[:, (2 * part + dr) * nh:(2 * part + dr + 1) * nh]
                                    for dr in range(2) for part in range(2)], axis=1)
            w_ab = jnp.pad(w_ab, ((0, 0), (0, LANES - 4 * nh)))

            def decay_lanes(p):
                z = jnp.zeros((nh,), F32)
                return jnp.pad(jnp.concatenate([z, p[0], z, p[1]]), (0, LANES - 4 * nh))

            cf = _pad_rows(jnp.stack([decay_lanes(jnp.exp(gdn_a_log[j])), decay_lanes(gdn_dt_bias[j])]),
                           SUBLANES)
            q, k, v, gate, ab = _gdn_in(h, mod, g[0], w_main, w_ab, _pad_rows(gdn_conv[j], SUBLANES))
            o_f = _gdn_scan(q, k, v, ab, cf, reverse=False)
            o_b = _gdn_scan(q, k, v, ab, cf, reverse=True)
            h = _gdn_out(o_f, o_b, gate, h, mod, gdn_o_norm[j].reshape(1, -1),
                         gdn_w_out[j].astype(BF16), g[1])
        elif kind == 1:
            h = _conformer(h, mod, g[0], conf_w_in[j].astype(BF16), conf_b_in[j].reshape(1, -1),
                           _pad_rows(conf_dw[j], 32), conf_dw_b[j].reshape(1, -1),
                           conf_ln_g[j].reshape(1, -1), conf_ln_b[j].reshape(1, -1),
                           conf_w_out[j].astype(BF16), conf_b_out[j].reshape(1, -1), g[1])
        else:
            w_in = mla_w_in[j]
            w_kr = w_in[:, MLA_Q_RANK + MLA_KV_RANK:]
            w_in2 = jnp.concatenate([w_in, _swap_halves(w_kr)], axis=1).astype(BF16)
            hq = MLA_NOPE + MLA_ROPE
            w_uq = mla_w_uq[j].reshape(MLA_Q_RANK, MLA_HEADS, hq)
            w_uq2 = jnp.concatenate([w_uq, _swap_halves(w_uq[..., MLA_NOPE:])], axis=-1)
            w_uq2 = w_uq2.reshape(MLA_Q_RANK, -1).astype(BF16)
            w_ukv = mla_w_ukv[j].reshape(MLA_KV_RANK, MLA_HEADS, MLA_NOPE + MLA_V)
            w_ukv2 = jnp.concatenate([w_ukv[..., :MLA_NOPE].reshape(MLA_KV_RANK, -1),
                                      w_ukv[..., MLA_NOPE:].reshape(MLA_KV_RANK, -1)], axis=1).astype(BF16)
            qa, ka, va = _mla_proj(h, mod, g[0], w_in2, mla_q_norm[j].reshape(1, -1),
                                   mla_kv_norm[j].reshape(1, -1), w_uq2, w_ukv2, rope_cs)
            o_l = _attention(qa, ka, va, 4 * TM, seq, 0, seq + ctx_len, 0, "attn_latent")
            o_c = _attention(qa, ka, va, TM, ctx_len, seq, ctx_len, seq, "attn_context")
            o = jnp.concatenate([o_l, o_c], axis=1)
            h = _proj_out(o, mla_w_o[j].astype(BF16), h, mod, g[1])

        n_tiles = n_lat if last else nt
        f = i // 2
        if i % 2 == 0:
            h = _ffn(h, mod, g[2], ffn_w_gate[f].astype(BF16), ffn_w_up[f].astype(BF16),
                     ffn_w_down[f].astype(BF16), g[3], n_tiles)
        else:
            w_r = jnp.pad(moe_router[f], ((0, 0), (0, LANES - N_EXPERTS)))
            h = _moe(h, mod, g[2], w_r, moe_w_gate, moe_w_up, moe_w_down, f, g[3], n_tiles)
    return h
```
